```python
import math
import jax
import jax.numpy as jnp
from jax import lax
import numpy as np

D_MODEL = 4096
BATCH = 1
SEQ = 8192
DEPTH = 4

GRID_W = 64
CTX_LEN = 256
HEAD_DIM = 128
W_MIX = 3 * D_MODEL // 8
N_Q_HEADS = W_MIX // HEAD_DIM
N_KV_HEADS = N_Q_HEADS // 3
GQA_GROUP = N_Q_HEADS // N_KV_HEADS
KV_W = N_KV_HEADS * HEAD_DIM
Q_BLOCK = 128
ROPE_THETA = 10000.0
LRU_BLOCKS = W_MIX // HEAD_DIM
LRU_BLOCK_W = W_MIX // LRU_BLOCKS
LRU_C = 8.0
CONV_A = 4
CONV_B = 3
HYENA_ORDER = 2
HYENA_BANDS = 16
HYENA_EMB = 2 * HYENA_BANDS + 1
HYENA_HIDDEN = 64
HYENA_FAST_DECAY = 0.3
HYENA_SLOW_DECAY = 1.5
HYENA_TARGET = 1e-2
D_FF = D_MODEL
MOD_RANK = 256
N_MOD = 9
N_BRANCH = 3
EPS = 1e-6

COL_AX = 0
COL_CK = COL_AX + W_MIX
COL_CV = COL_CK + KV_W
COL_AG = COL_CV + KV_W
COL_B = COL_AG + W_MIX
COL_CQ = COL_B + 3 * W_MIX
COL_G = COL_CQ + W_MIX
N_IN = COL_G + N_BRANCH * D_MODEL

kernel_name = 'hybrid_rglru_hyena_gqa_dit'


def rms_norm(x, g):
    xf = x.astype(jnp.float32)
    y = xf * lax.rsqrt(jnp.mean(xf * xf, axis=-1, keepdims=True) + EPS)
    return (y * g.astype(jnp.float32)).astype(x.dtype)


def modulate(x, shift, scale):
    return x * (1.0 + scale) + shift


def ada_modulation(s, w_down, w_up, b):
    m = (s @ w_down) @ w_up + b
    return m.reshape(s.shape[0], N_MOD, 1, D_MODEL)


def swiglu(x, w_gu, w_down):
    a, b = jnp.split(x @ w_gu, 2, axis=-1)
    return (jax.nn.silu(a) * b) @ w_down


def ffn_sublayer(x, shift, scale, gate, g_norm, w_gu, w_down):
    return x + 0.5 * gate * swiglu(modulate(rms_norm(x, g_norm), shift, scale), w_gu, w_down)


def dw_conv(x, w):
    K = w.shape[0]
    L = x.shape[1]
    left = (K - 1) // 2
    xp = jnp.pad(x, ((0, 0), (left, K - 1 - left), (0, 0)))
    y = xp[:, 0:L] * w[0]
    for k in range(1, K):
        y = y + xp[:, k:k + L] * w[k]
    return y


def axial_rope_tables(n_tokens):
    rows = n_tokens // GRID_W
    row = jnp.repeat(jnp.arange(rows, dtype=jnp.float32), GRID_W)
    col = jnp.tile(jnp.arange(GRID_W, dtype=jnp.float32), rows)
    n_pairs = HEAD_DIM // 4
    inv = ROPE_THETA ** (-jnp.arange(n_pairs, dtype=jnp.float32) / n_pairs)
    ang = jnp.concatenate([row[:, None] * inv, col[:, None] * inv], axis=-1)
    return jnp.cos(ang), jnp.sin(ang)


def rope_2d(t, cos, sin):
    tp = t.reshape(*t.shape[:-1], HEAD_DIM // 2, 2).astype(jnp.float32)
    c = cos[None, :, None, :]
    s = sin[None, :, None, :]
    t0, t1 = tp[..., 0], tp[..., 1]
    out = jnp.stack([t0 * c - t1 * s, t0 * s + t1 * c], axis=-1)
    return out.reshape(t.shape).astype(t.dtype)


def _lin_combine(left, right):
    a1, b1 = left
    a2, b2 = right
    return a1 * a2, a2 * b1 + b2


def rglru(x, w_a, b_a, w_x, b_x, lam, h0, reverse):
    bsz, L, W = x.shape
    xf = x.astype(jnp.float32)
    xb = xf.reshape(bsz, L, LRU_BLOCKS, LRU_BLOCK_W)
    r = jax.nn.sigmoid(jnp.einsum('blnd,nde->blne', xb, w_a.astype(jnp.float32)).reshape(bsz, L, W)
                       + b_a.astype(jnp.float32))
    i = jax.nn.sigmoid(jnp.einsum('blnd,nde->blne', xb, w_x.astype(jnp.float32)).reshape(bsz, L, W)
                       + b_x.astype(jnp.float32))
    log_a = -LRU_C * r * jax.nn.softplus(-lam.astype(jnp.float32))
    a = jnp.exp(log_a)
    b = jnp.sqrt(-jnp.expm1(2.0 * log_a)) * (i * xf)
    first = -1 if reverse else 0
    b = b.at[:, first].add(a[:, first] * h0)
    _, h = lax.associative_scan(_lin_combine, (a, b), axis=1, reverse=reverse)
    return h


def hyena_kernel(L, fw1, fb1, freq, fw2, fb2, fw3):
    f32 = jnp.float32
    t_idx = jnp.arange(L, dtype=f32)
    t_lin = t_idx / max(L - 1, 1)
    bands = jnp.linspace(1e-4, HYENA_BANDS - 1, HYENA_BANDS, dtype=f32)
    ang = (2.0 * math.pi / L) * t_idx[:, None] * bands[None, :]
    z = jnp.concatenate([t_lin[:, None], jnp.cos(ang), -jnp.sin(ang)], axis=-1)
    fr = freq.astype(f32)
    h = jnp.sin(fr * (z @ fw1.astype(f32) + fb1.astype(f32)))
    h = jnp.sin(fr * (h @ fw2.astype(f32) + fb2.astype(f32)))
    h = (h @ fw3.astype(f32)).reshape(L, HYENA_ORDER, 2, W_MIX)
    deltas = jnp.linspace(math.log(HYENA_TARGET) / HYENA_SLOW_DECAY,
                          math.log(HYENA_TARGET) / HYENA_FAST_DECAY, W_MIX, dtype=f32)
    h = h * jnp.exp(-t_lin[:, None] * jnp.abs(deltas))[:, None, None, :]
    hf, hb = h[:, :, 0], h[:, :, 1]
    k = jnp.concatenate([hf[:1] + hb[:1], hf[1:], jnp.zeros_like(hf[:1]), hb[:0:-1]], axis=0)
    k = k * lax.rsqrt(jnp.sum(k * k, axis=0, keepdims=True) + EPS)
    return jnp.fft.rfft(k, axis=0)


def fft_long_conv(u, kf):
    L = u.shape[1]
    uf = jnp.fft.rfft(u, n=2 * L, axis=1)
    return jnp.fft.irfft(uf * kf[None], n=2 * L, axis=1)[:, :L]


def hyena_seq(pb, conv_w, kf, skip):
    u = dw_conv(pb, conv_w).astype(jnp.float32)
    v, x1, x2 = jnp.split(u, 3, axis=-1)
    sk = skip.astype(jnp.float32)
    z = v
    for o, gate in enumerate((x1, x2)):
        z = gate * (fft_long_conv(z, kf[:, o]) + sk[o] * z)
    return z.astype(pb.dtype)


def attend(q, k, v):
    bsz, Lq = q.shape[:2]
    nb = Lq // Q_BLOCK
    qb = q.reshape(bsz, nb, Q_BLOCK, N_KV_HEADS, GQA_GROUP, HEAD_DIM).transpose(1, 0, 2, 3, 4, 5)
    scale = HEAD_DIM ** -0.5

    def block(qi):
        s = jnp.einsum('bqkgd,btkd->bkgqt', qi, k, preferred_element_type=jnp.float32) * scale
        pr = jax.nn.softmax(s, axis=-1).astype(v.dtype)
        return jnp.einsum('bkgqt,btkd->bqkgd', pr, v)

    o = lax.map(block, qb)
    return o.transpose(1, 0, 2, 3, 4, 5).reshape(bsz, Lq, N_Q_HEADS * HEAD_DIM)


def heads(t, n):
    return t.reshape(*t.shape[:-1], n, HEAD_DIM)


def merge_branches(pj, ya, yb, yc, p):
    g = jax.nn.sigmoid(pj[..., COL_G:]).reshape(*pj.shape[:-1], N_BRANCH, D_MODEL)
    m = (g[..., 0, :] * (ya @ p['w_branch_a'])
         + g[..., 1, :] * (yb @ p['w_branch_b'])
         + g[..., 2, :] * (yc @ p['w_branch_c']))
    return m @ p['w_out']


def token_mixer(uc, ul, p, ctx_out, cos, sin):
    S = ul.shape[1]
    C = uc.shape[1]
    pl = ul @ p['w_in']
    pc = uc @ (p['w_in'] if ctx_out else p['w_in'][:, :COL_AG])

    xa_c = dw_conv(pc[..., COL_AX:COL_AX + W_MIX], p['lru_conv'])
    xa_l = dw_conv(pl[..., COL_AX:COL_AX + W_MIX], p['lru_conv'])
    h0 = jnp.zeros((uc.shape[0], W_MIX), jnp.float32)
    hs_c = []
    hs_l = []
    for d, rev in enumerate((False, True)):
        gp = (p['lru_w_a'][d], p['lru_b_a'][d], p['lru_w_x'][d], p['lru_b_x'][d], p['lru_lambda'][d])
        h_c = rglru(xa_c, *gp, h0, rev)
        h_l = rglru(xa_l, *gp, h_c[:, 0] if rev else h_c[:, -1], rev)
        hs_c.append(h_c)
        hs_l.append(h_l)
    ya_l = (hs_l[0] + hs_l[1]).astype(ul.dtype) * jax.nn.gelu(pl[..., COL_AG:COL_AG + W_MIX])

    kf_l = hyena_kernel(S, p['hy_fw1'], p['hy_fb1'], p['hy_freq'], p['hy_fw2'], p['hy_fb2'], p['hy_fw3'])
    yb_l = hyena_seq(pl[..., COL_B:COL_B + 3 * W_MIX], p['hy_conv'], kf_l, p['hy_skip'])

    kc = rms_norm(heads(pc[..., COL_CK:COL_CK + KV_W], N_KV_HEADS), p['k_norm'])
    vc = heads(pc[..., COL_CV:COL_CV + KV_W], N_KV_HEADS)
    ql = rope_2d(rms_norm(heads(pl[..., COL_CQ:COL_CQ + W_MIX], N_Q_HEADS), p['q_norm']), cos, sin)
    kl = rope_2d(rms_norm(heads(pl[..., COL_CK:COL_CK + KV_W], N_KV_HEADS), p['k_norm']), cos, sin)
    vl = heads(pl[..., COL_CV:COL_CV + KV_W], N_KV_HEADS)
    yc_l = attend(ql, jnp.concatenate([kc, kl], axis=1), jnp.concatenate([vc, vl], axis=1))

    out_l = merge_branches(pl, ya_l, yb_l, yc_l, p)
    if not ctx_out:
        return None, out_l

    ya_c = (hs_c[0] + hs_c[1]).astype(uc.dtype) * jax.nn.gelu(pc[..., COL_AG:COL_AG + W_MIX])
    kf_c = hyena_kernel(C, p['hy_fw1'], p['hy_fb1'], p['hy_freq'], p['hy_fw2'], p['hy_fb2'], p['hy_fw3'])
    yb_c = hyena_seq(pc[..., COL_B:COL_B + 3 * W_MIX], p['hy_conv'], kf_c, p['hy_skip'])
    qc = rms_norm(heads(pc[..., COL_CQ:COL_CQ + W_MIX], N_Q_HEADS), p['q_norm'])
    yc_c = attend(qc, kc, vc)
    out_c = merge_branches(pc, ya_c, yb_c, yc_c, p)
    return out_c, out_l


def setup_inputs(seed: int = 0) -> dict:
    key = jax.random.key(seed)
    keys = jax.random.split(key, 40)
    kit = iter(range(40))
    f32 = jnp.float32

    def nrm(shape, scale):
        return jax.random.normal(keys[next(kit)], shape, f32) * scale

    def gain(shape):
        return 1.0 + 0.01 * jax.random.normal(keys[next(kit)], shape, f32)

    x = nrm((BATCH, SEQ, D_MODEL), 1.0)
    c = nrm((BATCH, D_MODEL), 1.0)
    ctx = nrm((BATCH, CTX_LEN, D_MODEL), 1.0)
    c_ctx = nrm((D_MODEL,), 1.0)
    w_mod_down = nrm((DEPTH, D_MODEL, MOD_RANK), D_MODEL ** -0.5)
    w_mod_up = nrm((DEPTH, MOD_RANK, N_MOD * D_MODEL), 0.5 * MOD_RANK ** -0.5)
    b_mod = nrm((DEPTH, N_MOD * D_MODEL), 0.01)
    norm_ffn1 = gain((DEPTH, D_MODEL))
    norm_mix = gain((DEPTH, D_MODEL))
    norm_ffn2 = gain((DEPTH, D_MODEL))
    ffn1_w_in = nrm((DEPTH, D_MODEL, 2 * D_FF), D_MODEL ** -0.5)
    ffn1_w_out = nrm((DEPTH, D_FF, D_MODEL), D_FF ** -0.5)
    ffn2_w_in = nrm((DEPTH, D_MODEL, 2 * D_FF), D_MODEL ** -0.5)
    ffn2_w_out = nrm((DEPTH, D_FF, D_MODEL), D_FF ** -0.5)
    w_in = nrm((DEPTH, D_MODEL, N_IN), D_MODEL ** -0.5)
    lru_conv = nrm((DEPTH, CONV_A, W_MIX), CONV_A ** -0.5)
    lru_w_a = nrm((DEPTH, 2, LRU_BLOCKS, LRU_BLOCK_W, LRU_BLOCK_W), LRU_BLOCK_W ** -0.5)
    lru_b_a = nrm((DEPTH, 2, W_MIX), 0.01)
    lru_w_x = nrm((DEPTH, 2, LRU_BLOCKS, LRU_BLOCK_W, LRU_BLOCK_W), LRU_BLOCK_W ** -0.5)
    lru_b_x = nrm((DEPTH, 2, W_MIX), 0.01)
    u = jax.random.uniform(keys[next(kit)], (DEPTH, 2, W_MIX), f32, 0.9, 0.999)
    s = u ** (1.0 / LRU_C)
    lru_lambda = jnp.log(s) - jnp.log1p(-s)
    hy_conv = nrm((DEPTH, CONV_B, 3 * W_MIX), CONV_B ** -0.5)
    hy_fw1 = nrm((DEPTH, HYENA_EMB, HYENA_HIDDEN), HYENA_EMB ** -0.5)
    hy_fb1 = nrm((DEPTH, HYENA_HIDDEN), 0.01)
    hy_freq = gain((DEPTH, HYENA_HIDDEN))
    hy_fw2 = nrm((DEPTH, HYENA_HIDDEN, HYENA_HIDDEN), HYENA_HIDDEN ** -0.5)
    hy_fb2 = nrm((DEPTH, HYENA_HIDDEN), 0.01)
    hy_fw3 = nrm((DEPTH, HYENA_HIDDEN, HYENA_ORDER * 2 * W_MIX), HYENA_HIDDEN ** -0.5)
    hy_skip = nrm((DEPTH, HYENA_ORDER, W_MIX), 0.1)
    q_norm = gain((DEPTH, HEAD_DIM))
    k_norm = gain((DEPTH, HEAD_DIM))
    w_branch_a = nrm((DEPTH, W_MIX, D_MODEL), W_MIX ** -0.5)
    w_branch_b = nrm((DEPTH, W_MIX, D_MODEL), W_MIX ** -0.5)
    w_branch_c = nrm((DEPTH, W_MIX, D_MODEL), W_MIX ** -0.5)
    w_out = nrm((DEPTH, D_MODEL, D_MODEL), D_MODEL ** -0.5)
    final_norm = gain((D_MODEL,))
    return {'x': x, 'c': c, 'ctx': ctx, 'c_ctx': c_ctx,
            'w_mod_down': w_mod_down, 'w_mod_up': w_mod_up, 'b_mod': b_mod,
            'norm_ffn1': norm_ffn1, 'norm_mix': norm_mix, 'norm_ffn2': norm_ffn2,
            'ffn1_w_in': ffn1_w_in, 'ffn1_w_out': ffn1_w_out, 'ffn2_w_in': ffn2_w_in, 'ffn2_w_out': ffn2_w_out,
            'w_in': w_in, 'lru_conv': lru_conv, 'lru_w_a': lru_w_a, 'lru_b_a': lru_b_a,
            'lru_w_x': lru_w_x, 'lru_b_x': lru_b_x, 'lru_lambda': lru_lambda,
            'hy_conv': hy_conv, 'hy_fw1': hy_fw1, 'hy_fb1': hy_fb1, 'hy_freq': hy_freq,
            'hy_fw2': hy_fw2, 'hy_fb2': hy_fb2, 'hy_fw3': hy_fw3, 'hy_skip': hy_skip,
            'q_norm': q_norm, 'k_norm': k_norm,
            'w_branch_a': w_branch_a, 'w_branch_b': w_branch_b, 'w_branch_c': w_branch_c,
            'w_out': w_out, 'final_norm': final_norm}


def reference(x, c, ctx, c_ctx, w_mod_down, w_mod_up, b_mod, norm_ffn1, norm_mix, norm_ffn2,
              ffn1_w_in, ffn1_w_out, ffn2_w_in, ffn2_w_out, w_in, lru_conv, lru_w_a, lru_b_a,
              lru_w_x, lru_b_x, lru_lambda, hy_conv, hy_fw1, hy_fb1, hy_freq, hy_fw2, hy_fb2,
              hy_fw3, hy_skip, q_norm, k_norm, w_branch_a, w_branch_b, w_branch_c, w_out, final_norm):
    cos, sin = axial_rope_tables(x.shape[1])
    s_lat = jax.nn.silu(c)
    s_ctx = jax.nn.silu(c_ctx)[None, :]
    xl, xc = x, ctx
    for i in range(DEPTH):
        ctx_out = i < DEPTH - 1
        p = {'w_in': w_in[i], 'lru_conv': lru_conv[i], 'lru_w_a': lru_w_a[i], 'lru_b_a': lru_b_a[i],
             'lru_w_x': lru_w_x[i], 'lru_b_x': lru_b_x[i], 'lru_lambda': lru_lambda[i],
             'hy_conv': hy_conv[i], 'hy_fw1': hy_fw1[i], 'hy_fb1': hy_fb1[i], 'hy_freq': hy_freq[i],
             'hy_fw2': hy_fw2[i], 'hy_fb2': hy_fb2[i], 'hy_fw3': hy_fw3[i], 'hy_skip': hy_skip[i],
             'q_norm': q_norm[i], 'k_norm': k_norm[i], 'w_branch_a': w_branch_a[i],
             'w_branch_b': w_branch_b[i], 'w_branch_c': w_branch_c[i], 'w_out': w_out[i]}
        ml = ada_modulation(s_lat, w_mod_down[i], w_mod_up[i], b_mod[i])
        mc = ada_modulation(s_ctx, w_mod_down[i], w_mod_up[i], b_mod[i])
        xl = ffn_sublayer(xl, ml[:, 0], ml[:, 1], ml[:, 2], norm_ffn1[i], ffn1_w_in[i], ffn1_w_out[i])
        xc = ffn_sublayer(xc, mc[:, 0], mc[:, 1], mc[:, 2], norm_ffn1[i], ffn1_w_in[i], ffn1_w_out[i])
        ul = modulate(rms_norm(xl, norm_mix[i]), ml[:, 3], ml[:, 4])
        uc = modulate(rms_norm(xc, norm_mix[i]), mc[:, 3], mc[:, 4])
        yc, yl = token_mixer(uc, ul, p, ctx_out, cos, sin)
        xl = xl + ml[:, 5] * yl
        xl = ffn_sublayer(xl, ml[:, 6], ml[:, 7], ml[:, 8], norm_ffn2[i], ffn2_w_in[i], ffn2_w_out[i])
        if ctx_out:
            xc = xc + mc[:, 5] * yc
            xc = ffn_sublayer(xc, mc[:, 6], mc[:, 7], mc[:, 8], norm_ffn2[i], ffn2_w_in[i], ffn2_w_out[i])
    return rms_norm(xl, final_norm)
```

```python
import functools
import math

import jax
import jax.numpy as jnp
from jax import lax
from jax.experimental import pallas as pl
from jax.experimental.pallas import tpu as pltpu

F32 = jnp.float32
BF16 = jnp.bfloat16

HEAD_DIM = 128
LANES = 128
SUBLANES = 8
GQA_GROUP = 3
GRID_W = 64
ROPE_THETA = 10000.0
LRU_C = 8.0
CONV_A = 4
CONV_B = 3
HYENA_ORDER = 2
HYENA_BANDS = 16
HYENA_EMB = 2 * HYENA_BANDS + 1
HYENA_FAST_DECAY = 0.3
HYENA_SLOW_DECAY = 1.5
HYENA_TARGET = 1e-2
N_MOD = 9
EPS = 1e-6
DFT_INNER = 128
VMEM_LIMIT = 56 * 1024 * 1024

ROW_TILE = 768
COL_TILE = 512
EW_ROWS = 256


def _params(*sem):
    return pltpu.CompilerParams(dimension_semantics=sem, vmem_limit_bytes=VMEM_LIMIT)


def _dot(a, b):
    return jnp.dot(a.astype(BF16), b.astype(BF16), preferred_element_type=F32)


def _mod_kernel(c_ref, wd_ref, wu_ref, b_ref, o_ref):
    c = c_ref[...]
    s = c * jax.nn.sigmoid(c)
    t = _dot(s, wd_ref[0])
    o_ref[0] = _dot(t, wu_ref[0]) + b_ref[0]


def _modulation(cc, w_down, w_up, b_mod):
    depth, d, rank = w_down.shape
    out = pl.pallas_call(
        _mod_kernel,
        grid=(depth, N_MOD),
        in_specs=[
            pl.BlockSpec((SUBLANES, d), lambda l, j: (0, 0)),
            pl.BlockSpec((1, d, rank), lambda l, j: (l, 0, 0)),
            pl.BlockSpec((1, rank, d), lambda l, j: (l, 0, j)),
            pl.BlockSpec((1, 1, d), lambda l, j: (l, 0, j)),
        ],
        out_specs=pl.BlockSpec((1, SUBLANES, d), lambda l, j: (l, 0, j)),
        out_shape=jax.ShapeDtypeStruct((depth, SUBLANES, N_MOD * d), F32),
        compiler_params=_params("arbitrary", "arbitrary"),
        name="modulation",
    )(cc, w_down, w_up, b_mod.reshape(depth, 1, N_MOD * d))
    return out.reshape(depth, SUBLANES, N_MOD, d)[:, :2]


def _norm_mod_kernel(x_ref, g_ref, sh_ref, sc_ref, o_ref):
    x = x_ref[...]
    y = x * lax.rsqrt(jnp.mean(x * x, axis=-1, keepdims=True) + EPS)
    y = y * g_ref[...]
    o_ref[...] = (y * (1.0 + sc_ref[0]) + sh_ref[0]).astype(o_ref.dtype)


def _norm_mod(x, g, mods, idx, n_lat):
    t, d = x.shape
    nl = n_lat // EW_ROWS

    def sel(i, k):
        return (jnp.where(i >= nl, N_MOD, 0) + k, 0, 0)

    return pl.pallas_call(
        _norm_mod_kernel,
        grid=(t // EW_ROWS,),
        in_specs=[
            pl.BlockSpec((EW_ROWS, d), lambda i: (i, 0)),
            pl.BlockSpec((1, d), lambda i: (0, 0)),
            pl.BlockSpec((1, 1, d), lambda i: sel(i, idx)),
            pl.BlockSpec((1, 1, d), lambda i: sel(i, idx + 1)),
        ],
        out_specs=pl.BlockSpec((EW_ROWS, d), lambda i: (i, 0)),
        out_shape=jax.ShapeDtypeStruct((t, d), BF16),
        compiler_params=_params("parallel"),
        name="norm_mod",
    )(x, g.reshape(1, d), mods, mods)


def _final_norm_kernel(x_ref, g_ref, o_ref):
    x = x_ref[...]
    y = x * lax.rsqrt(jnp.mean(x * x, axis=-1, keepdims=True) + EPS)
    o_ref[...] = y * g_ref[...]


def _final_norm(x, g, n_lat):
    t, d = x.shape
    return pl.pallas_call(
        _final_norm_kernel,
        grid=(n_lat // EW_ROWS,),
        in_specs=[pl.BlockSpec((EW_ROWS, d), lambda i: (i, 0)),
                  pl.BlockSpec((1, d), lambda i: (0, 0))],
        out_specs=pl.BlockSpec((EW_ROWS, d), lambda i: (i, 0)),
        out_shape=jax.ShapeDtypeStruct((n_lat, d), F32),
        compiler_params=_params("parallel"),
        name="final_norm",
    )(x, g.reshape(1, d))


def _ffn_up_kernel(u_ref, wg_ref, wu_ref, o_ref):
    u = u_ref[...]
    a = jnp.dot(u, wg_ref[0], preferred_element_type=F32)
    b = jnp.dot(u, wu_ref[0], preferred_element_type=F32)
    o_ref[...] = (a * jax.nn.sigmoid(a) * b).astype(o_ref.dtype)


def _ffn_up(u, w_gu, layer):
    t, d = u.shape
    f = w_gu.shape[-1] // 2
    nj = f // COL_TILE
    return pl.pallas_call(
        _ffn_up_kernel,
        grid=(t // ROW_TILE, nj),
        in_specs=[
            pl.BlockSpec((ROW_TILE, d), lambda i, j: (i, 0)),
            pl.BlockSpec((1, d, COL_TILE), lambda i, j: (layer, 0, j)),
            pl.BlockSpec((1, d, COL_TILE), lambda i, j: (layer, 0, j + nj)),
        ],
        out_specs=pl.BlockSpec((ROW_TILE, COL_TILE), lambda i, j: (i, j)),
        out_shape=jax.ShapeDtypeStruct((t, f), BF16),
        compiler_params=_params("parallel", "arbitrary"),
        name="ffn_up",
    )(u, w_gu, w_gu)


def _down_kernel(h_ref, w_ref, x_ref, g_ref, o_ref, *, coef, n_lat):
    acc = jnp.dot(h_ref[...], w_ref[0], preferred_element_type=F32)
    tm = acc.shape[0]
    row = pl.program_id(0) * tm + lax.broadcasted_iota(jnp.int32, (tm, 1), 0)
    g = jnp.where(row >= n_lat, g_ref[1:2, :], g_ref[0:1, :])
    o_ref[...] = x_ref[...] + coef * g * acc


def _down(h, w, layer, x, gates, coef, n_lat):
    t, k = h.shape
    d = w.shape[-1]
    return pl.pallas_call(
        functools.partial(_down_kernel, coef=coef, n_lat=n_lat),
        grid=(t // ROW_TILE, d // COL_TILE),
        in_specs=[
            pl.BlockSpec((ROW_TILE, k), lambda i, j: (i, 0)),
            pl.BlockSpec((1, k, COL_TILE), lambda i, j: (layer, 0, j)),
            pl.BlockSpec((ROW_TILE, COL_TILE), lambda i, j: (i, j)),
            pl.BlockSpec((2, COL_TILE), lambda i, j: (0, j)),
        ],
        out_specs=pl.BlockSpec((ROW_TILE, COL_TILE), lambda i, j: (i, j)),
        out_shape=jax.ShapeDtypeStruct((t, d), F32),
        input_output_aliases={2: 0},
        compiler_params=_params("parallel", "arbitrary"),
        name="down_residual",
    )(h, w, x, gates)


def _proj_kernel(u_ref, w_ref, o_ref, *, sigmoid):
    acc = jnp.dot(u_ref[...], w_ref[0], preferred_element_type=F32)
    if sigmoid:
        acc = jax.nn.sigmoid(acc)
    o_ref[...] = acc.astype(o_ref.dtype)


def _proj(u, w, layer, col0, ncols, sigmoid, out_dtype, name):
    t, d = u.shape
    j0 = col0 // COL_TILE
    return pl.pallas_call(
        functools.partial(_proj_kernel, sigmoid=sigmoid),
        grid=(t // ROW_TILE, ncols // COL_TILE),
        in_specs=[
            pl.BlockSpec((ROW_TILE, d), lambda i, j: (i, 0)),
            pl.BlockSpec((1, d, COL_TILE), lambda i, j: (layer, 0, j + j0)),
        ],
        out_specs=pl.BlockSpec((ROW_TILE, COL_TILE), lambda i, j: (i, j)),
        out_shape=jax.ShapeDtypeStruct((t, ncols), out_dtype),
        compiler_params=_params("parallel", "arbitrary"),
        name=name,
    )(u, w)


def _merge_kernel(ya_ref, yb_ref, yc_ref, wa_ref, wb_ref, wc_ref, ga_ref, gb_ref, gc_ref, o_ref):
    m = ga_ref[...].astype(F32) * jnp.dot(ya_ref[...], wa_ref[0], preferred_element_type=F32)
    m += gb_ref[...].astype(F32) * jnp.dot(yb_ref[...], wb_ref[0], preferred_element_type=F32)
    m += gc_ref[...].astype(F32) * jnp.dot(yc_ref[...], wc_ref[0], preferred_element_type=F32)
    o_ref[...] = m.astype(o_ref.dtype)


def _merge(ya, yb, yc, wa, wb, wc, layer, gates):
    t, w = ya.shape
    d = wa.shape[-1]
    nj = d // COL_TILE
    y_spec = pl.BlockSpec((ROW_TILE, w), lambda i, j: (i, 0))
    w_spec = pl.BlockSpec((1, w, COL_TILE), lambda i, j: (layer, 0, j))

    def g_spec(k):
        return pl.BlockSpec((ROW_TILE, COL_TILE), lambda i, j: (i, j + k * nj))

    return pl.pallas_call(
        _merge_kernel,
        grid=(t // ROW_TILE, nj),
        in_specs=[y_spec, y_spec, y_spec, w_spec, w_spec, w_spec, g_spec(0), g_spec(1), g_spec(2)],
        out_specs=pl.BlockSpec((ROW_TILE, COL_TILE), lambda i, j: (i, j)),
        out_shape=jax.ShapeDtypeStruct((t, d), BF16),
        compiler_params=_params("parallel", "arbitrary"),
        name="merge",
    )(ya, yb, yc, wa, wb, wc, gates, gates, gates)


def _lru_kernel(pa_ref, pg_ref, cw_ref, wa_ref, ba_ref, wx_ref, bx_ref, lam_ref, h0_ref,
                ya_ref, hT_ref, xs, a_sc, b_sc, *, ts, chunk):
    pad = SUBLANES
    win = chunk + 2 * pad
    n_chunks = ts // chunk
    zeros = jnp.zeros((pad, LANES), F32)
    xs[pl.ds(0, pad), :] = zeros
    xs[pl.ds(pad + ts, pad), :] = zeros

    def copy_in(c, carry):
        t0 = pl.multiple_of(c * chunk, chunk)
        xs[pl.ds(pad + t0, chunk), :] = pa_ref[pl.ds(t0, chunk), :]
        return carry

    lax.fori_loop(0, n_chunks, copy_in, 0)

    sp = [jax.nn.softplus(-lam_ref[d:d + 1, :]) for d in range(2)]

    def gates(c, carry):
        t0 = pl.multiple_of(c * chunk, chunk)
        xw = xs[pl.ds(t0, win), :]
        xa = None
        for k in range(CONV_A):
            sh = pltpu.roll(xw, (win + 1 - k) % win, 0) if k != 1 else xw
            term = sh[pad:pad + chunk, :] * cw_ref[k:k + 1, :]
            xa = term if xa is None else xa + term
        xb = xa.astype(BF16)
        for d in range(2):
            r = jax.nn.sigmoid(_dot(xb, wa_ref[d, 0]) + ba_ref[d:d + 1, :])
            i = jax.nn.sigmoid(_dot(xb, wx_ref[d, 0]) + bx_ref[d:d + 1, :])
            log_a = -LRU_C * r * sp[d]
            a = jnp.exp(log_a)
            b = jnp.sqrt(1.0 - a * a) * (i * xa)
            a_sc[d, pl.ds(t0, chunk), :] = a
            b_sc[d, pl.ds(t0, chunk), :] = b
        return carry

    lax.fori_loop(0, n_chunks, gates, 0)

    row = lax.broadcasted_iota(jnp.int32, (SUBLANES, LANES), 0)
    steps = (1, 2, 4)

    def scan(j, carry):
        cf, cb = carry
        tf = pl.multiple_of(j * SUBLANES, SUBLANES)
        tb = pl.multiple_of(ts - (j + 1) * SUBLANES, SUBLANES)
        af = a_sc[0, pl.ds(tf, SUBLANES), :]
        bf = b_sc[0, pl.ds(tf, SUBLANES), :]
        ab = a_sc[1, pl.ds(tb, SUBLANES), :]
        bb = b_sc[1, pl.ds(tb, SUBLANES), :]
        for s in steps:
            mf = row >= s
            bf = bf + af * jnp.where(mf, pltpu.roll(bf, s, 0), 0.0)
            af = af * jnp.where(mf, pltpu.roll(af, s, 0), 1.0)
            mb = row < SUBLANES - s
            bb = bb + ab * jnp.where(mb, pltpu.roll(bb, SUBLANES - s, 0), 0.0)
            ab = ab * jnp.where(mb, pltpu.roll(ab, SUBLANES - s, 0), 1.0)
        hf = bf + af * cf
        hb = bb + ab * cb
        b_sc[0, pl.ds(tf, SUBLANES), :] = hf
        b_sc[1, pl.ds(tb, SUBLANES), :] = hb
        cf = jnp.broadcast_to(hf[SUBLANES - 1:SUBLANES, :], (SUBLANES, LANES))
        cb = jnp.broadcast_to(hb[0:1, :], (SUBLANES, LANES))
        return cf, cb

    c0 = (jnp.broadcast_to(h0_ref[0:1, :], (SUBLANES, LANES)),
          jnp.broadcast_to(h0_ref[1:2, :], (SUBLANES, LANES)))
    cf, cb = lax.fori_loop(0, ts // SUBLANES, scan, c0)
    hT_ref[0:1, :] = cf[0:1, :]
    hT_ref[1:2, :] = cb[0:1, :]

    def finish(c, carry):
        t0 = pl.multiple_of(c * chunk, chunk)
        h = b_sc[0, pl.ds(t0, chunk), :] + b_sc[1, pl.ds(t0, chunk), :]
        g = jax.nn.gelu(pg_ref[pl.ds(t0, chunk), :], approximate=True)
        ya_ref[pl.ds(t0, chunk), :] = (h * g).astype(ya_ref.dtype)
        return carry

    lax.fori_loop(0, n_chunks, finish, 0)


def _lru(p_main, row0, ts, col_ax, col_ag, cw, wa, ba, wx, bx, lam, h0):
    w = cw.shape[-1]
    nblk = w // LANES
    rb = row0 // ts
    cax = col_ax // LANES
    cag = col_ag // LANES
    chunk = min(EW_ROWS, ts)
    kern = functools.partial(_lru_kernel, ts=ts, chunk=chunk)
    vec = pl.BlockSpec((2, LANES), lambda j: (0, j))
    mat = pl.BlockSpec((2, 1, LANES, LANES), lambda j: (0, j, 0, 0))
    return pl.pallas_call(
        kern,
        grid=(nblk,),
        in_specs=[
            pl.BlockSpec((ts, LANES), lambda j: (rb, cax + j)),
            pl.BlockSpec((ts, LANES), lambda j: (rb, cag + j)),
            pl.BlockSpec((CONV_A, LANES), lambda j: (0, j)),
            mat, vec, mat, vec, vec, vec,
        ],
        out_specs=[pl.BlockSpec((ts, LANES), lambda j: (0, j)),
                   pl.BlockSpec((2, LANES), lambda j: (0, j))],
        out_shape=[jax.ShapeDtypeStruct((ts, w), BF16),
                   jax.ShapeDtypeStruct((2, w), F32)],
        scratch_shapes=[pltpu.VMEM((ts + 2 * SUBLANES, LANES), F32),
                        pltpu.VMEM((2, ts, LANES), F32),
                        pltpu.VMEM((2, ts, LANES), F32)],
        compiler_params=_params("parallel"),
        name="rglru",
    )(p_main, p_main, cw, wa, ba, wx, bx, lam, h0)


def _conv3_kernel(x_ref, xp_ref, xn_ref, w_ref, o_ref):
    i = pl.program_id(0)
    first = i == 0
    last = i == pl.num_programs(0) - 1
    x = x_ref[...]
    r = x.shape[0]
    row = lax.broadcasted_iota(jnp.int32, (r, 1), 0)
    prev_row = jnp.where(first, 0.0, xp_ref[SUBLANES - 1:SUBLANES, :])
    next_row = jnp.where(last, 0.0, xn_ref[0:1, :])
    xm1 = jnp.where(row == 0, prev_row, pltpu.roll(x, 1, 0))
    xp1 = jnp.where(row == r - 1, next_row, pltpu.roll(x, r - 1, 0))
    o_ref[0] = w_ref[0:1, :] * xm1 + w_ref[1:2, :] * x + w_ref[2:3, :] * xp1


def _conv3(p_main, row0, ts, col_b, w3, wmix):
    r = min(EW_ROWS, ts)
    rb = row0 // r
    hb = r // SUBLANES
    cb = col_b // LANES
    nw = wmix // LANES
    n_r = ts // r
    return pl.pallas_call(
        _conv3_kernel,
        grid=(n_r, 3 * nw),
        in_specs=[
            pl.BlockSpec((r, LANES), lambda i, j: (rb + i, cb + j)),
            pl.BlockSpec((SUBLANES, LANES), lambda i, j: (jnp.maximum((rb + i) * hb - 1, 0), cb + j)),
            pl.BlockSpec((SUBLANES, LANES),
                         lambda i, j: (jnp.minimum((rb + i + 1) * hb, (rb + n_r) * hb - 1), cb + j)),
            pl.BlockSpec((CONV_B, LANES), lambda i, j: (0, j)),
        ],
        out_specs=pl.BlockSpec((1, r, LANES), lambda i, j: (j // nw, i, j % nw)),
        out_shape=jax.ShapeDtypeStruct((3, ts, wmix), F32),
        compiler_params=_params("arbitrary", "arbitrary"),
        name="hyena_conv3",
    )(p_main, p_main, p_main, w3)


def _filter_kernel(z_ref, tl_ref, w1_ref, b1_ref, fr_ref, w2_ref, b2_ref, w3_ref, w3b_ref, ad_ref,
                   k_ref, ssq_ref, *, half_tiles):
    i = pl.program_id(0)
    hi = lax.Precision.HIGHEST
    fr = fr_ref[...]
    h = jnp.sin(fr * (jnp.dot(z_ref[...], w1_ref[...], precision=hi, preferred_element_type=F32)
                      + b1_ref[...]))
    h = jnp.sin(fr * (jnp.dot(h, w2_ref[...], precision=hi, preferred_element_type=F32) + b2_ref[...]))
    decay = jnp.exp(-tl_ref[...] * ad_ref[...])
    taps = jnp.dot(h, w3_ref[0], precision=hi, preferred_element_type=F32) * decay
    r = taps.shape[0]
    row = lax.broadcasted_iota(jnp.int32, (r, 1), 0)
    k_ref[...] = taps

    @pl.when(i == 0)
    def _():
        back = jnp.dot(h, w3b_ref[0], precision=hi, preferred_element_type=F32) * decay
        k_ref[...] = taps + jnp.where(row == 0, back, 0.0)
        ssq_ref[...] = jnp.zeros_like(ssq_ref)

    @pl.when(i == half_tiles)
    def _():
        k_ref[...] = jnp.where(row == 0, 0.0, taps)

    kk = k_ref[...]
    ssq_ref[...] += jnp.sum(kk * kk, axis=0, keepdims=True)


def _hyena_filter(seq, p):
    wmix = p["hy_skip"].shape[-1]
    hidden = p["hy_fw1"].shape[-1]
    r = min(EW_ROWS, seq)
    t_idx = jnp.arange(seq, dtype=F32)
    t_lin = t_idx / max(seq - 1, 1)
    bands = jnp.linspace(1e-4, HYENA_BANDS - 1, HYENA_BANDS, dtype=F32)
    ang = (2.0 * math.pi / seq) * t_idx[:, None] * bands[None, :]
    z = jnp.concatenate([t_lin[:, None], jnp.cos(ang), -jnp.sin(ang)], axis=-1)
    rev = lambda a: jnp.concatenate([a[:1], jnp.flip(a[1:], axis=0)], axis=0)
    zz = jnp.concatenate([z, rev(z)], axis=0)
    zz = jnp.pad(zz, ((0, 0), (0, LANES - HYENA_EMB)))
    tl = jnp.concatenate([t_lin, rev(t_lin)])[:, None]
    w1 = jnp.pad(p["hy_fw1"], ((0, LANES - HYENA_EMB), (0, 0)))
    w3 = p["hy_fw3"].reshape(hidden, HYENA_ORDER, 2, wmix).transpose(2, 0, 1, 3)
    w3 = w3.reshape(2, hidden, HYENA_ORDER * wmix)
    deltas = jnp.linspace(math.log(HYENA_TARGET) / HYENA_SLOW_DECAY,
                          math.log(HYENA_TARGET) / HYENA_FAST_DECAY, wmix, dtype=F32)
    ad = jnp.tile(jnp.abs(deltas), HYENA_ORDER)[None, :]
    ow = HYENA_ORDER * wmix
    half = seq // r
    full = lambda shape: pl.BlockSpec(shape, lambda i: tuple(0 for _ in shape))
    return pl.pallas_call(
        functools.partial(_filter_kernel, half_tiles=half),
        grid=(2 * half,),
        in_specs=[
            pl.BlockSpec((r, LANES), lambda i: (i, 0)),
            pl.BlockSpec((r, 1), lambda i: (i, 0)),
            full((LANES, hidden)), full((1, hidden)), full((1, hidden)),
            full((hidden, hidden)), full((1, hidden)),
            pl.BlockSpec((1, hidden, ow), lambda i: (jnp.where(i >= half, 1, 0), 0, 0)),
            pl.BlockSpec((1, hidden, ow), lambda i: (1, 0, 0)),
            full((1, ow)),
        ],
        out_specs=[pl.BlockSpec((r, ow), lambda i: (i, 0)),
                   pl.BlockSpec((1, ow), lambda i: (0, 0))],
        out_shape=[jax.ShapeDtypeStruct((2 * seq, ow), F32),
                   jax.ShapeDtypeStruct((1, ow), F32)],
        compiler_params=_params("arbitrary"),
        name="hyena_filter",
    )(zz, tl, w1, p["hy_fb1"][None, :], p["hy_freq"][None, :], p["hy_fw2"], p["hy_fb2"][None, :],
      w3, w3, ad)


def _wide_mm_kernel(a_ref, b_ref, o_ref):
    o_ref[...] = _dot(a_ref[...], b_ref[...]).astype(o_ref.dtype)


def _wide_gate_kernel(a_ref, b_ref, x_ref, v_ref, sk_ref, o_ref):
    y = _dot(a_ref[...], b_ref[...])
    o_ref[...] = (x_ref[...] * (y + sk_ref[...] * v_ref[...])).astype(o_ref.dtype)


def _wide_tile(n):
    tn = 4096
    while n % tn:
        tn //= 2
    return tn


def _wide_mm(a, b, out_dtype, name):
    m, k = a.shape
    n = b.shape[1]
    tn = _wide_tile(n)
    return pl.pallas_call(
        _wide_mm_kernel,
        grid=(n // tn,),
        in_specs=[pl.BlockSpec((m, k), lambda j: (0, 0)),
                  pl.BlockSpec((k, tn), lambda j: (0, j))],
        out_specs=pl.BlockSpec((m, tn), lambda j: (0, j)),
        out_shape=jax.ShapeDtypeStruct((m, n), out_dtype),
        compiler_params=_params("parallel"),
        name=name,
    )(a, b)


def _wide_gate(a, b, x, v, sk, out_dtype, name):
    m, k = a.shape
    n = b.shape[1]
    tn = _wide_tile(n)
    blk = pl.BlockSpec((m, tn), lambda j: (0, j))
    return pl.pallas_call(
        _wide_gate_kernel,
        grid=(n // tn,),
        in_specs=[pl.BlockSpec((m, k), lambda j: (0, 0)),
                  pl.BlockSpec((k, tn), lambda j: (0, j)),
                  blk, blk,
                  pl.BlockSpec((1, tn), lambda j: (0, j))],
        out_specs=blk,
        out_shape=jax.ShapeDtypeStruct((m, n), out_dtype),
        compiler_params=_params("parallel"),
        name=name,
    )(a, b, x, v, sk)


def _bmm_scale_kernel(m_ref, a_ref, s_ref, o_ref):
    o_ref[...] = (_dot(m_ref[0], a_ref[...]) * s_ref[...]).astype(o_ref.dtype)


def _bmm_scale(m2, a, scale, rows_in, name):
    nb, rows_out, _ = m2.shape
    n = a.shape[1]
    tn = _wide_tile(n)
    return pl.pallas_call(
        _bmm_scale_kernel,
        grid=(nb, n // tn),
        in_specs=[pl.BlockSpec((1, rows_out, rows_in), lambda b, j: (b, 0, 0)),
                  pl.BlockSpec((rows_in, tn), lambda b, j: (b, j)),
                  pl.BlockSpec((1, tn), lambda b, j: (0, j))],
        out_specs=pl.BlockSpec((rows_out, tn), lambda b, j: (b, j)),
        out_shape=jax.ShapeDtypeStruct((nb * rows_out, n), F32),
        compiler_params=_params("parallel", "arbitrary"),
        name=name,
    )(m2, a, scale)


def _spectral_body(m2_ref, a_ref, k_ref, m3_ref):
    x = _dot(m2_ref[0], a_ref[...])
    f = x.shape[0] // 2
    xr, xi = x[:f], x[f:]
    kr, ki = k_ref[:f, :], k_ref[f:, :]
    y = jnp.concatenate([xr * kr - xi * ki, xr * ki + xi * kr], axis=0)
    return _dot(m3_ref[0], y)


def _spectral_kernel(m2_ref, a_ref, k_ref, m3_ref, o_ref):
    o_ref[...] = _spectral_body(m2_ref, a_ref, k_ref, m3_ref).astype(o_ref.dtype)


def _spectral_gate_kernel(m2_ref, a_ref, k_ref, m3_ref, x_ref, v_ref, sk_ref, o_ref):
    y = _spectral_body(m2_ref, a_ref, k_ref, m3_ref)
    o_ref[...] = (x_ref[...] * (y + sk_ref[...] * v_ref[...])).astype(o_ref.dtype)


def _spectral(m2, a, kspec, order, m3, out_dtype, gate=None, name="hyena_spectral"):
    nb, f2, rows_in = m2.shape
    rows_out = m3.shape[1]
    wmix = a.shape[1]
    in_specs = [pl.BlockSpec((1, f2, rows_in), lambda b: (b, 0, 0)),
                pl.BlockSpec((rows_in, wmix), lambda b: (b, 0)),
                pl.BlockSpec((f2, wmix), lambda b: (b, order)),
                pl.BlockSpec((1, rows_out, f2), lambda b: (b, 0, 0))]
    args = [m2, a, kspec, m3]
    kern = _spectral_kernel
    if gate is not None:
        x, v, sk = gate
        blk = pl.BlockSpec((rows_out, wmix), lambda b: (b, 0))
        in_specs += [blk, blk, pl.BlockSpec((1, wmix), lambda b: (0, 0))]
        args += [x, v, sk]
        kern = _spectral_gate_kernel
    return pl.pallas_call(
        kern,
        grid=(nb,),
        in_specs=in_specs,
        out_specs=pl.BlockSpec((rows_out, wmix), lambda b: (b, 0)),
        out_shape=jax.ShapeDtypeStruct((nb * rows_out, wmix), out_dtype),
        compiler_params=_params("parallel"),
        name=name,
    )(*args)


def _dft_tables(seq):
    n = 2 * seq
    n2 = DFT_INNER
    n1 = n // n2
    i1 = jnp.arange(n1, dtype=jnp.int32)
    i2 = jnp.arange(n2, dtype=jnp.int32)
    ang1 = (2.0 * math.pi / n1) * ((i1[:, None] * i1[None, :]) % n1).astype(F32)
    c1, s1 = jnp.cos(ang1), jnp.sin(ang1)
    f1 = jnp.stack([c1, -s1], axis=1).reshape(2 * n1, n1)
    q = i1[:, None, None] + n1 * i2[None, :, None]
    ang = (2.0 * math.pi / n) * ((q * i2[None, None, :]) % n).astype(F32)
    tr, ti = jnp.cos(ang), -jnp.sin(ang)
    m2 = jnp.concatenate([jnp.concatenate([tr, -ti], axis=2),
                          jnp.concatenate([ti, tr], axis=2)], axis=1)
    trt, tit = jnp.swapaxes(tr, 1, 2), jnp.swapaxes(ti, 1, 2)
    m3 = jnp.concatenate([jnp.concatenate([trt, tit], axis=2),
                          jnp.concatenate([-tit, trt], axis=2)], axis=1)
    g = jnp.stack([c1, -s1], axis=2).reshape(n1, 2 * n1)[: n1 // 2] / n
    return (f1[:, : n1 // 2].astype(BF16), f1.astype(BF16), m2.astype(BF16), m3.astype(BF16),
            g.astype(BF16))


def _direct_dft_tables(seq):
    n = 2 * seq
    i = jnp.arange(n, dtype=jnp.int32)
    ang = (2.0 * math.pi / n) * ((i[:, None] * i[None, :]) % n).astype(F32)
    c, s = jnp.cos(ang), jnp.sin(ang)
    fwd = jnp.concatenate([c, -s], axis=0)
    inv = jnp.concatenate([c[:seq], -s[:seq]], axis=1) / n
    return fwd[:, :seq].astype(BF16)[None], fwd.astype(BF16)[None], inv.astype(BF16)[None]


def _hyena_long(u3, taps, ssq, skip, tables):
    f1_half, f1_full, m2, m3, g = tables
    _, seq, wmix = u3.shape
    n2 = DFT_INNER
    n1 = 2 * seq // n2
    ow = taps.shape[1]
    scale = lax.rsqrt(ssq + EPS)
    ak = _wide_mm(f1_full, taps.reshape(n1, n2 * ow), BF16, "hyena_filter_dft1")
    kspec = _bmm_scale(m2, ak.reshape(n1 * 2 * n2, ow), scale, 2 * n2, "hyena_filter_dft2")
    v = u3[0].reshape(n1 // 2, n2 * wmix)
    z = v
    for o in range(HYENA_ORDER):
        a = _wide_mm(f1_half, z, BF16, "hyena_dft1")
        b = _spectral(m2, a.reshape(n1 * 2 * n2, wmix), kspec, o, m3, BF16)
        gate = u3[1 + o].reshape(n1 // 2, n2 * wmix)
        sk = jnp.tile(skip[o], n2)[None, :]
        last = o == HYENA_ORDER - 1
        z = _wide_gate(g, b.reshape(n1 * 2, n2 * wmix), gate, z, sk, BF16 if last else F32,
                       "hyena_dft4_gate")
    return z.reshape(seq, wmix)


def _hyena_short(u3, taps, ssq, skip, tables):
    fwd_half, fwd_full, inv = tables
    scale = lax.rsqrt(ssq + EPS)
    kspec = _bmm_scale(fwd_full, taps, scale, taps.shape[0], "hyena_ctx_filter_dft")
    z = u3[0]
    for o in range(HYENA_ORDER):
        last = o == HYENA_ORDER - 1
        z = _spectral(fwd_half, z, kspec, o, inv, BF16 if last else F32,
                      gate=(u3[1 + o], z, skip[o][None, :]), name="hyena_ctx_spectral")
    return z


def _qk_prep_kernel(t_ref, g_ref, cc_ref, se_ref, so_ref, o_ref, *, norm, scale):
    y = t_ref[...]
    if norm:
        y = y * lax.rsqrt(jnp.mean(y * y, axis=-1, keepdims=True) + EPS) * g_ref[...]
        y = (y * cc_ref[...] + pltpu.roll(y, LANES - 1, 1) * se_ref[...]
             + pltpu.roll(y, 1, 1) * so_ref[...])
    o_ref[...] = (y * scale).astype(o_ref.dtype)


def _qk_prep(p_main, col0, ncols, gain, rope, norm, scale):
    t = p_main.shape[0]
    c0 = col0 // LANES
    cc, se, so = rope
    tab = pl.BlockSpec((EW_ROWS, LANES), lambda i, j: (i, 0))
    return pl.pallas_call(
        functools.partial(_qk_prep_kernel, norm=norm, scale=scale),
        grid=(t // EW_ROWS, ncols // LANES),
        in_specs=[pl.BlockSpec((EW_ROWS, LANES), lambda i, j: (i, c0 + j)),
                  pl.BlockSpec((1, LANES), lambda i, j: (0, 0)),
                  tab, tab, tab],
        out_specs=pl.BlockSpec((EW_ROWS, LANES), lambda i, j: (i, j)),
        out_shape=jax.ShapeDtypeStruct((t, ncols), BF16),
        compiler_params=_params("parallel", "arbitrary"),
        name="qk_prep",
    )(p_main, gain.reshape(1, LANES), cc, se, so)


def _attn_kernel(q_ref, k_ref, v_ref, o_ref, m_sc, l_sc, acc_sc):
    j = pl.program_id(2)

    @pl.when(j == 0)
    def _():
        m_sc[...] = jnp.full_like(m_sc, -jnp.inf)
        l_sc[...] = jnp.zeros_like(l_sc)
        acc_sc[...] = jnp.zeros_like(acc_sc)

    k = k_ref[...]
    v = v_ref[...]
    for g in range(GQA_GROUP):
        q = q_ref[:, g * HEAD_DIM:(g + 1) * HEAD_DIM]
        s = lax.dot_general(q, k, (((1,), (1,)), ((), ())), preferred_element_type=F32)
        m_prev = m_sc[g]
        m_new = jnp.maximum(m_prev, jnp.max(s, axis=1, keepdims=True))
        alpha = jnp.exp(m_prev - m_new)
        p = jnp.exp(s - m_new)
        l_sc[g] = alpha * l_sc[g] + jnp.sum(p, axis=1, keepdims=True)
        acc_sc[g] = alpha * acc_sc[g] + jnp.dot(p.astype(BF16), v, preferred_element_type=F32)
        m_sc[g] = m_new

    @pl.when(j == pl.num_programs(2) - 1)
    def _():
        for g in range(GQA_GROUP):
            o_ref[:, g * HEAD_DIM:(g + 1) * HEAD_DIM] = (acc_sc[g] / l_sc[g]).astype(o_ref.dtype)


def _attention(q, k, v, q_row0, n_q, k_row0, n_k, tq, tk):
    n_kv = k.shape[1] // HEAD_DIM
    gw = GQA_GROUP * HEAD_DIM
    qb, kb = q_row0 // tq, k_row0 // tk
    return pl.pallas_call(
        _attn_kernel,
        grid=(n_kv, n_q // tq, n_k // tk),
        in_specs=[pl.BlockSpec((tq, gw), lambda h, i, j: (qb + i, h)),
                  pl.BlockSpec((tk, HEAD_DIM), lambda h, i, j: (kb + j, h)),
                  pl.BlockSpec((tk, HEAD_DIM), lambda h, i, j: (kb + j, h))],
        out_specs=pl.BlockSpec((tq, gw), lambda h, i, j: (i, h)),
        out_shape=jax.ShapeDtypeStruct((n_q, q.shape[1]), BF16),
        scratch_shapes=[pltpu.VMEM((GQA_GROUP, tq, 1), F32),
                        pltpu.VMEM((GQA_GROUP, tq, 1), F32),
                        pltpu.VMEM((GQA_GROUP, tq, HEAD_DIM), F32)],
        compiler_params=_params("parallel", "parallel", "arbitrary"),
        name="attention",
    )(q, k, v)


def _rope_tables(seq, n_ctx):
    rows = seq // GRID_W
    row = jnp.repeat(jnp.arange(rows, dtype=F32), GRID_W)
    col = jnp.tile(jnp.arange(GRID_W, dtype=F32), rows)
    n_pairs = HEAD_DIM // 4
    inv = ROPE_THETA ** (-jnp.arange(n_pairs, dtype=F32) / n_pairs)
    ang = jnp.concatenate([row[:, None] * inv, col[:, None] * inv], axis=-1)
    ang = jnp.concatenate([ang, jnp.zeros((n_ctx, HEAD_DIM // 2), F32)], axis=0)
    c = jnp.repeat(jnp.cos(ang), 2, axis=1)
    s = jnp.repeat(jnp.sin(ang), 2, axis=1)
    even = (jnp.arange(HEAD_DIM) % 2 == 0)[None, :]
    return c, jnp.where(even, -s, 0.0), jnp.where(even, 0.0, s)


def kernel(x, c, ctx, c_ctx, w_mod_down, w_mod_up, b_mod, norm_ffn1, norm_mix, norm_ffn2,
           ffn1_w_in, ffn1_w_out, ffn2_w_in, ffn2_w_out, w_in, lru_conv, lru_w_a, lru_b_a,
           lru_w_x, lru_b_x, lru_lambda, hy_conv, hy_fw1, hy_fb1, hy_freq, hy_fw2, hy_fb2,
           hy_fw3, hy_skip, q_norm, k_norm, w_branch_a, w_branch_b, w_branch_c, w_out, final_norm):
    bsz, seq, d = x.shape
    assert bsz == 1 and c.shape[0] == 1 and ctx.shape[0] == 1
    n_ctx = ctx.shape[1]
    depth = w_in.shape[0]
    wmix = lru_conv.shape[-1]
    kvw = wmix // GQA_GROUP
    t_all = seq + n_ctx
    assert t_all % ROW_TILE == 0 and seq % EW_ROWS == 0 and n_ctx % EW_ROWS == 0

    col_ax = 0
    col_ck = col_ax + wmix
    col_cv = col_ck + kvw
    col_ag = col_cv + kvw
    col_b = col_ag + wmix
    col_cq = col_b + 3 * wmix
    col_g = col_cq + wmix

    xs = jnp.concatenate([x[0], ctx[0]], axis=0)
    cc = jnp.zeros((SUBLANES, d), F32).at[0].set(c[0]).at[1].set(c_ctx)
    mods_all = _modulation(cc, w_mod_down, w_mod_up, b_mod)

    ffn1_in, ffn1_out = ffn1_w_in.astype(BF16), ffn1_w_out.astype(BF16)
    ffn2_in, ffn2_out = ffn2_w_in.astype(BF16), ffn2_w_out.astype(BF16)
    w_in_b = w_in.astype(BF16)
    wba, wbb, wbc = w_branch_a.astype(BF16), w_branch_b.astype(BF16), w_branch_c.astype(BF16)
    w_out_b = w_out.astype(BF16)

    rope = _rope_tables(seq, n_ctx)
    dft_lat = _dft_tables(seq)
    dft_ctx = _direct_dft_tables(n_ctx)
    q_scale = HEAD_DIM ** -0.5
    tq_lat = 512 if seq % 512 == 0 else EW_ROWS
    tk_lat = ROW_TILE

    for i in range(depth):
        ctx_out = i < depth - 1
        mods = mods_all[i]
        mods3 = mods.reshape(2 * N_MOD, 1, d)

        u = _norm_mod(xs, norm_ffn1[i], mods3, 0, seq)
        h = _ffn_up(u, ffn1_in, i)
        xs = _down(h, ffn1_out, i, xs, mods[:, 2], 0.5, seq)

        u = _norm_mod(xs, norm_mix[i], mods3, 3, seq)
        p_main = _proj(u, w_in_b, i, 0, col_g, False, F32, "mixer_in")
        gates = _proj(u, w_in_b, i, col_g, 3 * d, True, BF16, "mixer_gates")

        lru_args = (lru_conv[i], lru_w_a[i], lru_b_a[i], lru_w_x[i], lru_b_x[i], lru_lambda[i])
        ya_c, h_c = _lru(p_main, seq, n_ctx, col_ax, col_ag, *lru_args, jnp.zeros((2, wmix), F32))
        ya_l, _ = _lru(p_main, 0, seq, col_ax, col_ag, *lru_args, h_c)

        hp = {"hy_fw1": hy_fw1[i], "hy_fb1": hy_fb1[i], "hy_freq": hy_freq[i], "hy_fw2": hy_fw2[i],
              "hy_fb2": hy_fb2[i], "hy_fw3": hy_fw3[i], "hy_skip": hy_skip[i]}
        taps_l, ssq_l = _hyena_filter(seq, hp)
        u3_l = _conv3(p_main, 0, seq, col_b, hy_conv[i], wmix)
        yb_l = _hyena_long(u3_l, taps_l, ssq_l, hy_skip[i], dft_lat)

        qh = _qk_prep(p_main, col_cq, wmix, q_norm[i], rope, True, q_scale)
        kh = _qk_prep(p_main, col_ck, kvw, k_norm[i], rope, True, 1.0)
        vh = _qk_prep(p_main, col_cv, kvw, k_norm[i], rope, False, 1.0)
        yc_l = _attention(qh, kh, vh, 0, seq, 0, t_all, tq_lat, tk_lat)

        if ctx_out:
            taps_c, ssq_c = _hyena_filter(n_ctx, hp)
            u3_c = _conv3(p_main, seq, n_ctx, col_b, hy_conv[i], wmix)
            yb_c = _hyena_short(u3_c, taps_c, ssq_c, hy_skip[i], dft_ctx)
            yc_c = _attention(qh, kh, vh, seq, n_ctx, seq, n_ctx, n_ctx, n_ctx)
        else:
            yb_c = jnp.zeros((n_ctx, wmix), BF16)
            yc_c = jnp.zeros((n_ctx, wmix), BF16)

        ya = jnp.concatenate([ya_l, ya_c], axis=0)
        yb = jnp.concatenate([yb_l, yb_c], axis=0)
        yc = jnp.concatenate([yc_l, yc_c], axis=0)
        m = _merge(ya, yb, yc, wba, wbb, wbc, i, gates)
        xs = _down(m, w_out_b, i, xs, mods[:, 5], 1.0, seq)

        u = _norm_mod(xs, norm_ffn2[i], mods3, 6, seq)
        h = _ffn_up(u, ffn2_in, i)
        xs = _down(h, ffn2_out, i, xs, mods[:, 8], 0.5, seq)

    return _final_norm(xs, final_norm, seq)[None]
```

```python
import functools
import math

import jax
import jax.numpy as jnp
from jax import lax
from jax.experimental import pallas as pl
from jax.experimental.pallas import tpu as pltpu

F32 = jnp.float32
BF16 = jnp.bfloat16

HEAD_DIM = 128
LANES = 128
SUBLANES = 8
GQA_GROUP = 3
GRID_W = 64
ROPE_THETA = 10000.0
LRU_C = 8.0
CONV_A = 4
CONV_B = 3
HYENA_ORDER = 2
HYENA_BANDS = 16
HYENA_EMB = 2 * HYENA_BANDS + 1
HYENA_FAST_DECAY = 0.3
HYENA_SLOW_DECAY = 1.5
HYENA_TARGET = 1e-2
N_MOD = 9
EPS = 1e-6
DFT_INNER = 128
VMEM_LIMIT = 56 * 1024 * 1024

ROW_TILE = 768
COL_TILE = 512
EW_ROWS = 256


def _params(*sem):
    return pltpu.CompilerParams(dimension_semantics=sem, vmem_limit_bytes=VMEM_LIMIT)


def _dot(a, b):
    return jnp.dot(a.astype(BF16), b.astype(BF16), preferred_element_type=F32)


def _mod_kernel(c_ref, wd_ref, wu_ref, b_ref, o_ref):
    c = c_ref[...]
    s = c * jax.nn.sigmoid(c)
    t = _dot(s, wd_ref[0])
    o_ref[0] = _dot(t, wu_ref[0]) + b_ref[0]


def _modulation(cc, w_down, w_up, b_mod):
    depth, d, rank = w_down.shape
    out = pl.pallas_call(
        _mod_kernel,
        grid=(depth, N_MOD),
        in_specs=[
            pl.BlockSpec((SUBLANES, d), lambda l, j: (0, 0)),
            pl.BlockSpec((1, d, rank), lambda l, j: (l, 0, 0)),
            pl.BlockSpec((1, rank, d), lambda l, j: (l, 0, j)),
            pl.BlockSpec((1, 1, d), lambda l, j: (l, 0, j)),
        ],
        out_specs=pl.BlockSpec((1, SUBLANES, d), lambda l, j: (l, 0, j)),
        out_shape=jax.ShapeDtypeStruct((depth, SUBLANES, N_MOD * d), F32),
        compiler_params=_params("arbitrary", "arbitrary"),
        name="modulation",
    )(cc, w_down, w_up, b_mod.reshape(depth, 1, N_MOD * d))
    return out.reshape(depth, SUBLANES, N_MOD, d)[:, :2]


def _norm_mod_kernel(x_ref, g_ref, sh_ref, sc_ref, o_ref):
    x = x_ref[...]
    y = x * lax.rsqrt(jnp.mean(x * x, axis=-1, keepdims=True) + EPS)
    y = y * g_ref[...]
    o_ref[...] = (y * (1.0 + sc_ref[0]) + sh_ref[0]).astype(o_ref.dtype)


def _norm_mod(x, g, mods, idx, n_lat):
    t, d = x.shape
    nl = n_lat // EW_ROWS

    def sel(i, k):
        return (jnp.where(i >= nl, N_MOD, 0) + k, 0, 0)

    return pl.pallas_call(
        _norm_mod_kernel,
        grid=(t // EW_ROWS,),
        in_specs=[
            pl.BlockSpec((EW_ROWS, d), lambda i: (i, 0)),
            pl.BlockSpec((1, d), lambda i: (0, 0)),
            pl.BlockSpec((1, 1, d), lambda i: sel(i, idx)),
            pl.BlockSpec((1, 1, d), lambda i: sel(i, idx + 1)),
        ],
        out_specs=pl.BlockSpec((EW_ROWS, d), lambda i: (i, 0)),
        out_shape=jax.ShapeDtypeStruct((t, d), BF16),
        compiler_params=_params("parallel"),
        name="norm_mod",
    )(x, g.reshape(1, d), mods, mods)


def _final_norm_kernel(x_ref, g_ref, o_ref):
    x = x_ref[...]
    y = x * lax.rsqrt(jnp.mean(x * x, axis=-1, keepdims=True) + EPS)
    o_ref[...] = y * g_ref[...]


def _final_norm(x, g, n_lat):
    t, d = x.shape
    return pl.pallas_call(
        _final_norm_kernel,
        grid=(n_lat // EW_ROWS,),
        in_specs=[pl.BlockSpec((EW_ROWS, d), lambda i: (i, 0)),
                  pl.BlockSpec((1, d), lambda i: (0, 0))],
        out_specs=pl.BlockSpec((EW_ROWS, d), lambda i: (i, 0)),
        out_shape=jax.ShapeDtypeStruct((n_lat, d), F32),
        compiler_params=_params("parallel"),
        name="final_norm",
    )(x, g.reshape(1, d))


def _ffn_up_kernel(u_ref, wg_ref, wu_ref, o_ref):
    u = u_ref[...]
    a = jnp.dot(u, wg_ref[0], preferred_element_type=F32)
    b = jnp.dot(u, wu_ref[0], preferred_element_type=F32)
    o_ref[...] = (a * jax.nn.sigmoid(a) * b).astype(o_ref.dtype)


def _ffn_up(u, w_gu, layer):
    t, d = u.shape
    f = w_gu.shape[-1] // 2
    nj = f // COL_TILE
    return pl.pallas_call(
        _ffn_up_kernel,
        grid=(t // ROW_TILE, nj),
        in_specs=[
            pl.BlockSpec((ROW_TILE, d), lambda i, j: (i, 0)),
            pl.BlockSpec((1, d, COL_TILE), lambda i, j: (layer, 0, j)),
            pl.BlockSpec((1, d, COL_TILE), lambda i, j: (layer, 0, j + nj)),
        ],
        out_specs=pl.BlockSpec((ROW_TILE, COL_TILE), lambda i, j: (i, j)),
        out_shape=jax.ShapeDtypeStruct((t, f), BF16),
        compiler_params=_params("parallel", "arbitrary"),
        name="ffn_up",
    )(u, w_gu, w_gu)


def _down_kernel(h_ref, w_ref, x_ref, g_ref, o_ref, *, coef, n_lat):
    acc = jnp.dot(h_ref[...], w_ref[0], preferred_element_type=F32)
    tm = acc.shape[0]
    row = pl.program_id(0) * tm + lax.broadcasted_iota(jnp.int32, (tm, 1), 0)
    g = jnp.where(row >= n_lat, g_ref[1:2, :], g_ref[0:1, :])
    o_ref[...] = x_ref[...] + coef * g * acc


def _down(h, w, layer, x, gates, coef, n_lat):
    t, k = h.shape
    d = w.shape[-1]
    return pl.pallas_call(
        functools.partial(_down_kernel, coef=coef, n_lat=n_lat),
        grid=(t // ROW_TILE, d // COL_TILE),
        in_specs=[
            pl.BlockSpec((ROW_TILE, k), lambda i, j: (i, 0)),
            pl.BlockSpec((1, k, COL_TILE), lambda i, j: (layer, 0, j)),
            pl.BlockSpec((ROW_TILE, COL_TILE), lambda i, j: (i, j)),
            pl.BlockSpec((2, COL_TILE), lambda i, j: (0, j)),
        ],
        out_specs=pl.BlockSpec((ROW_TILE, COL_TILE), lambda i, j: (i, j)),
        out_shape=jax.ShapeDtypeStruct((t, d), F32),
        input_output_aliases={2: 0},
        compiler_params=_params("parallel", "arbitrary"),
        name="down_residual",
    )(h, w, x, gates)


def _proj_kernel(u_ref, w_ref, o_ref, *, sigmoid):
    acc = jnp.dot(u_ref[...], w_ref[0], preferred_element_type=F32)
    if sigmoid:
        acc = jax.nn.sigmoid(acc)
    o_ref[...] = acc.astype(o_ref.dtype)


def _col_tile(col0, ncols):
    return next(t for t in (COL_TILE, 256, LANES) if col0 % t == 0 and ncols % t == 0)


def _proj(u, w, layer, col0, ncols, sigmoid, out_dtype, name):
    t, d = u.shape
    tn = _col_tile(col0, ncols)
    j0 = col0 // tn
    return pl.pallas_call(
        functools.partial(_proj_kernel, sigmoid=sigmoid),
        grid=(t // ROW_TILE, ncols // tn),
        in_specs=[
            pl.BlockSpec((ROW_TILE, d), lambda i, j: (i, 0)),
            pl.BlockSpec((1, d, tn), lambda i, j: (layer, 0, j + j0)),
        ],
        out_specs=pl.BlockSpec((ROW_TILE, tn), lambda i, j: (i, j)),
        out_shape=jax.ShapeDtypeStruct((t, ncols), out_dtype),
        compiler_params=_params("parallel", "arbitrary"),
        name=name,
    )(u, w)


def _proj_heads_kernel(u_ref, w_ref, g_ref, cc_ref, se_ref, so_ref, o_ref, *, scale):
    acc = jnp.dot(u_ref[...], w_ref[0], preferred_element_type=F32)
    for hh in range(acc.shape[1] // HEAD_DIM):
        y = acc[:, hh * HEAD_DIM:(hh + 1) * HEAD_DIM]
        y = y * lax.rsqrt(jnp.mean(y * y, axis=-1, keepdims=True) + EPS) * g_ref[...]
        y = (y * cc_ref[...] + pltpu.roll(y, LANES - 1, 1) * se_ref[...]
             + pltpu.roll(y, 1, 1) * so_ref[...])
        o_ref[:, hh * HEAD_DIM:(hh + 1) * HEAD_DIM] = (y * scale).astype(o_ref.dtype)


def _proj_heads(u, w, layer, col0, ncols, gain, rope, scale, name):
    t, d = u.shape
    tn = _col_tile(col0, ncols)
    j0 = col0 // tn
    cc, se, so = rope
    tab = pl.BlockSpec((ROW_TILE, LANES), lambda i, j: (i, 0))
    return pl.pallas_call(
        functools.partial(_proj_heads_kernel, scale=scale),
        grid=(t // ROW_TILE, ncols // tn),
        in_specs=[
            pl.BlockSpec((ROW_TILE, d), lambda i, j: (i, 0)),
            pl.BlockSpec((1, d, tn), lambda i, j: (layer, 0, j + j0)),
            pl.BlockSpec((1, LANES), lambda i, j: (0, 0)),
            tab, tab, tab,
        ],
        out_specs=pl.BlockSpec((ROW_TILE, tn), lambda i, j: (i, j)),
        out_shape=jax.ShapeDtypeStruct((t, ncols), BF16),
        compiler_params=_params("parallel", "arbitrary"),
        name=name,
    )(u, w, gain.reshape(1, LANES), cc, se, so)


def _merge_kernel(ya_ref, yb_ref, yc_ref, wa_ref, wb_ref, wc_ref, ga_ref, gb_ref, gc_ref, o_ref):
    m = ga_ref[...].astype(F32) * jnp.dot(ya_ref[...], wa_ref[0], preferred_element_type=F32)
    m += gb_ref[...].astype(F32) * jnp.dot(yb_ref[...], wb_ref[0], preferred_element_type=F32)
    m += gc_ref[...].astype(F32) * jnp.dot(yc_ref[...], wc_ref[0], preferred_element_type=F32)
    o_ref[...] = m.astype(o_ref.dtype)


def _merge(ya, yb, yc, wa, wb, wc, layer, gates):
    t, w = ya.shape
    d = wa.shape[-1]
    nj = d // COL_TILE
    y_spec = pl.BlockSpec((ROW_TILE, w), lambda i, j: (i, 0))
    w_spec = pl.BlockSpec((1, w, COL_TILE), lambda i, j: (layer, 0, j))

    def g_spec(k):
        return pl.BlockSpec((ROW_TILE, COL_TILE), lambda i, j: (i, j + k * nj))

    return pl.pallas_call(
        _merge_kernel,
        grid=(t // ROW_TILE, nj),
        in_specs=[y_spec, y_spec, y_spec, w_spec, w_spec, w_spec, g_spec(0), g_spec(1), g_spec(2)],
        out_specs=pl.BlockSpec((ROW_TILE, COL_TILE), lambda i, j: (i, j)),
        out_shape=jax.ShapeDtypeStruct((t, d), BF16),
        compiler_params=_params("parallel", "arbitrary"),
        name="merge",
    )(ya, yb, yc, wa, wb, wc, gates, gates, gates)


def _lru_kernel(pa_ref, pg_ref, cw_ref, wa_ref, ba_ref, wx_ref, bx_ref, lam_ref, h0_ref,
                ya_ref, hT_ref, xs, a_sc, b_sc, *, ts, chunk):
    pad = SUBLANES
    win = chunk + 2 * pad
    n_chunks = ts // chunk
    zeros = jnp.zeros((pad, LANES), F32)
    xs[pl.ds(0, pad), :] = zeros
    xs[pl.ds(pad + ts, pad), :] = zeros

    def copy_in(c, carry):
        t0 = pl.multiple_of(c * chunk, chunk)
        xs[pl.ds(pad + t0, chunk), :] = pa_ref[pl.ds(t0, chunk), :]
        return carry

    lax.fori_loop(0, n_chunks, copy_in, 0)

    sp = [jax.nn.softplus(-lam_ref[d:d + 1, :]) for d in range(2)]

    def gates(c, carry):
        t0 = pl.multiple_of(c * chunk, chunk)
        xw = xs[pl.ds(t0, win), :]
        xa = None
        for k in range(CONV_A):
            sh = pltpu.roll(xw, (win + 1 - k) % win, 0) if k != 1 else xw
            term = sh[pad:pad + chunk, :] * cw_ref[k:k + 1, :]
            xa = term if xa is None else xa + term
        xb = xa.astype(BF16)
        for d in range(2):
            r = jax.nn.sigmoid(_dot(xb, wa_ref[d, 0]) + ba_ref[d:d + 1, :])
            i = jax.nn.sigmoid(_dot(xb, wx_ref[d, 0]) + bx_ref[d:d + 1, :])
            log_a = -LRU_C * r * sp[d]
            a = jnp.exp(log_a)
            b = jnp.sqrt(1.0 - a * a) * (i * xa)
            a_sc[d, pl.ds(t0, chunk), :] = a
            b_sc[d, pl.ds(t0, chunk), :] = b
        return carry

    lax.fori_loop(0, n_chunks, gates, 0)

    row = lax.broadcasted_iota(jnp.int32, (SUBLANES, LANES), 0)
    steps = (1, 2, 4)

    def scan(j, carry):
        cf, cb = carry
        tf = pl.multiple_of(j * SUBLANES, SUBLANES)
        tb = pl.multiple_of(ts - (j + 1) * SUBLANES, SUBLANES)
        af = a_sc[0, pl.ds(tf, SUBLANES), :]
        bf = b_sc[0, pl.ds(tf, SUBLANES), :]
        ab = a_sc[1, pl.ds(tb, SUBLANES), :]
        bb = b_sc[1, pl.ds(tb, SUBLANES), :]
        for s in steps:
            mf = row >= s
            bf = bf + af * jnp.where(mf, pltpu.roll(bf, s, 0), 0.0)
            af = af * jnp.where(mf, pltpu.roll(af, s, 0), 1.0)
            mb = row < SUBLANES - s
            bb = bb + ab * jnp.where(mb, pltpu.roll(bb, SUBLANES - s, 0), 0.0)
            ab = ab * jnp.where(mb, pltpu.roll(ab, SUBLANES - s, 0), 1.0)
        hf = bf + af * cf
        hb = bb + ab * cb
        b_sc[0, pl.ds(tf, SUBLANES), :] = hf
        b_sc[1, pl.ds(tb, SUBLANES), :] = hb
        cf = jnp.broadcast_to(hf[SUBLANES - 1:SUBLANES, :], (SUBLANES, LANES))
        cb = jnp.broadcast_to(hb[0:1, :], (SUBLANES, LANES))
        return cf, cb

    c0 = (jnp.broadcast_to(h0_ref[0:1, :], (SUBLANES, LANES)),
          jnp.broadcast_to(h0_ref[1:2, :], (SUBLANES, LANES)))
    cf, cb = lax.fori_loop(0, ts // SUBLANES, scan, c0)
    hT_ref[0:1, :] = cf[0:1, :]
    hT_ref[1:2, :] = cb[0:1, :]

    def finish(c, carry):
        t0 = pl.multiple_of(c * chunk, chunk)
        h = b_sc[0, pl.ds(t0, chunk), :] + b_sc[1, pl.ds(t0, chunk), :]
        g = jax.nn.gelu(pg_ref[pl.ds(t0, chunk), :], approximate=True)
        ya_ref[pl.ds(t0, chunk), :] = (h * g).astype(ya_ref.dtype)
        return carry

    lax.fori_loop(0, n_chunks, finish, 0)


def _lru(p_ax, p_ag, row0, ts, col_ax, col_ag, cw, wa, ba, wx, bx, lam, h0):
    w = cw.shape[-1]
    nblk = w // LANES
    rb = row0 // ts
    cax = col_ax // LANES
    cag = col_ag // LANES
    chunk = min(EW_ROWS, ts)
    kern = functools.partial(_lru_kernel, ts=ts, chunk=chunk)
    vec = pl.BlockSpec((2, LANES), lambda j: (0, j))
    mat = pl.BlockSpec((2, 1, LANES, LANES), lambda j: (0, j, 0, 0))
    return pl.pallas_call(
        kern,
        grid=(nblk,),
        in_specs=[
            pl.BlockSpec((ts, LANES), lambda j: (rb, cax + j)),
            pl.BlockSpec((ts, LANES), lambda j: (rb, cag + j)),
            pl.BlockSpec((CONV_A, LANES), lambda j: (0, j)),
            mat, vec, mat, vec, vec, vec,
        ],
        out_specs=[pl.BlockSpec((ts, LANES), lambda j: (0, j)),
                   pl.BlockSpec((2, LANES), lambda j: (0, j))],
        out_shape=[jax.ShapeDtypeStruct((ts, w), BF16),
                   jax.ShapeDtypeStruct((2, w), F32)],
        scratch_shapes=[pltpu.VMEM((ts + 2 * SUBLANES, LANES), F32),
                        pltpu.VMEM((2, ts, LANES), F32),
                        pltpu.VMEM((2, ts, LANES), F32)],
        compiler_params=_params("parallel"),
        name="rglru",
    )(p_ax, p_ag, cw, wa, ba, wx, bx, lam, h0)


def _conv3_kernel(x_ref, xp_ref, xn_ref, w_ref, o_ref):
    i = pl.program_id(0)
    first = i == 0
    last = i == pl.num_programs(0) - 1
    x = x_ref[...]
    r = x.shape[0]
    row = lax.broadcasted_iota(jnp.int32, (r, 1), 0)
    prev_row = jnp.where(first, 0.0, xp_ref[SUBLANES - 1:SUBLANES, :])
    next_row = jnp.where(last, 0.0, xn_ref[0:1, :])
    xm1 = jnp.where(row == 0, prev_row, pltpu.roll(x, 1, 0))
    xp1 = jnp.where(row == r - 1, next_row, pltpu.roll(x, r - 1, 0))
    o_ref[0] = w_ref[0:1, :] * xm1 + w_ref[1:2, :] * x + w_ref[2:3, :] * xp1


def _conv3(p_b, row0, ts, col_b, w3, wmix):
    r = min(EW_ROWS, ts)
    rb = row0 // r
    hb = r // SUBLANES
    cb = col_b // wmix
    n_r = ts // r
    return pl.pallas_call(
        _conv3_kernel,
        grid=(n_r, 3),
        in_specs=[
            pl.BlockSpec((r, wmix), lambda i, j: (rb + i, cb + j)),
            pl.BlockSpec((SUBLANES, wmix), lambda i, j: (jnp.maximum((rb + i) * hb - 1, 0), cb + j)),
            pl.BlockSpec((SUBLANES, wmix),
                         lambda i, j: (jnp.minimum((rb + i + 1) * hb, (rb + n_r) * hb - 1), cb + j)),
            pl.BlockSpec((CONV_B, wmix), lambda i, j: (0, j)),
        ],
        out_specs=pl.BlockSpec((1, r, wmix), lambda i, j: (j, i, 0)),
        out_shape=jax.ShapeDtypeStruct((3, ts, wmix), F32),
        compiler_params=_params("arbitrary", "arbitrary"),
        name="hyena_conv3",
    )(p_b, p_b, p_b, w3)


def _filter_kernel(z_ref, tl_ref, w1_ref, b1_ref, fr_ref, w2_ref, b2_ref, w3_ref, w3b_ref, ad_ref,
                   k_ref, ssq_ref, *, half_tiles):
    i = pl.program_id(0)
    hi = lax.Precision.HIGHEST
    fr = fr_ref[...]
    h = jnp.sin(fr * (jnp.dot(z_ref[...], w1_ref[...], precision=hi, preferred_element_type=F32)
                      + b1_ref[...]))
    h = jnp.sin(fr * (jnp.dot(h, w2_ref[...], precision=hi, preferred_element_type=F32) + b2_ref[...]))
    decay = jnp.exp(-tl_ref[...] * ad_ref[...])
    taps = _dot(h, w3_ref[0]) * decay
    r = taps.shape[0]
    row = lax.broadcasted_iota(jnp.int32, (r, 1), 0)
    k_ref[...] = taps

    @pl.when(i == 0)
    def _():
        back = _dot(h, w3b_ref[0]) * decay
        k_ref[...] = taps + jnp.where(row == 0, back, 0.0)
        ssq_ref[...] = jnp.zeros_like(ssq_ref)

    @pl.when(i == half_tiles)
    def _():
        k_ref[...] = jnp.where(row == 0, 0.0, taps)

    kk = k_ref[...]
    ssq_ref[...] += jnp.sum(kk * kk, axis=0, keepdims=True)


def _hyena_filter(seq, p):
    wmix = p["hy_skip"].shape[-1]
    hidden = p["hy_fw1"].shape[-1]
    r = min(EW_ROWS, seq)
    t_idx = jnp.arange(seq, dtype=F32)
    t_lin = t_idx / max(seq - 1, 1)
    bands = jnp.linspace(1e-4, HYENA_BANDS - 1, HYENA_BANDS, dtype=F32)
    ang = (2.0 * math.pi / seq) * t_idx[:, None] * bands[None, :]
    z = jnp.concatenate([t_lin[:, None], jnp.cos(ang), -jnp.sin(ang)], axis=-1)
    rev = lambda a: jnp.concatenate([a[:1], jnp.flip(a[1:], axis=0)], axis=0)
    zz = jnp.concatenate([z, rev(z)], axis=0)
    zz = jnp.pad(zz, ((0, 0), (0, LANES - HYENA_EMB)))
    tl = jnp.concatenate([t_lin, rev(t_lin)])[:, None]
    w1 = jnp.pad(p["hy_fw1"], ((0, LANES - HYENA_EMB), (0, 0)))
    w3 = p["hy_fw3"].reshape(hidden, HYENA_ORDER, 2, wmix).transpose(2, 0, 1, 3)
    w3 = w3.reshape(2, hidden, HYENA_ORDER * wmix)
    deltas = jnp.linspace(math.log(HYENA_TARGET) / HYENA_SLOW_DECAY,
                          math.log(HYENA_TARGET) / HYENA_FAST_DECAY, wmix, dtype=F32)
    ad = jnp.tile(jnp.abs(deltas), HYENA_ORDER)[None, :]
    ow = HYENA_ORDER * wmix
    half = seq // r
    full = lambda shape: pl.BlockSpec(shape, lambda i: tuple(0 for _ in shape))
    return pl.pallas_call(
        functools.partial(_filter_kernel, half_tiles=half),
        grid=(2 * half,),
        in_specs=[
            pl.BlockSpec((r, LANES), lambda i: (i, 0)),
            pl.BlockSpec((r, 1), lambda i: (i, 0)),
            full((LANES, hidden)), full((1, hidden)), full((1, hidden)),
            full((hidden, hidden)), full((1, hidden)),
            pl.BlockSpec((1, hidden, ow), lambda i: (jnp.where(i >= half, 1, 0), 0, 0)),
            pl.BlockSpec((1, hidden, ow), lambda i: (1, 0, 0)),
            full((1, ow)),
        ],
        out_specs=[pl.BlockSpec((r, ow), lambda i: (i, 0)),
                   pl.BlockSpec((1, ow), lambda i: (0, 0))],
        out_shape=[jax.ShapeDtypeStruct((2 * seq, ow), F32),
                   jax.ShapeDtypeStruct((1, ow), F32)],
        compiler_params=_params("arbitrary"),
        name="hyena_filter",
    )(zz, tl, w1, p["hy_fb1"][None, :], p["hy_freq"][None, :], p["hy_fw2"], p["hy_fb2"][None, :],
      w3, w3, ad)


def _kron_fwd_kernel(f_ref, x_ref, o_ref):
    f = f_ref[...]
    halves = []
    for h in range(2):
        xh = x_ref[0, :, h * SUBLANES:(h + 1) * SUBLANES, :]
        xh = xh.reshape(xh.shape[0] * SUBLANES, xh.shape[2]).astype(BF16)
        r = jnp.dot(f, xh, preferred_element_type=F32)
        halves.append(r.reshape(r.shape[0] // SUBLANES, SUBLANES, r.shape[1]))
    o_ref[...] = jnp.concatenate(halves, axis=1).astype(o_ref.dtype)


def _kron_fwd(fk, x4, sel, name):
    _, nt1, nt2, w = x4.shape
    rows = fk.shape[0] // SUBLANES
    tw = 512 if w % 512 == 0 else LANES
    rt = 2 * SUBLANES
    return pl.pallas_call(
        _kron_fwd_kernel,
        grid=(nt2 // rt, w // tw),
        in_specs=[pl.BlockSpec(fk.shape, lambda i, j: (0, 0)),
                  pl.BlockSpec((1, nt1, rt, tw), lambda i, j: (sel, 0, i, j))],
        out_specs=pl.BlockSpec((rows, rt, tw), lambda i, j: (0, i, j)),
        out_shape=jax.ShapeDtypeStruct((rows, nt2, w), BF16),
        compiler_params=_params("parallel", "parallel"),
        name=name,
    )(fk, x4)


def _kron_inv_gate_kernel(g_ref, b_ref, x_ref, v_ref, sk_ref, o_ref):
    g = g_ref[...]
    b = b_ref[...].astype(F32)
    sk = sk_ref[...]
    halves = []
    for h in range(2):
        lo, hi = h * SUBLANES, (h + 1) * SUBLANES
        bh = b[:, lo:hi, :]
        bh = bh.reshape(bh.shape[0] * SUBLANES, bh.shape[2]).astype(BF16)
        y = jnp.dot(g, bh, preferred_element_type=F32)
        y = y.reshape(y.shape[0] // SUBLANES, SUBLANES, y.shape[1])
        halves.append(x_ref[0, :, lo:hi, :] * (y + sk * v_ref[0, :, lo:hi, :]))
    o_ref[0] = jnp.concatenate(halves, axis=1).astype(o_ref.dtype)


def _kron_inv_gate(gk, b3, x4, x_sel, v4, v_sel, sk, out_dtype, name):
    _, nt2, w = b3.shape
    nt1 = gk.shape[0] // SUBLANES
    tw = 512 if w % 512 == 0 else LANES
    rt = 2 * SUBLANES
    return pl.pallas_call(
        _kron_inv_gate_kernel,
        grid=(nt2 // rt, w // tw),
        in_specs=[pl.BlockSpec(gk.shape, lambda i, j: (0, 0)),
                  pl.BlockSpec((b3.shape[0], rt, tw), lambda i, j: (0, i, j)),
                  pl.BlockSpec((1, nt1, rt, tw), lambda i, j: (x_sel, 0, i, j)),
                  pl.BlockSpec((1, nt1, rt, tw), lambda i, j: (v_sel, 0, i, j)),
                  pl.BlockSpec((1, 1, tw), lambda i, j: (0, 0, j))],
        out_specs=pl.BlockSpec((1, nt1, rt, tw), lambda i, j: (0, 0, i, j)),
        out_shape=jax.ShapeDtypeStruct((1, nt1, nt2, w), out_dtype),
        compiler_params=_params("parallel", "parallel"),
        name=name,
    )(gk, b3, x4, v4, sk.reshape(1, 1, w))


def _bmm_scale_kernel(m_ref, a_ref, s_ref, o_ref, *, bpb):
    rows_in = a_ref.shape[0] // bpb
    rows_out = o_ref.shape[0] // bpb
    for b in range(bpb):
        y = _dot(m_ref[b], a_ref[b * rows_in:(b + 1) * rows_in, :]) * s_ref[...]
        o_ref[b * rows_out:(b + 1) * rows_out, :] = y.astype(o_ref.dtype)


def _bmm_scale(m2, a, scale, rows_in, name):
    nb, rows_out, _ = m2.shape
    n = a.shape[1]
    tn = n // 2 if (n // 2) % LANES == 0 else n
    bpb = 4 if nb % 4 == 0 else 1
    return pl.pallas_call(
        functools.partial(_bmm_scale_kernel, bpb=bpb),
        grid=(nb // bpb, n // tn),
        in_specs=[pl.BlockSpec((bpb, rows_out, rows_in), lambda b, j: (b, 0, 0)),
                  pl.BlockSpec((bpb * rows_in, tn), lambda b, j: (b, j)),
                  pl.BlockSpec((1, tn), lambda b, j: (0, j))],
        out_specs=pl.BlockSpec((bpb * rows_out, tn), lambda b, j: (b, j)),
        out_shape=jax.ShapeDtypeStruct((nb * rows_out, n), BF16),
        compiler_params=_params("parallel", "arbitrary"),
        name=name,
    )(m2, a, scale)


def _spectral_one(m2, a, k, m3):
    x = _dot(m2, a)
    f = x.shape[0] // 2
    xr, xi = x[:f], x[f:]
    kr, ki = k[:f].astype(F32), k[f:].astype(F32)
    y = jnp.concatenate([xr * kr - xi * ki, xr * ki + xi * kr], axis=0)
    return _dot(m3, y)


def _spectral_kernel(m2_ref, a_ref, k_ref, m3_ref, o_ref, *, bpb):
    rows_in = a_ref.shape[0] // bpb
    f2 = k_ref.shape[0] // bpb
    rows_out = o_ref.shape[0] // bpb
    for b in range(bpb):
        y = _spectral_one(m2_ref[b], a_ref[b * rows_in:(b + 1) * rows_in, :],
                          k_ref[b * f2:(b + 1) * f2, :], m3_ref[b])
        o_ref[b * rows_out:(b + 1) * rows_out, :] = y.astype(o_ref.dtype)


def _spectral_gate_kernel(m2_ref, a_ref, k_ref, m3_ref, x_ref, v_ref, sk_ref, o_ref):
    y = _spectral_one(m2_ref[0], a_ref[...], k_ref[...], m3_ref[0])
    o_ref[...] = (x_ref[...] * (y + sk_ref[...] * v_ref[...])).astype(o_ref.dtype)


def _spectral(m2, a, kspec, order, m3, out_dtype, gate=None, name="hyena_spectral"):
    nb, f2, rows_in = m2.shape
    rows_out = m3.shape[1]
    wmix = a.shape[1]
    bpb = 4 if (nb % 4 == 0 and gate is None) else 1
    in_specs = [pl.BlockSpec((bpb, f2, rows_in), lambda b: (b, 0, 0)),
                pl.BlockSpec((bpb * rows_in, wmix), lambda b: (b, 0)),
                pl.BlockSpec((bpb * f2, wmix), lambda b: (b, order)),
                pl.BlockSpec((bpb, rows_out, f2), lambda b: (b, 0, 0))]
    args = [m2, a, kspec, m3]
    kern = functools.partial(_spectral_kernel, bpb=bpb)
    if gate is not None:
        x, v, sk = gate
        blk = pl.BlockSpec((rows_out, wmix), lambda b: (b, 0))
        in_specs += [blk, blk, pl.BlockSpec((1, wmix), lambda b: (0, 0))]
        args += [x, v, sk]
        kern = _spectral_gate_kernel
    return pl.pallas_call(
        kern,
        grid=(nb // bpb,),
        in_specs=in_specs,
        out_specs=pl.BlockSpec((bpb * rows_out, wmix), lambda b: (b, 0)),
        out_shape=jax.ShapeDtypeStruct((nb * rows_out, wmix), out_dtype),
        compiler_params=_params("parallel"),
        name=name,
    )(*args)


def _dft_tables(seq):
    n = 2 * seq
    n2 = DFT_INNER
    n1 = n // n2
    i1 = jnp.arange(n1, dtype=jnp.int32)
    i2 = jnp.arange(n2, dtype=jnp.int32)
    ang1 = (2.0 * math.pi / n1) * ((i1[:, None] * i1[None, :]) % n1).astype(F32)
    c1, s1 = jnp.cos(ang1), jnp.sin(ang1)
    f1 = jnp.stack([c1, -s1], axis=1).reshape(2 * n1, n1)
    q = i1[:, None, None] + n1 * i2[None, :, None]
    ang = (2.0 * math.pi / n) * ((q * i2[None, None, :]) % n).astype(F32)
    tr, ti = jnp.cos(ang), -jnp.sin(ang)
    m2 = jnp.concatenate([jnp.concatenate([tr, -ti], axis=2),
                          jnp.concatenate([ti, tr], axis=2)], axis=1)
    trt, tit = jnp.swapaxes(tr, 1, 2), jnp.swapaxes(ti, 1, 2)
    m3 = jnp.concatenate([jnp.concatenate([trt, tit], axis=2),
                          jnp.concatenate([-tit, trt], axis=2)], axis=1)
    g = jnp.stack([c1, -s1], axis=2).reshape(n1, 2 * n1)[: n1 // 2] / n
    eye = jnp.eye(SUBLANES, dtype=F32)
    kron = lambda m: jnp.kron(m, eye).astype(BF16)
    return kron(f1[:, : n1 // 2]), kron(f1), m2.astype(BF16), m3.astype(BF16), kron(g)


def _direct_dft_tables(seq):
    n = 2 * seq
    i = jnp.arange(n, dtype=jnp.int32)
    ang = (2.0 * math.pi / n) * ((i[:, None] * i[None, :]) % n).astype(F32)
    c, s = jnp.cos(ang), jnp.sin(ang)
    fwd = jnp.concatenate([c, -s], axis=0)
    inv = jnp.concatenate([c[:seq], -s[:seq]], axis=1) / n
    return fwd[:, :seq].astype(BF16)[None], fwd.astype(BF16)[None], inv.astype(BF16)[None]


def _hyena_long(u3, taps, ssq, skip, tables):
    fk_half, fk_full, m2, m3, gk = tables
    _, seq, wmix = u3.shape
    n2 = DFT_INNER
    n1 = 2 * seq // n2
    ow = taps.shape[1]
    scale = lax.rsqrt(ssq + EPS)
    ak = _kron_fwd(fk_full, taps.reshape(1, n1, n2, ow), 0, "hyena_filter_dft1")
    kspec = _bmm_scale(m2, ak.reshape(n1 * 2 * n2, ow), scale, 2 * n2, "hyena_filter_dft2")
    u4 = u3.reshape(3, n1 // 2, n2, wmix)
    z4, z_sel = u4, 0
    for o in range(HYENA_ORDER):
        a = _kron_fwd(fk_half, z4, z_sel, "hyena_dft1")
        b = _spectral(m2, a.reshape(n1 * 2 * n2, wmix), kspec, o, m3, BF16)
        last = o == HYENA_ORDER - 1
        z4 = _kron_inv_gate(gk, b.reshape(2 * n1, n2, wmix), u4, 1 + o, z4, z_sel, skip[o],
                            BF16 if last else F32, "hyena_dft4_gate")
        z_sel = 0
    return z4.reshape(seq, wmix)


def _hyena_short(u3, taps, ssq, skip, tables):
    fwd_half, fwd_full, inv = tables
    scale = lax.rsqrt(ssq + EPS)
    kspec = _bmm_scale(fwd_full, taps, scale, taps.shape[0], "hyena_ctx_filter_dft")
    z = u3[0]
    for o in range(HYENA_ORDER):
        last = o == HYENA_ORDER - 1
        z = _spectral(fwd_half, z, kspec, o, inv, BF16 if last else F32,
                      gate=(u3[1 + o], z, skip[o][None, :]), name="hyena_ctx_spectral")
    return z


def _attn_kernel(q_ref, k_ref, v_ref, o_ref, m_sc, l_sc, acc_sc, s_sc, p_sc, a_sc, *, rows):
    j = pl.program_id(2)
    _, tq, tk = s_sc.shape
    nlb = tk // LANES

    @pl.when(j == 0)
    def _():
        m_sc[...] = jnp.full_like(m_sc, -jnp.inf)
        l_sc[...] = jnp.zeros_like(l_sc)
        acc_sc[...] = jnp.zeros_like(acc_sc)

    k = k_ref[...]
    v = v_ref[...]
    for g in range(GQA_GROUP):
        q = q_ref[:, g * HEAD_DIM:(g + 1) * HEAD_DIM]
        s_sc[g] = lax.dot_general(q, k, (((1,), (1,)), ((), ())), preferred_element_type=F32)

    for g in range(GQA_GROUP):
        for c in range(tq // rows):
            rs = slice(c * rows, (c + 1) * rows)
            blocks = [s_sc[g, rs, b * LANES:(b + 1) * LANES] for b in range(nlb)]
            bmax = blocks[0]
            for blk in blocks[1:]:
                bmax = jnp.maximum(bmax, blk)
            m_prev = m_sc[g, rs, :]
            m_new = jnp.maximum(m_prev, jnp.max(bmax, axis=1, keepdims=True))
            alpha = jnp.exp2(m_prev - m_new)
            psum = None
            for b, blk in enumerate(blocks):
                p = jnp.exp2(blk - m_new)
                psum = p if psum is None else psum + p
                p_sc[g, rs, b * LANES:(b + 1) * LANES] = p.astype(BF16)
            l_sc[g, rs, :] = alpha * l_sc[g, rs, :] + jnp.sum(psum, axis=1, keepdims=True)
            m_sc[g, rs, :] = m_new
            a_sc[g, rs, :] = alpha

    for g in range(GQA_GROUP):
        acc_sc[g] = a_sc[g] * acc_sc[g] + jnp.dot(p_sc[g], v, preferred_element_type=F32)

    @pl.when(j == pl.num_programs(2) - 1)
    def _():
        for g in range(GQA_GROUP):
            o_ref[:, g * HEAD_DIM:(g + 1) * HEAD_DIM] = (acc_sc[g] / l_sc[g]).astype(o_ref.dtype)


def _attention(q, k, v, q_row0, n_q, k_row0, n_k, tq, tk):
    n_kv = k.shape[1] // HEAD_DIM
    gw = GQA_GROUP * HEAD_DIM
    qb, kb = q_row0 // tq, k_row0 // tk
    return pl.pallas_call(
        functools.partial(_attn_kernel, rows=32),
        grid=(n_kv, n_q // tq, n_k // tk),
        in_specs=[pl.BlockSpec((tq, gw), lambda h, i, j: (qb + i, h)),
                  pl.BlockSpec((tk, HEAD_DIM), lambda h, i, j: (kb + j, h)),
                  pl.BlockSpec((tk, HEAD_DIM), lambda h, i, j: (kb + j, h))],
        out_specs=pl.BlockSpec((tq, gw), lambda h, i, j: (i, h)),
        out_shape=jax.ShapeDtypeStruct((n_q, q.shape[1]), BF16),
        scratch_shapes=[pltpu.VMEM((GQA_GROUP, tq, LANES), F32),
                        pltpu.VMEM((GQA_GROUP, tq, LANES), F32),
                        pltpu.VMEM((GQA_GROUP, tq, HEAD_DIM), F32),
                        pltpu.VMEM((GQA_GROUP, tq, tk), F32),
                        pltpu.VMEM((GQA_GROUP, tq, tk), BF16),
                        pltpu.VMEM((GQA_GROUP, tq, LANES), F32)],
        compiler_params=_params("parallel", "parallel", "arbitrary"),
        name="attention",
    )(q, k, v)


def _rope_tables(seq, n_ctx):
    rows = seq // GRID_W
    row = jnp.repeat(jnp.arange(rows, dtype=F32), GRID_W)
    col = jnp.tile(jnp.arange(GRID_W, dtype=F32), rows)
    n_pairs = HEAD_DIM // 4
    inv = ROPE_THETA ** (-jnp.arange(n_pairs, dtype=F32) / n_pairs)
    ang = jnp.concatenate([row[:, None] * inv, col[:, None] * inv], axis=-1)
    ang = jnp.concatenate([ang, jnp.zeros((n_ctx, HEAD_DIM // 2), F32)], axis=0)
    c = jnp.repeat(jnp.cos(ang), 2, axis=1)
    s = jnp.repeat(jnp.sin(ang), 2, axis=1)
    even = (jnp.arange(HEAD_DIM) % 2 == 0)[None, :]
    return c, jnp.where(even, -s, 0.0), jnp.where(even, 0.0, s)


def kernel(x, c, ctx, c_ctx, w_mod_down, w_mod_up, b_mod, norm_ffn1, norm_mix, norm_ffn2,
           ffn1_w_in, ffn1_w_out, ffn2_w_in, ffn2_w_out, w_in, lru_conv, lru_w_a, lru_b_a,
           lru_w_x, lru_b_x, lru_lambda, hy_conv, hy_fw1, hy_fb1, hy_freq, hy_fw2, hy_fb2,
           hy_fw3, hy_skip, q_norm, k_norm, w_branch_a, w_branch_b, w_branch_c, w_out, final_norm):
    bsz, seq, d = x.shape
    assert bsz == 1 and c.shape[0] == 1 and ctx.shape[0] == 1
    n_ctx = ctx.shape[1]
    depth = w_in.shape[0]
    wmix = lru_conv.shape[-1]
    kvw = wmix // GQA_GROUP
    t_all = seq + n_ctx
    assert t_all % ROW_TILE == 0 and seq % EW_ROWS == 0 and n_ctx % EW_ROWS == 0

    col_ax = 0
    col_ck = col_ax + wmix
    col_cv = col_ck + kvw
    col_ag = col_cv + kvw
    col_b = col_ag + wmix
    col_cq = col_b + 3 * wmix
    col_g = col_cq + wmix

    xs = jnp.concatenate([x[0], ctx[0]], axis=0)
    cc = jnp.zeros((SUBLANES, d), F32).at[0].set(c[0]).at[1].set(c_ctx)
    mods_all = _modulation(cc, w_mod_down, w_mod_up, b_mod)

    ffn1_in, ffn1_out = ffn1_w_in.astype(BF16), ffn1_w_out.astype(BF16)
    ffn2_in, ffn2_out = ffn2_w_in.astype(BF16), ffn2_w_out.astype(BF16)
    w_in_b = w_in.astype(BF16)
    wba, wbb, wbc = w_branch_a.astype(BF16), w_branch_b.astype(BF16), w_branch_c.astype(BF16)
    w_out_b = w_out.astype(BF16)

    rope = _rope_tables(seq, n_ctx)
    dft_lat = _dft_tables(seq)
    dft_ctx = _direct_dft_tables(n_ctx)
    q_scale = HEAD_DIM ** -0.5 * math.log2(math.e)
    tq_lat = 512 if seq % 512 == 0 else EW_ROWS
    tk_lat = next(t for t in (1408, ROW_TILE) if t_all % t == 0)

    for i in range(depth):
        ctx_out = i < depth - 1
        mods = mods_all[i]
        mods3 = mods.reshape(2 * N_MOD, 1, d)

        u = _norm_mod(xs, norm_ffn1[i], mods3, 0, seq)
        h = _ffn_up(u, ffn1_in, i)
        xs = _down(h, ffn1_out, i, xs, mods[:, 2], 0.5, seq)

        u = _norm_mod(xs, norm_mix[i], mods3, 3, seq)
        p_ax = _proj(u, w_in_b, i, col_ax, wmix, False, F32, "mixer_in_lru")
        p_b = _proj(u, w_in_b, i, col_ag, 4 * wmix, False, F32, "mixer_in_gelu_hyena")
        kh = _proj_heads(u, w_in_b, i, col_ck, kvw, k_norm[i], rope, 1.0, "mixer_in_k")
        vh = _proj(u, w_in_b, i, col_cv, kvw, False, BF16, "mixer_in_v")
        qh = _proj_heads(u, w_in_b, i, col_cq, wmix, q_norm[i], rope, q_scale, "mixer_in_q")
        gates = _proj(u, w_in_b, i, col_g, 3 * d, True, BF16, "mixer_gates")

        lru_args = (lru_conv[i], lru_w_a[i], lru_b_a[i], lru_w_x[i], lru_b_x[i], lru_lambda[i])
        ya_c, h_c = _lru(p_ax, p_b, seq, n_ctx, 0, 0, *lru_args, jnp.zeros((2, wmix), F32))
        ya_l, _ = _lru(p_ax, p_b, 0, seq, 0, 0, *lru_args, h_c)

        hp = {"hy_fw1": hy_fw1[i], "hy_fb1": hy_fb1[i], "hy_freq": hy_freq[i], "hy_fw2": hy_fw2[i],
              "hy_fb2": hy_fb2[i], "hy_fw3": hy_fw3[i], "hy_skip": hy_skip[i]}
        taps_l, ssq_l = _hyena_filter(seq, hp)
        u3_l = _conv3(p_b, 0, seq, wmix, hy_conv[i], wmix)
        yb_l = _hyena_long(u3_l, taps_l, ssq_l, hy_skip[i], dft_lat)

        yc_l = _attention(qh, kh, vh, 0, seq, 0, t_all, tq_lat, tk_lat)

        if ctx_out:
            taps_c, ssq_c = _hyena_filter(n_ctx, hp)
            u3_c = _conv3(p_b, seq, n_ctx, wmix, hy_conv[i], wmix)
            yb_c = _hyena_short(u3_c, taps_c, ssq_c, hy_skip[i], dft_ctx)
            yc_c = _attention(qh, kh, vh, seq, n_ctx, seq, n_ctx, n_ctx, n_ctx)
        else:
            yb_c = jnp.zeros((n_ctx, wmix), BF16)
            yc_c = jnp.zeros((n_ctx, wmix), BF16)

        ya = jnp.concatenate([ya_l, ya_c], axis=0)
        yb = jnp.concatenate([yb_l, yb_c], axis=0)
        yc = jnp.concatenate([yc_l, yc_c], axis=0)
        m = _merge(ya, yb, yc, wba, wbb, wbc, i, gates)
        xs = _down(m, w_out_b, i, xs, mods[:, 5], 1.0, seq)

        u = _norm_mod(xs, norm_ffn2[i], mods3, 6, seq)
        h = _ffn_up(u, ffn2_in, i)
        xs = _down(h, ffn2_out, i, xs, mods[:, 8], 0.5, seq)

    return _final_norm(xs, final_norm, seq)[None]
```

```python
import functools
import math

import jax
import jax.numpy as jnp
from jax import lax
from jax.experimental import pallas as pl
from jax.experimental.pallas import tpu as pltpu

F32 = jnp.float32
BF16 = jnp.bfloat16

HEAD_DIM = 128
LANES = 128
SUBLANES = 8
GQA_GROUP = 3
GRID_W = 64
ROPE_THETA = 10000.0
LRU_C = 8.0
CONV_A = 4
CONV_B = 3
HYENA_ORDER = 2
HYENA_BANDS = 16
HYENA_EMB = 2 * HYENA_BANDS + 1
HYENA_FAST_DECAY = 0.3
HYENA_SLOW_DECAY = 1.5
HYENA_TARGET = 1e-2
N_MOD = 9
EPS = 1e-6
DFT_INNER = 128
VMEM_LIMIT = 56 * 1024 * 1024

ROW_TILE = 768
COL_TILE = 512
EW_ROWS = 256


def _params(*sem):
    return pltpu.CompilerParams(dimension_semantics=sem, vmem_limit_bytes=VMEM_LIMIT)


def _dot(a, b):
    return jnp.dot(a.astype(BF16), b.astype(BF16), preferred_element_type=F32)


def _mod_kernel(c_ref, wd_ref, wu_ref, b_ref, o_ref):
    c = c_ref[...]
    s = c * jax.nn.sigmoid(c)
    t = _dot(s, wd_ref[0])
    o_ref[0] = _dot(t, wu_ref[0]) + b_ref[0]


def _modulation(cc, w_down, w_up, b_mod):
    depth, d, rank = w_down.shape
    out = pl.pallas_call(
        _mod_kernel,
        grid=(depth, N_MOD),
        in_specs=[
            pl.BlockSpec((SUBLANES, d), lambda l, j: (0, 0)),
            pl.BlockSpec((1, d, rank), lambda l, j: (l, 0, 0)),
            pl.BlockSpec((1, rank, d), lambda l, j: (l, 0, j)),
            pl.BlockSpec((1, 1, d), lambda l, j: (l, 0, j)),
        ],
        out_specs=pl.BlockSpec((1, SUBLANES, d), lambda l, j: (l, 0, j)),
        out_shape=jax.ShapeDtypeStruct((depth, SUBLANES, N_MOD * d), F32),
        compiler_params=_params("arbitrary", "arbitrary"),
        name="modulation",
    )(cc, w_down, w_up, b_mod.reshape(depth, 1, N_MOD * d))
    return out.reshape(depth, SUBLANES, N_MOD, d)[:, :2]


def _norm_mod_kernel(x_ref, g_ref, sh_ref, sc_ref, o_ref):
    x = x_ref[...]
    y = x * lax.rsqrt(jnp.mean(x * x, axis=-1, keepdims=True) + EPS)
    y = y * g_ref[...]
    o_ref[...] = (y * (1.0 + sc_ref[0]) + sh_ref[0]).astype(o_ref.dtype)


def _norm_mod(x, g, mods, idx, n_lat):
    t, d = x.shape
    nl = n_lat // EW_ROWS

    def sel(i, k):
        return (jnp.where(i >= nl, N_MOD, 0) + k, 0, 0)

    return pl.pallas_call(
        _norm_mod_kernel,
        grid=(t // EW_ROWS,),
        in_specs=[
            pl.BlockSpec((EW_ROWS, d), lambda i: (i, 0)),
            pl.BlockSpec((1, d), lambda i: (0, 0)),
            pl.BlockSpec((1, 1, d), lambda i: sel(i, idx)),
            pl.BlockSpec((1, 1, d), lambda i: sel(i, idx + 1)),
        ],
        out_specs=pl.BlockSpec((EW_ROWS, d), lambda i: (i, 0)),
        out_shape=jax.ShapeDtypeStruct((t, d), BF16),
        compiler_params=_params("parallel"),
        name="norm_mod",
    )(x, g.reshape(1, d), mods, mods)


def _final_norm_kernel(x_ref, g_ref, o_ref):
    x = x_ref[...]
    y = x * lax.rsqrt(jnp.mean(x * x, axis=-1, keepdims=True) + EPS)
    o_ref[...] = y * g_ref[...]


def _final_norm(x, g, n_lat):
    t, d = x.shape
    return pl.pallas_call(
        _final_norm_kernel,
        grid=(n_lat // EW_ROWS,),
        in_specs=[pl.BlockSpec((EW_ROWS, d), lambda i: (i, 0)),
                  pl.BlockSpec((1, d), lambda i: (0, 0))],
        out_specs=pl.BlockSpec((EW_ROWS, d), lambda i: (i, 0)),
        out_shape=jax.ShapeDtypeStruct((n_lat, d), F32),
        compiler_params=_params("parallel"),
        name="final_norm",
    )(x, g.reshape(1, d))


def _round_weights(w_ref, wb_ref):
    @pl.when(pl.program_id(1) == 0)
    def _():
        wb_ref[...] = w_ref[0].astype(BF16)


def _ffn_up_kernel(u_ref, wg_ref, wu_ref, o_ref, wgb, wub):
    _round_weights(wg_ref, wgb)
    _round_weights(wu_ref, wub)
    u = u_ref[...]
    a = jnp.dot(u, wgb[...], preferred_element_type=F32)
    b = jnp.dot(u, wub[...], preferred_element_type=F32)
    o_ref[...] = (a * jax.nn.sigmoid(a) * b).astype(o_ref.dtype)


def _ffn_up(u, w_gu, layer):
    t, d = u.shape
    f = w_gu.shape[-1] // 2
    tn = COL_TILE // 2
    nj = f // tn
    return pl.pallas_call(
        _ffn_up_kernel,
        grid=(nj, t // ROW_TILE),
        in_specs=[
            pl.BlockSpec((ROW_TILE, d), lambda j, i: (i, 0)),
            pl.BlockSpec((1, d, tn), lambda j, i: (layer, 0, j)),
            pl.BlockSpec((1, d, tn), lambda j, i: (layer, 0, j + nj)),
        ],
        out_specs=pl.BlockSpec((ROW_TILE, tn), lambda j, i: (i, j)),
        out_shape=jax.ShapeDtypeStruct((t, f), BF16),
        scratch_shapes=[pltpu.VMEM((d, tn), BF16), pltpu.VMEM((d, tn), BF16)],
        compiler_params=_params("parallel", "arbitrary"),
        name="ffn_up",
    )(u, w_gu, w_gu)


def _down_kernel(h_ref, w_ref, x_ref, g_ref, o_ref, wb, *, coef, n_lat):
    _round_weights(w_ref, wb)
    acc = jnp.dot(h_ref[...], wb[...], preferred_element_type=F32)
    tm = acc.shape[0]
    row = pl.program_id(1) * tm + lax.broadcasted_iota(jnp.int32, (tm, 1), 0)
    g = jnp.where(row >= n_lat, g_ref[1:2, :], g_ref[0:1, :])
    o_ref[...] = x_ref[...] + coef * g * acc


def _down(h, w, layer, x, gates, coef, n_lat):
    t, k = h.shape
    d = w.shape[-1]
    return pl.pallas_call(
        functools.partial(_down_kernel, coef=coef, n_lat=n_lat),
        grid=(d // COL_TILE, t // ROW_TILE),
        in_specs=[
            pl.BlockSpec((ROW_TILE, k), lambda j, i: (i, 0)),
            pl.BlockSpec((1, k, COL_TILE), lambda j, i: (layer, 0, j)),
            pl.BlockSpec((ROW_TILE, COL_TILE), lambda j, i: (i, j)),
            pl.BlockSpec((2, COL_TILE), lambda j, i: (0, j)),
        ],
        out_specs=pl.BlockSpec((ROW_TILE, COL_TILE), lambda j, i: (i, j)),
        out_shape=jax.ShapeDtypeStruct((t, d), F32),
        input_output_aliases={2: 0},
        scratch_shapes=[pltpu.VMEM((k, COL_TILE), BF16)],
        compiler_params=_params("parallel", "arbitrary"),
        name="down_residual",
    )(h, w, x, gates)


def _proj_kernel(u_ref, w_ref, o_ref, wb, *, sigmoid):
    _round_weights(w_ref, wb)
    acc = jnp.dot(u_ref[...], wb[...], preferred_element_type=F32)
    if sigmoid:
        acc = jax.nn.sigmoid(acc)
    o_ref[...] = acc.astype(o_ref.dtype)


def _col_tile(col0, ncols):
    return next(t for t in (COL_TILE, 256, LANES) if col0 % t == 0 and ncols % t == 0)


def _proj(u, w, layer, col0, ncols, sigmoid, out_dtype, name):
    t, d = u.shape
    tn = _col_tile(col0, ncols)
    j0 = col0 // tn
    return pl.pallas_call(
        functools.partial(_proj_kernel, sigmoid=sigmoid),
        grid=(ncols // tn, t // ROW_TILE),
        in_specs=[
            pl.BlockSpec((ROW_TILE, d), lambda j, i: (i, 0)),
            pl.BlockSpec((1, d, tn), lambda j, i: (layer, 0, j + j0)),
        ],
        out_specs=pl.BlockSpec((ROW_TILE, tn), lambda j, i: (i, j)),
        out_shape=jax.ShapeDtypeStruct((t, ncols), out_dtype),
        scratch_shapes=[pltpu.VMEM((d, tn), BF16)],
        compiler_params=_params("parallel", "arbitrary"),
        name=name,
    )(u, w)


def _proj_heads_kernel(u_ref, w_ref, g_ref, cc_ref, se_ref, so_ref, o_ref, wb, *, scale):
    _round_weights(w_ref, wb)
    acc = jnp.dot(u_ref[...], wb[...], preferred_element_type=F32)
    for hh in range(acc.shape[1] // HEAD_DIM):
        y = acc[:, hh * HEAD_DIM:(hh + 1) * HEAD_DIM]
        y = y * lax.rsqrt(jnp.mean(y * y, axis=-1, keepdims=True) + EPS) * g_ref[...]
        y = (y * cc_ref[...] + pltpu.roll(y, LANES - 1, 1) * se_ref[...]
             + pltpu.roll(y, 1, 1) * so_ref[...])
        o_ref[:, hh * HEAD_DIM:(hh + 1) * HEAD_DIM] = (y * scale).astype(o_ref.dtype)


def _proj_heads(u, w, layer, col0, ncols, gain, rope, scale, name):
    t, d = u.shape
    tn = _col_tile(col0, ncols)
    j0 = col0 // tn
    cc, se, so = rope
    tab = pl.BlockSpec((ROW_TILE, LANES), lambda j, i: (i, 0))
    return pl.pallas_call(
        functools.partial(_proj_heads_kernel, scale=scale),
        grid=(ncols // tn, t // ROW_TILE),
        in_specs=[
            pl.BlockSpec((ROW_TILE, d), lambda j, i: (i, 0)),
            pl.BlockSpec((1, d, tn), lambda j, i: (layer, 0, j + j0)),
            pl.BlockSpec((1, LANES), lambda j, i: (0, 0)),
            tab, tab, tab,
        ],
        out_specs=pl.BlockSpec((ROW_TILE, tn), lambda j, i: (i, j)),
        out_shape=jax.ShapeDtypeStruct((t, ncols), BF16),
        scratch_shapes=[pltpu.VMEM((d, tn), BF16)],
        compiler_params=_params("parallel", "arbitrary"),
        name=name,
    )(u, w, gain.reshape(1, LANES), cc, se, so)


def _merge_kernel(ya_ref, yb_ref, yc_ref, wa_ref, wb_ref, wc_ref, ga_ref, gb_ref, gc_ref, o_ref,
                  wab, wbb, wcb):
    _round_weights(wa_ref, wab)
    _round_weights(wb_ref, wbb)
    _round_weights(wc_ref, wcb)
    m = ga_ref[...].astype(F32) * jnp.dot(ya_ref[...], wab[...], preferred_element_type=F32)
    m += gb_ref[...].astype(F32) * jnp.dot(yb_ref[...], wbb[...], preferred_element_type=F32)
    m += gc_ref[...].astype(F32) * jnp.dot(yc_ref[...], wcb[...], preferred_element_type=F32)
    o_ref[...] = m.astype(o_ref.dtype)


def _merge(ya, yb, yc, wa, wb, wc, layer, gates):
    t, w = ya.shape
    d = wa.shape[-1]
    nj = d // COL_TILE
    y_spec = pl.BlockSpec((ROW_TILE, w), lambda j, i: (i, 0))
    w_spec = pl.BlockSpec((1, w, COL_TILE), lambda j, i: (layer, 0, j))

    def g_spec(k):
        return pl.BlockSpec((ROW_TILE, COL_TILE), lambda j, i: (i, j + k * nj))

    return pl.pallas_call(
        _merge_kernel,
        grid=(nj, t // ROW_TILE),
        in_specs=[y_spec, y_spec, y_spec, w_spec, w_spec, w_spec, g_spec(0), g_spec(1), g_spec(2)],
        out_specs=pl.BlockSpec((ROW_TILE, COL_TILE), lambda j, i: (i, j)),
        out_shape=jax.ShapeDtypeStruct((t, d), BF16),
        scratch_shapes=[pltpu.VMEM((w, COL_TILE), BF16)] * 3,
        compiler_params=_params("parallel", "arbitrary"),
        name="merge",
    )(ya, yb, yc, wa, wb, wc, gates, gates, gates)


def _lru_kernel(pa_ref, pg_ref, cw_ref, wa_ref, ba_ref, wx_ref, bx_ref, lam_ref, h0_ref,
                ya_ref, hT_ref, work, a_sc, b_sc, *, ts, chunk):
    xs = work.at[0]
    pad = SUBLANES
    win = chunk + 2 * pad
    n_chunks = ts // chunk
    zeros = jnp.zeros((pad, LANES), F32)
    xs[pl.ds(0, pad), :] = zeros
    xs[pl.ds(pad + ts, pad), :] = zeros

    def copy_in(c, carry):
        t0 = pl.multiple_of(c * chunk, chunk)
        xs[pl.ds(pad + t0, chunk), :] = pa_ref[pl.ds(t0, chunk), :]
        return carry

    lax.fori_loop(0, n_chunks, copy_in, 0)

    sp = [jax.nn.softplus(-lam_ref[d:d + 1, :]) for d in range(2)]

    def gates(c, carry):
        t0 = pl.multiple_of(c * chunk, chunk)
        xw = xs[pl.ds(t0, win), :]
        xa = None
        for k in range(CONV_A):
            sh = pltpu.roll(xw, (win + 1 - k) % win, 0) if k != 1 else xw
            term = sh[pad:pad + chunk, :] * cw_ref[k:k + 1, :]
            xa = term if xa is None else xa + term
        xb = xa.astype(BF16)
        for d in range(2):
            r = jax.nn.sigmoid(_dot(xb, wa_ref[d, 0]) + ba_ref[d:d + 1, :])
            i = jax.nn.sigmoid(_dot(xb, wx_ref[d, 0]) + bx_ref[d:d + 1, :])
            log_a = -LRU_C * r * sp[d]
            a = jnp.exp(log_a)
            b = jnp.sqrt(1.0 - a * a) * (i * xa)
            a_sc[d, pl.ds(t0, chunk), :] = a
            b_sc[d, pl.ds(t0, chunk), :] = b
        return carry

    lax.fori_loop(0, n_chunks, gates, 0)

    row = lax.broadcasted_iota(jnp.int32, (SUBLANES, LANES), 0)
    steps = (1, 2, 4)

    def scan(j, carry):
        cf, cb = carry
        tf = pl.multiple_of(j * SUBLANES, SUBLANES)
        tb = pl.multiple_of(ts - (j + 1) * SUBLANES, SUBLANES)
        af = a_sc[0, pl.ds(tf, SUBLANES), :]
        bf = b_sc[0, pl.ds(tf, SUBLANES), :]
        ab = a_sc[1, pl.ds(tb, SUBLANES), :]
        bb = b_sc[1, pl.ds(tb, SUBLANES), :]
        for s in steps:
            mf = row >= s
            bf = bf + af * jnp.where(mf, pltpu.roll(bf, s, 0), 0.0)
            af = af * jnp.where(mf, pltpu.roll(af, s, 0), 1.0)
            mb = row < SUBLANES - s
            bb = bb + ab * jnp.where(mb, pltpu.roll(bb, SUBLANES - s, 0), 0.0)
            ab = ab * jnp.where(mb, pltpu.roll(ab, SUBLANES - s, 0), 1.0)
        hf = bf + af * cf
        hb = bb + ab * cb
        work[0, pl.ds(tf, SUBLANES), :] = hf
        work[1, pl.ds(tb, SUBLANES), :] = hb
        cf = jnp.broadcast_to(hf[SUBLANES - 1:SUBLANES, :], (SUBLANES, LANES))
        cb = jnp.broadcast_to(hb[0:1, :], (SUBLANES, LANES))
        return cf, cb

    c0 = (jnp.broadcast_to(h0_ref[0:1, :], (SUBLANES, LANES)),
          jnp.broadcast_to(h0_ref[1:2, :], (SUBLANES, LANES)))
    cf, cb = lax.fori_loop(0, ts // SUBLANES, scan, c0, unroll=4)
    hT_ref[0:1, :] = cf[0:1, :]
    hT_ref[1:2, :] = cb[0:1, :]

    def finish(c, carry):
        t0 = pl.multiple_of(c * chunk, chunk)
        h = work[0, pl.ds(t0, chunk), :] + work[1, pl.ds(t0, chunk), :]
        g = jax.nn.gelu(pg_ref[pl.ds(t0, chunk), :], approximate=True)
        ya_ref[pl.ds(t0, chunk), :] = (h * g).astype(ya_ref.dtype)
        return carry

    lax.fori_loop(0, n_chunks, finish, 0)


def _lru(p_ax, p_ag, row0, ts, col_ax, col_ag, cw, wa, ba, wx, bx, lam, h0):
    w = cw.shape[-1]
    nblk = w // LANES
    rb = row0 // ts
    cax = col_ax // LANES
    cag = col_ag // LANES
    chunk = min(EW_ROWS, ts)
    kern = functools.partial(_lru_kernel, ts=ts, chunk=chunk)
    vec = pl.BlockSpec((2, LANES), lambda j: (0, j))
    mat = pl.BlockSpec((2, 1, LANES, LANES), lambda j: (0, j, 0, 0))
    return pl.pallas_call(
        kern,
        grid=(nblk,),
        in_specs=[
            pl.BlockSpec((ts, LANES), lambda j: (rb, cax + j)),
            pl.BlockSpec((ts, LANES), lambda j: (rb, cag + j)),
            pl.BlockSpec((CONV_A, LANES), lambda j: (0, j)),
            mat, vec, mat, vec, vec, vec,
        ],
        out_specs=[pl.BlockSpec((ts, LANES), lambda j: (0, j)),
                   pl.BlockSpec((2, LANES), lambda j: (0, j))],
        out_shape=[jax.ShapeDtypeStruct((ts, w), BF16),
                   jax.ShapeDtypeStruct((2, w), F32)],
        scratch_shapes=[pltpu.VMEM((2, ts + 2 * SUBLANES, LANES), F32),
                        pltpu.VMEM((2, ts, LANES), F32),
                        pltpu.VMEM((2, ts, LANES), F32)],
        compiler_params=_params("parallel"),
        name="rglru",
    )(p_ax, p_ag, cw, wa, ba, wx, bx, lam, h0)


def _conv3_kernel(x_ref, xp_ref, xn_ref, w_ref, o_ref):
    i = pl.program_id(0)
    first = i == 0
    last = i == pl.num_programs(0) - 1
    x = x_ref[...]
    r = x.shape[0]
    row = lax.broadcasted_iota(jnp.int32, (r, 1), 0)
    prev_row = jnp.where(first, 0.0, xp_ref[SUBLANES - 1:SUBLANES, :])
    next_row = jnp.where(last, 0.0, xn_ref[0:1, :])
    xm1 = jnp.where(row == 0, prev_row, pltpu.roll(x, 1, 0))
    xp1 = jnp.where(row == r - 1, next_row, pltpu.roll(x, r - 1, 0))
    o_ref[0] = w_ref[0:1, :] * xm1 + w_ref[1:2, :] * x + w_ref[2:3, :] * xp1


def _conv3(p_b, row0, ts, col_b, w3, wmix):
    r = min(EW_ROWS, ts)
    rb = row0 // r
    hb = r // SUBLANES
    cb = col_b // wmix
    n_r = ts // r
    return pl.pallas_call(
        _conv3_kernel,
        grid=(n_r, 3),
        in_specs=[
            pl.BlockSpec((r, wmix), lambda i, j: (rb + i, cb + j)),
            pl.BlockSpec((SUBLANES, wmix), lambda i, j: (jnp.maximum((rb + i) * hb - 1, 0), cb + j)),
            pl.BlockSpec((SUBLANES, wmix),
                         lambda i, j: (jnp.minimum((rb + i + 1) * hb, (rb + n_r) * hb - 1), cb + j)),
            pl.BlockSpec((CONV_B, wmix), lambda i, j: (0, j)),
        ],
        out_specs=pl.BlockSpec((1, r, wmix), lambda i, j: (j, i, 0)),
        out_shape=jax.ShapeDtypeStruct((3, ts, wmix), F32),
        compiler_params=_params("arbitrary", "arbitrary"),
        name="hyena_conv3",
    )(p_b, p_b, p_b, w3)


def _filter_kernel(z_ref, tl_ref, w1_ref, b1_ref, fr_ref, w2_ref, b2_ref, w3_ref, w3b_ref, ad_ref,
                   k_ref, ssq_ref, *, half_tiles):
    i = pl.program_id(0)
    hi = lax.Precision.HIGHEST
    fr = fr_ref[...]
    h = jnp.sin(fr * (jnp.dot(z_ref[...], w1_ref[...], precision=hi, preferred_element_type=F32)
                      + b1_ref[...]))
    h = jnp.sin(fr * (jnp.dot(h, w2_ref[...], precision=hi, preferred_element_type=F32) + b2_ref[...]))
    decay = jnp.exp(-tl_ref[...] * ad_ref[...])
    taps = _dot(h, w3_ref[0]) * decay
    r = taps.shape[0]
    row = lax.broadcasted_iota(jnp.int32, (r, 1), 0)
    k_ref[...] = taps

    @pl.when(i == 0)
    def _():
        back = _dot(h, w3b_ref[0]) * decay
        k_ref[...] = taps + jnp.where(row == 0, back, 0.0)
        ssq_ref[...] = jnp.zeros_like(ssq_ref)

    @pl.when(i == half_tiles)
    def _():
        k_ref[...] = jnp.where(row == 0, 0.0, taps)

    kk = k_ref[...]
    ssq_ref[...] += jnp.sum(kk * kk, axis=0, keepdims=True)


def _hyena_filter(seq, p):
    wmix = p["hy_skip"].shape[-1]
    hidden = p["hy_fw1"].shape[-1]
    r = min(EW_ROWS, seq)
    t_idx = jnp.arange(seq, dtype=F32)
    t_lin = t_idx / max(seq - 1, 1)
    bands = jnp.linspace(1e-4, HYENA_BANDS - 1, HYENA_BANDS, dtype=F32)
    ang = (2.0 * math.pi / seq) * t_idx[:, None] * bands[None, :]
    z = jnp.concatenate([t_lin[:, None], jnp.cos(ang), -jnp.sin(ang)], axis=-1)
    rev = lambda a: jnp.concatenate([a[:1], jnp.flip(a[1:], axis=0)], axis=0)
    zz = jnp.concatenate([z, rev(z)], axis=0)
    zz = jnp.pad(zz, ((0, 0), (0, LANES - HYENA_EMB)))
    tl = jnp.concatenate([t_lin, rev(t_lin)])[:, None]
    w1 = jnp.pad(p["hy_fw1"], ((0, LANES - HYENA_EMB), (0, 0)))
    w3 = p["hy_fw3"].reshape(hidden, HYENA_ORDER, 2, wmix).transpose(2, 0, 1, 3)
    w3 = w3.reshape(2, hidden, HYENA_ORDER * wmix)
    deltas = jnp.linspace(math.log(HYENA_TARGET) / HYENA_SLOW_DECAY,
                          math.log(HYENA_TARGET) / HYENA_FAST_DECAY, wmix, dtype=F32)
    ad = jnp.tile(jnp.abs(deltas), HYENA_ORDER)[None, :]
    ow = HYENA_ORDER * wmix
    half = seq // r
    full = lambda shape: pl.BlockSpec(shape, lambda i: tuple(0 for _ in shape))
    return pl.pallas_call(
        functools.partial(_filter_kernel, half_tiles=half),
        grid=(2 * half,),
        in_specs=[
            pl.BlockSpec((r, LANES), lambda i: (i, 0)),
            pl.BlockSpec((r, 1), lambda i: (i, 0)),
            full((LANES, hidden)), full((1, hidden)), full((1, hidden)),
            full((hidden, hidden)), full((1, hidden)),
            pl.BlockSpec((1, hidden, ow), lambda i: (jnp.where(i >= half, 1, 0), 0, 0)),
            pl.BlockSpec((1, hidden, ow), lambda i: (1, 0, 0)),
            full((1, ow)),
        ],
        out_specs=[pl.BlockSpec((r, ow), lambda i: (i, 0)),
                   pl.BlockSpec((1, ow), lambda i: (0, 0))],
        out_shape=[jax.ShapeDtypeStruct((2 * seq, ow), F32),
                   jax.ShapeDtypeStruct((1, ow), F32)],
        compiler_params=_params("arbitrary"),
        name="hyena_filter",
    )(zz, tl, w1, p["hy_fb1"][None, :], p["hy_freq"][None, :], p["hy_fw2"], p["hy_fb2"][None, :],
      w3, w3, ad)


def _kron_fwd_kernel(f_ref, x_ref, o_ref):
    f = f_ref[...]
    halves = []
    for h in range(2):
        xh = x_ref[0, :, h * SUBLANES:(h + 1) * SUBLANES, :]
        xh = xh.reshape(xh.shape[0] * SUBLANES, xh.shape[2]).astype(BF16)
        r = jnp.dot(f, xh, preferred_element_type=F32)
        halves.append(r.reshape(r.shape[0] // SUBLANES, SUBLANES, r.shape[1]))
    o_ref[...] = jnp.concatenate(halves, axis=1).astype(o_ref.dtype)


def _kron_fwd(fk, x4, sel, name):
    _, nt1, nt2, w = x4.shape
    rows = fk.shape[0] // SUBLANES
    tw = 512 if w % 512 == 0 else LANES
    rt = 2 * SUBLANES
    return pl.pallas_call(
        _kron_fwd_kernel,
        grid=(nt2 // rt, w // tw),
        in_specs=[pl.BlockSpec(fk.shape, lambda i, j: (0, 0)),
                  pl.BlockSpec((1, nt1, rt, tw), lambda i, j: (sel, 0, i, j))],
        out_specs=pl.BlockSpec((rows, rt, tw), lambda i, j: (0, i, j)),
        out_shape=jax.ShapeDtypeStruct((rows, nt2, w), BF16),
        compiler_params=_params("parallel", "parallel"),
        name=name,
    )(fk, x4)


def _kron_inv_gate_kernel(g_ref, b_ref, x_ref, v_ref, sk_ref, o_ref):
    g = g_ref[...]
    b = b_ref[...].astype(F32)
    sk = sk_ref[...]
    halves = []
    for h in range(2):
        lo, hi = h * SUBLANES, (h + 1) * SUBLANES
        bh = b[:, lo:hi, :]
        bh = bh.reshape(bh.shape[0] * SUBLANES, bh.shape[2]).astype(BF16)
        y = jnp.dot(g, bh, preferred_element_type=F32)
        y = y.reshape(y.shape[0] // SUBLANES, SUBLANES, y.shape[1])
        halves.append(x_ref[0, :, lo:hi, :] * (y + sk * v_ref[0, :, lo:hi, :]))
    o_ref[0] = jnp.concatenate(halves, axis=1).astype(o_ref.dtype)


def _kron_inv_gate(gk, b3, x4, x_sel, v4, v_sel, sk, out_dtype, name):
    _, nt2, w = b3.shape
    nt1 = gk.shape[0] // SUBLANES
    tw = 512 if w % 512 == 0 else LANES
    rt = 2 * SUBLANES
    return pl.pallas_call(
        _kron_inv_gate_kernel,
        grid=(nt2 // rt, w // tw),
        in_specs=[pl.BlockSpec(gk.shape, lambda i, j: (0, 0)),
                  pl.BlockSpec((b3.shape[0], rt, tw), lambda i, j: (0, i, j)),
                  pl.BlockSpec((1, nt1, rt, tw), lambda i, j: (x_sel, 0, i, j)),
                  pl.BlockSpec((1, nt1, rt, tw), lambda i, j: (v_sel, 0, i, j)),
                  pl.BlockSpec((1, 1, tw), lambda i, j: (0, 0, j))],
        out_specs=pl.BlockSpec((1, nt1, rt, tw), lambda i, j: (0, 0, i, j)),
        out_shape=jax.ShapeDtypeStruct((1, nt1, nt2, w), out_dtype),
        compiler_params=_params("parallel", "parallel"),
        name=name,
    )(gk, b3, x4, v4, sk.reshape(1, 1, w))


def _bmm_scale_kernel(m_ref, a_ref, s_ref, o_ref, *, bpb):
    rows_in = a_ref.shape[0] // bpb
    rows_out = o_ref.shape[0] // bpb
    for b in range(bpb):
        y = _dot(m_ref[b], a_ref[b * rows_in:(b + 1) * rows_in, :]) * s_ref[...]
        o_ref[b * rows_out:(b + 1) * rows_out, :] = y.astype(o_ref.dtype)


def _bmm_scale(m2, a, scale, rows_in, name):
    nb, rows_out, _ = m2.shape
    n = a.shape[1]
    tn = n // 2 if (n // 2) % LANES == 0 else n
    bpb = 4 if nb % 4 == 0 else 1
    return pl.pallas_call(
        functools.partial(_bmm_scale_kernel, bpb=bpb),
        grid=(nb // bpb, n // tn),
        in_specs=[pl.BlockSpec((bpb, rows_out, rows_in), lambda b, j: (b, 0, 0)),
                  pl.BlockSpec((bpb * rows_in, tn), lambda b, j: (b, j)),
                  pl.BlockSpec((1, tn), lambda b, j: (0, j))],
        out_specs=pl.BlockSpec((bpb * rows_out, tn), lambda b, j: (b, j)),
        out_shape=jax.ShapeDtypeStruct((nb * rows_out, n), BF16),
        compiler_params=_params("parallel", "arbitrary"),
        name=name,
    )(m2, a, scale)


def _spectral_one(m2, a, k, m3):
    x = _dot(m2, a)
    f = x.shape[0] // 2
    xr, xi = x[:f], x[f:]
    kr, ki = k[:f].astype(F32), k[f:].astype(F32)
    y = jnp.concatenate([xr * kr - xi * ki, xr * ki + xi * kr], axis=0)
    return _dot(m3, y)


def _spectral_kernel(m2_ref, a_ref, k_ref, m3_ref, o_ref, *, bpb):
    rows_in = a_ref.shape[0] // bpb
    f2 = k_ref.shape[0] // bpb
    rows_out = o_ref.shape[0] // bpb
    for b in range(bpb):
        y = _spectral_one(m2_ref[b], a_ref[b * rows_in:(b + 1) * rows_in, :],
                          k_ref[b * f2:(b + 1) * f2, :], m3_ref[b])
        o_ref[b * rows_out:(b + 1) * rows_out, :] = y.astype(o_ref.dtype)


def _spectral_gate_kernel(m2_ref, a_ref, k_ref, m3_ref, x_ref, v_ref, sk_ref, o_ref):
    y = _spectral_one(m2_ref[0], a_ref[...], k_ref[...], m3_ref[0])
    o_ref[...] = (x_ref[...] * (y + sk_ref[...] * v_ref[...])).astype(o_ref.dtype)


def _spectral(m2, a, kspec, order, m3, out_dtype, gate=None, name="hyena_spectral"):
    nb, f2, rows_in = m2.shape
    rows_out = m3.shape[1]
    wmix = a.shape[1]
    bpb = 4 if (nb % 4 == 0 and gate is None) else 1
    in_specs = [pl.BlockSpec((bpb, f2, rows_in), lambda b: (b, 0, 0)),
                pl.BlockSpec((bpb * rows_in, wmix), lambda b: (b, 0)),
                pl.BlockSpec((bpb * f2, wmix), lambda b: (b, order)),
                pl.BlockSpec((bpb, rows_out, f2), lambda b: (b, 0, 0))]
    args = [m2, a, kspec, m3]
    kern = functools.partial(_spectral_kernel, bpb=bpb)
    if gate is not None:
        x, v, sk = gate
        blk = pl.BlockSpec((rows_out, wmix), lambda b: (b, 0))
        in_specs += [blk, blk, pl.BlockSpec((1, wmix), lambda b: (0, 0))]
        args += [x, v, sk]
        kern = _spectral_gate_kernel
    return pl.pallas_call(
        kern,
        grid=(nb // bpb,),
        in_specs=in_specs,
        out_specs=pl.BlockSpec((bpb * rows_out, wmix), lambda b: (b, 0)),
        out_shape=jax.ShapeDtypeStruct((nb * rows_out, wmix), out_dtype),
        compiler_params=_params("parallel"),
        name=name,
    )(*args)


def _dft_tables(seq):
    n = 2 * seq
    n2 = DFT_INNER
    n1 = n // n2
    i1 = jnp.arange(n1, dtype=jnp.int32)
    i2 = jnp.arange(n2, dtype=jnp.int32)
    ang1 = (2.0 * math.pi / n1) * ((i1[:, None] * i1[None, :]) % n1).astype(F32)
    c1, s1 = jnp.cos(ang1), jnp.sin(ang1)
    f1 = jnp.stack([c1, -s1], axis=1).reshape(2 * n1, n1)
    q = i1[:, None, None] + n1 * i2[None, :, None]
    ang = (2.0 * math.pi / n) * ((q * i2[None, None, :]) % n).astype(F32)
    tr, ti = jnp.cos(ang), -jnp.sin(ang)
    m2 = jnp.concatenate([jnp.concatenate([tr, -ti], axis=2),
                          jnp.concatenate([ti, tr], axis=2)], axis=1)
    trt, tit = jnp.swapaxes(tr, 1, 2), jnp.swapaxes(ti, 1, 2)
    m3 = jnp.concatenate([jnp.concatenate([trt, tit], axis=2),
                          jnp.concatenate([-tit, trt], axis=2)], axis=1)
    g = jnp.stack([c1, -s1], axis=2).reshape(n1, 2 * n1)[: n1 // 2] / n
    eye = jnp.eye(SUBLANES, dtype=F32)
    kron = lambda m: jnp.kron(m, eye).astype(BF16)
    return kron(f1[:, : n1 // 2]), kron(f1), m2.astype(BF16), m3.astype(BF16), kron(g)


def _direct_dft_tables(seq):
    n = 2 * seq
    i = jnp.arange(n, dtype=jnp.int32)
    ang = (2.0 * math.pi / n) * ((i[:, None] * i[None, :]) % n).astype(F32)
    c, s = jnp.cos(ang), jnp.sin(ang)
    fwd = jnp.concatenate([c, -s], axis=0)
    inv = jnp.concatenate([c[:seq], -s[:seq]], axis=1) / n
    return fwd[:, :seq].astype(BF16)[None], fwd.astype(BF16)[None], inv.astype(BF16)[None]


def _hyena_long(u3, taps, ssq, skip, tables):
    fk_half, fk_full, m2, m3, gk = tables
    _, seq, wmix = u3.shape
    n2 = DFT_INNER
    n1 = 2 * seq // n2
    ow = taps.shape[1]
    scale = lax.rsqrt(ssq + EPS)
    ak = _kron_fwd(fk_full, taps.reshape(1, n1, n2, ow), 0, "hyena_filter_dft1")
    kspec = _bmm_scale(m2, ak.reshape(n1 * 2 * n2, ow), scale, 2 * n2, "hyena_filter_dft2")
    u4 = u3.reshape(3, n1 // 2, n2, wmix)
    z4, z_sel = u4, 0
    for o in range(HYENA_ORDER):
        a = _kron_fwd(fk_half, z4, z_sel, "hyena_dft1")
        b = _spectral(m2, a.reshape(n1 * 2 * n2, wmix), kspec, o, m3, BF16)
        last = o == HYENA_ORDER - 1
        z4 = _kron_inv_gate(gk, b.reshape(2 * n1, n2, wmix), u4, 1 + o, z4, z_sel, skip[o],
                            BF16 if last else F32, "hyena_dft4_gate")
        z_sel = 0
    return z4.reshape(seq, wmix)


def _hyena_short(u3, taps, ssq, skip, tables):
    fwd_half, fwd_full, inv = tables
    scale = lax.rsqrt(ssq + EPS)
    kspec = _bmm_scale(fwd_full, taps, scale, taps.shape[0], "hyena_ctx_filter_dft")
    z = u3[0]
    for o in range(HYENA_ORDER):
        last = o == HYENA_ORDER - 1
        z = _spectral(fwd_half, z, kspec, o, inv, BF16 if last else F32,
                      gate=(u3[1 + o], z, skip[o][None, :]), name="hyena_ctx_spectral")
    return z


def _attn_kernel(q_ref, k_ref, v_ref, o_ref, m_sc, acc_sc, s_sc, p_sc, a_sc, *, rows):
    j = pl.program_id(2)
    _, tq, tk = s_sc.shape
    nlb = tk // LANES

    @pl.when(j == 0)
    def _():
        m_sc[...] = jnp.full_like(m_sc, -jnp.inf)
        acc_sc[...] = jnp.zeros_like(acc_sc)

    k = k_ref[...]
    v = v_ref[...]
    v1 = jnp.concatenate([v, jnp.ones_like(v)], axis=1)
    for g in range(GQA_GROUP):
        q = q_ref[:, g * HEAD_DIM:(g + 1) * HEAD_DIM]
        s_sc[g] = lax.dot_general(q, k, (((1,), (1,)), ((), ())), preferred_element_type=F32)

    for g in range(GQA_GROUP):
        for c in range(tq // rows):
            rs = slice(c * rows, (c + 1) * rows)
            blocks = [s_sc[g, rs, b * LANES:(b + 1) * LANES] for b in range(nlb)]
            bmax = blocks[0]
            for blk in blocks[1:]:
                bmax = jnp.maximum(bmax, blk)
            m_prev = m_sc[g, rs, :]
            m_new = jnp.maximum(m_prev, jnp.max(bmax, axis=1, keepdims=True))
            for b, blk in enumerate(blocks):
                p_sc[g, rs, b * LANES:(b + 1) * LANES] = jnp.exp2(blk - m_new).astype(BF16)
            m_sc[g, rs, :] = m_new
            a_sc[g, rs, :] = jnp.exp2(m_prev - m_new)

    for g in range(GQA_GROUP):
        alpha = jnp.concatenate([a_sc[g], a_sc[g]], axis=1)
        acc_sc[g] = alpha * acc_sc[g] + jnp.dot(p_sc[g], v1, preferred_element_type=F32)

    @pl.when(j == pl.num_programs(2) - 1)
    def _():
        for g in range(GQA_GROUP):
            acc = acc_sc[g]
            o_ref[:, g * HEAD_DIM:(g + 1) * HEAD_DIM] = (
                acc[:, :HEAD_DIM] / acc[:, HEAD_DIM:]).astype(o_ref.dtype)


def _attention(q, k, v, q_row0, n_q, k_row0, n_k, tq, tk):
    n_kv = k.shape[1] // HEAD_DIM
    gw = GQA_GROUP * HEAD_DIM
    qb, kb = q_row0 // tq, k_row0 // tk
    return pl.pallas_call(
        functools.partial(_attn_kernel, rows=2 * SUBLANES),
        grid=(n_kv, n_q // tq, n_k // tk),
        in_specs=[pl.BlockSpec((tq, gw), lambda h, i, j: (qb + i, h)),
                  pl.BlockSpec((tk, HEAD_DIM), lambda h, i, j: (kb + j, h)),
                  pl.BlockSpec((tk, HEAD_DIM), lambda h, i, j: (kb + j, h))],
        out_specs=pl.BlockSpec((tq, gw), lambda h, i, j: (i, h)),
        out_shape=jax.ShapeDtypeStruct((n_q, q.shape[1]), BF16),
        scratch_shapes=[pltpu.VMEM((GQA_GROUP, tq, LANES), F32),
                        pltpu.VMEM((GQA_GROUP, tq, 2 * HEAD_DIM), F32),
                        pltpu.VMEM((GQA_GROUP, tq, tk), F32),
                        pltpu.VMEM((GQA_GROUP, tq, tk), BF16),
                        pltpu.VMEM((GQA_GROUP, tq, LANES), F32)],
        compiler_params=_params("parallel", "parallel", "arbitrary"),
        name="attention",
    )(q, k, v)


def _rope_tables(seq, n_ctx):
    rows = seq // GRID_W
    row = jnp.repeat(jnp.arange(rows, dtype=F32), GRID_W)
    col = jnp.tile(jnp.arange(GRID_W, dtype=F32), rows)
    n_pairs = HEAD_DIM // 4
    inv = ROPE_THETA ** (-jnp.arange(n_pairs, dtype=F32) / n_pairs)
    ang = jnp.concatenate([row[:, None] * inv, col[:, None] * inv], axis=-1)
    ang = jnp.concatenate([ang, jnp.zeros((n_ctx, HEAD_DIM // 2), F32)], axis=0)
    c = jnp.repeat(jnp.cos(ang), 2, axis=1)
    s = jnp.repeat(jnp.sin(ang), 2, axis=1)
    even = (jnp.arange(HEAD_DIM) % 2 == 0)[None, :]
    return c, jnp.where(even, -s, 0.0), jnp.where(even, 0.0, s)


def kernel(x, c, ctx, c_ctx, w_mod_down, w_mod_up, b_mod, norm_ffn1, norm_mix, norm_ffn2,
           ffn1_w_in, ffn1_w_out, ffn2_w_in, ffn2_w_out, w_in, lru_conv, lru_w_a, lru_b_a,
           lru_w_x, lru_b_x, lru_lambda, hy_conv, hy_fw1, hy_fb1, hy_freq, hy_fw2, hy_fb2,
           hy_fw3, hy_skip, q_norm, k_norm, w_branch_a, w_branch_b, w_branch_c, w_out, final_norm):
    bsz, seq, d = x.shape
    assert bsz == 1 and c.shape[0] == 1 and ctx.shape[0] == 1
    n_ctx = ctx.shape[1]
    depth = w_in.shape[0]
    wmix = lru_conv.shape[-1]
    kvw = wmix // GQA_GROUP
    t_all = seq + n_ctx
    assert t_all % ROW_TILE == 0 and seq % EW_ROWS == 0 and n_ctx % EW_ROWS == 0

    col_ax = 0
    col_ck = col_ax + wmix
    col_cv = col_ck + kvw
    col_ag = col_cv + kvw
    col_b = col_ag + wmix
    col_cq = col_b + 3 * wmix
    col_g = col_cq + wmix

    xs = jnp.concatenate([x[0], ctx[0]], axis=0)
    cc = jnp.zeros((SUBLANES, d), F32).at[0].set(c[0]).at[1].set(c_ctx)
    mods_all = _modulation(cc, w_mod_down, w_mod_up, b_mod)

    ffn1_in, ffn1_out, ffn2_in, ffn2_out = ffn1_w_in, ffn1_w_out, ffn2_w_in, ffn2_w_out
    w_in_b, wba, wbb, wbc, w_out_b = w_in, w_branch_a, w_branch_b, w_branch_c, w_out

    rope = _rope_tables(seq, n_ctx)
    dft_lat = _dft_tables(seq)
    dft_ctx = _direct_dft_tables(n_ctx)
    q_scale = HEAD_DIM ** -0.5 * math.log2(math.e)
    tq_lat = next(t for t in (1024, 512, EW_ROWS) if seq % t == 0)
    tk_lat = next(t for t in (1408, ROW_TILE) if t_all % t == 0)

    for i in range(depth):
        ctx_out = i < depth - 1
        mods = mods_all[i]
        mods3 = mods.reshape(2 * N_MOD, 1, d)

        u = _norm_mod(xs, norm_ffn1[i], mods3, 0, seq)
        h = _ffn_up(u, ffn1_in, i)
        xs = _down(h, ffn1_out, i, xs, mods[:, 2], 0.5, seq)

        u = _norm_mod(xs, norm_mix[i], mods3, 3, seq)
        p_ax = _proj(u, w_in_b, i, col_ax, wmix, False, F32, "mixer_in_lru")
        p_b = _proj(u, w_in_b, i, col_ag, 4 * wmix, False, F32, "mixer_in_gelu_hyena")
        kh = _proj_heads(u, w_in_b, i, col_ck, kvw, k_norm[i], rope, 1.0, "mixer_in_k")
        vh = _proj(u, w_in_b, i, col_cv, kvw, False, BF16, "mixer_in_v")
        qh = _proj_heads(u, w_in_b, i, col_cq, wmix, q_norm[i], rope, q_scale, "mixer_in_q")
        gates = _proj(u, w_in_b, i, col_g, 3 * d, True, BF16, "mixer_gates")

        lru_args = (lru_conv[i], lru_w_a[i], lru_b_a[i], lru_w_x[i], lru_b_x[i], lru_lambda[i])
        ya_c, h_c = _lru(p_ax, p_b, seq, n_ctx, 0, 0, *lru_args, jnp.zeros((2, wmix), F32))
        ya_l, _ = _lru(p_ax, p_b, 0, seq, 0, 0, *lru_args, h_c)

        hp = {"hy_fw1": hy_fw1[i], "hy_fb1": hy_fb1[i], "hy_freq": hy_freq[i], "hy_fw2": hy_fw2[i],
              "hy_fb2": hy_fb2[i], "hy_fw3": hy_fw3[i], "hy_skip": hy_skip[i]}
        taps_l, ssq_l = _hyena_filter(seq, hp)
        u3_l = _conv3(p_b, 0, seq, wmix, hy_conv[i], wmix)
        yb_l = _hyena_long(u3_l, taps_l, ssq_l, hy_skip[i], dft_lat)

        yc_l = _attention(qh, kh, vh, 0, seq, 0, t_all, tq_lat, tk_lat)

        if ctx_out:
            taps_c, ssq_c = _hyena_filter(n_ctx, hp)
            u3_c = _conv3(p_b, seq, n_ctx, wmix, hy_conv[i], wmix)
            yb_c = _hyena_short(u3_c, taps_c, ssq_c, hy_skip[i], dft_ctx)
            yc_c = _attention(qh, kh, vh, seq, n_ctx, seq, n_ctx, n_ctx, n_ctx)
        else:
            yb_c = jnp.zeros((n_ctx, wmix), BF16)
            yc_c = jnp.zeros((n_ctx, wmix), BF16)

        ya = jnp.concatenate([ya_l, ya_c], axis=0)
        yb = jnp.concatenate([yb_l, yb_c], axis=0)
        yc = jnp.concatenate([yc_l, yc_c], axis=0)
        m = _merge(ya, yb, yc, wba, wbb, wbc, i, gates)
        xs = _down(m, w_out_b, i, xs, mods[:, 5], 1.0, seq)

        u = _norm_mod(xs, norm_ffn2[i], mods3, 6, seq)
        h = _ffn_up(u, ffn2_in, i)
        xs = _down(h, ffn2_out, i, xs, mods[:, 8], 0.5, seq)

    return _final_norm(xs, final_norm, seq)[None]
```

```python
import functools
import math

import jax
import jax.numpy as jnp
from jax import lax
from jax.experimental import pallas as pl
from jax.experimental.pallas import tpu as pltpu

F32 = jnp.float32
BF16 = jnp.bfloat16

HEAD_DIM = 128
LANES = 128
SUBLANES = 8
GQA_GROUP = 3
GRID_W = 64
ROPE_THETA = 10000.0
LRU_C = 8.0
CONV_A = 4
CONV_B = 3
HYENA_ORDER = 2
HYENA_BANDS = 16
HYENA_EMB = 2 * HYENA_BANDS + 1
HYENA_FAST_DECAY = 0.3
HYENA_SLOW_DECAY = 1.5
HYENA_TARGET = 1e-2
N_MOD = 9
EPS = 1e-6
DFT_INNER = 128
VMEM_LIMIT = 56 * 1024 * 1024

ROW_TILE = 768
COL_TILE = 512
EW_ROWS = 256


def _params(*sem):
    return pltpu.CompilerParams(dimension_semantics=sem, vmem_limit_bytes=VMEM_LIMIT)


def _dot(a, b):
    return jnp.dot(a.astype(BF16), b.astype(BF16), preferred_element_type=F32)


def _mod_kernel(c_ref, wd_ref, wu_ref, b_ref, o_ref):
    c = c_ref[...]
    s = c * jax.nn.sigmoid(c)
    t = _dot(s, wd_ref[0])
    o_ref[0] = _dot(t, wu_ref[0]) + b_ref[0]


def _modulation(cc, w_down, w_up, b_mod):
    depth, d, rank = w_down.shape
    out = pl.pallas_call(
        _mod_kernel,
        grid=(depth, N_MOD),
        in_specs=[
            pl.BlockSpec((SUBLANES, d), lambda l, j: (0, 0)),
            pl.BlockSpec((1, d, rank), lambda l, j: (l, 0, 0)),
            pl.BlockSpec((1, rank, d), lambda l, j: (l, 0, j)),
            pl.BlockSpec((1, 1, d), lambda l, j: (l, 0, j)),
        ],
        out_specs=pl.BlockSpec((1, SUBLANES, d), lambda l, j: (l, 0, j)),
        out_shape=jax.ShapeDtypeStruct((depth, SUBLANES, N_MOD * d), F32),
        compiler_params=_params("arbitrary", "arbitrary"),
        name="modulation",
    )(cc, w_down, w_up, b_mod.reshape(depth, 1, N_MOD * d))
    return out.reshape(depth, SUBLANES, N_MOD, d)[:, :2]


def _norm_mod_kernel(x_ref, g_ref, sh_ref, sc_ref, o_ref):
    x = x_ref[...]
    y = x * lax.rsqrt(jnp.mean(x * x, axis=-1, keepdims=True) + EPS)
    y = y * g_ref[...]
    o_ref[...] = (y * (1.0 + sc_ref[0]) + sh_ref[0]).astype(o_ref.dtype)


def _norm_mod(x, g, mods, idx, n_lat):
    t, d = x.shape
    nl = n_lat // EW_ROWS

    def sel(i, k):
        return (jnp.where(i >= nl, N_MOD, 0) + k, 0, 0)

    return pl.pallas_call(
        _norm_mod_kernel,
        grid=(t // EW_ROWS,),
        in_specs=[
            pl.BlockSpec((EW_ROWS, d), lambda i: (i, 0)),
            pl.BlockSpec((1, d), lambda i: (0, 0)),
            pl.BlockSpec((1, 1, d), lambda i: sel(i, idx)),
            pl.BlockSpec((1, 1, d), lambda i: sel(i, idx + 1)),
        ],
        out_specs=pl.BlockSpec((EW_ROWS, d), lambda i: (i, 0)),
        out_shape=jax.ShapeDtypeStruct((t, d), BF16),
        compiler_params=_params("parallel"),
        name="norm_mod",
    )(x, g.reshape(1, d), mods, mods)


def _final_norm_kernel(x_ref, g_ref, o_ref):
    x = x_ref[...]
    y = x * lax.rsqrt(jnp.mean(x * x, axis=-1, keepdims=True) + EPS)
    o_ref[...] = y * g_ref[...]


def _final_norm(x, g, n_lat):
    t, d = x.shape
    return pl.pallas_call(
        _final_norm_kernel,
        grid=(n_lat // EW_ROWS,),
        in_specs=[pl.BlockSpec((EW_ROWS, d), lambda i: (i, 0)),
                  pl.BlockSpec((1, d), lambda i: (0, 0))],
        out_specs=pl.BlockSpec((EW_ROWS, d), lambda i: (i, 0)),
        out_shape=jax.ShapeDtypeStruct((n_lat, d), F32),
        compiler_params=_params("parallel"),
        name="final_norm",
    )(x, g.reshape(1, d))


def _serpentine(j, i, n_i):
    return jnp.where(j % 2 == 0, i, n_i - 1 - i)


def _with_bf16_weights(w_refs, wb_refs, body):
    first = pl.program_id(1) == 0

    @pl.when(first)
    def _():
        ws = []
        for w_ref, wb in zip(w_refs, wb_refs):
            wv = w_ref[0].astype(BF16)
            wb[...] = wv
            ws.append(wv)
        body(ws)

    @pl.when(jnp.logical_not(first))
    def _():
        body([wb[...] for wb in wb_refs])


def _ffn_up_kernel(u_ref, wg_ref, wu_ref, o_ref, wgb, wub):
    def body(ws):
        u = u_ref[...]
        a = jnp.dot(u, ws[0], preferred_element_type=F32)
        b = jnp.dot(u, ws[1], preferred_element_type=F32)
        o_ref[...] = (a * jax.nn.sigmoid(a) * b).astype(o_ref.dtype)

    _with_bf16_weights((wg_ref, wu_ref), (wgb, wub), body)


def _ffn_up(u, w_gu, layer):
    t, d = u.shape
    f = w_gu.shape[-1] // 2
    tn = COL_TILE // 2
    nj = f // tn
    ni = t // ROW_TILE
    return pl.pallas_call(
        _ffn_up_kernel,
        grid=(nj, ni),
        in_specs=[
            pl.BlockSpec((ROW_TILE, d), lambda j, i: (_serpentine(j, i, ni), 0)),
            pl.BlockSpec((1, d, tn), lambda j, i: (layer, 0, j)),
            pl.BlockSpec((1, d, tn), lambda j, i: (layer, 0, j + nj)),
        ],
        out_specs=pl.BlockSpec((ROW_TILE, tn), lambda j, i: (_serpentine(j, i, ni), j)),
        out_shape=jax.ShapeDtypeStruct((t, f), BF16),
        scratch_shapes=[pltpu.VMEM((d, tn), BF16), pltpu.VMEM((d, tn), BF16)],
        compiler_params=_params("parallel", "arbitrary"),
        name="ffn_up",
    )(u, w_gu, w_gu)


def _down_kernel(h_ref, w_ref, x_ref, g_ref, o_ref, wb, *, coef, n_lat, ni):
    def body(ws):
        acc = jnp.dot(h_ref[...], ws[0], preferred_element_type=F32)
        tm = acc.shape[0]
        tile = _serpentine(pl.program_id(0), pl.program_id(1), ni)
        row = tile * tm + lax.broadcasted_iota(jnp.int32, (tm, 1), 0)
        g = jnp.where(row >= n_lat, g_ref[1:2, :], g_ref[0:1, :])
        o_ref[...] = x_ref[...] + coef * g * acc

    _with_bf16_weights((w_ref,), (wb,), body)


def _down(h, w, layer, x, gates, coef, n_lat):
    t, k = h.shape
    d = w.shape[-1]
    ni = t // ROW_TILE
    return pl.pallas_call(
        functools.partial(_down_kernel, coef=coef, n_lat=n_lat, ni=ni),
        grid=(d // COL_TILE, ni),
        in_specs=[
            pl.BlockSpec((ROW_TILE, k), lambda j, i: (_serpentine(j, i, ni), 0)),
            pl.BlockSpec((1, k, COL_TILE), lambda j, i: (layer, 0, j)),
            pl.BlockSpec((ROW_TILE, COL_TILE), lambda j, i: (_serpentine(j, i, ni), j)),
            pl.BlockSpec((2, COL_TILE), lambda j, i: (0, j)),
        ],
        out_specs=pl.BlockSpec((ROW_TILE, COL_TILE), lambda j, i: (_serpentine(j, i, ni), j)),
        out_shape=jax.ShapeDtypeStruct((t, d), F32),
        input_output_aliases={2: 0},
        scratch_shapes=[pltpu.VMEM((k, COL_TILE), BF16)],
        compiler_params=_params("parallel", "arbitrary"),
        name="down_residual",
    )(h, w, x, gates)


def _proj_kernel(u_ref, w_ref, o_ref, wb, *, sigmoid):
    def body(ws):
        acc = jnp.dot(u_ref[...], ws[0], preferred_element_type=F32)
        if sigmoid:
            acc = jax.nn.sigmoid(acc)
        o_ref[...] = acc.astype(o_ref.dtype)

    _with_bf16_weights((w_ref,), (wb,), body)


def _col_tile(col0, ncols):
    return next(t for t in (COL_TILE, 256, LANES) if col0 % t == 0 and ncols % t == 0)


def _proj(u, w, layer, col0, ncols, sigmoid, out_dtype, name):
    t, d = u.shape
    tn = _col_tile(col0, ncols)
    j0 = col0 // tn
    ni = t // ROW_TILE
    return pl.pallas_call(
        functools.partial(_proj_kernel, sigmoid=sigmoid),
        grid=(ncols // tn, ni),
        in_specs=[
            pl.BlockSpec((ROW_TILE, d), lambda j, i: (_serpentine(j, i, ni), 0)),
            pl.BlockSpec((1, d, tn), lambda j, i: (layer, 0, j + j0)),
        ],
        out_specs=pl.BlockSpec((ROW_TILE, tn), lambda j, i: (_serpentine(j, i, ni), j)),
        out_shape=jax.ShapeDtypeStruct((t, ncols), out_dtype),
        scratch_shapes=[pltpu.VMEM((d, tn), BF16)],
        compiler_params=_params("parallel", "arbitrary"),
        name=name,
    )(u, w)


def _proj_heads_kernel(u_ref, w_ref, g_ref, cc_ref, se_ref, so_ref, o_ref, wb, *, scale):
    def body(ws):
        acc = jnp.dot(u_ref[...], ws[0], preferred_element_type=F32)
        for hh in range(acc.shape[1] // HEAD_DIM):
            y = acc[:, hh * HEAD_DIM:(hh + 1) * HEAD_DIM]
            y = y * lax.rsqrt(jnp.mean(y * y, axis=-1, keepdims=True) + EPS) * g_ref[...]
            y = (y * cc_ref[...] + pltpu.roll(y, LANES - 1, 1) * se_ref[...]
                 + pltpu.roll(y, 1, 1) * so_ref[...])
            o_ref[:, hh * HEAD_DIM:(hh + 1) * HEAD_DIM] = (y * scale).astype(o_ref.dtype)

    _with_bf16_weights((w_ref,), (wb,), body)


def _proj_heads(u, w, layer, col0, ncols, gain, rope, scale, name):
    t, d = u.shape
    tn = _col_tile(col0, ncols)
    j0 = col0 // tn
    ni = t // ROW_TILE
    cc, se, so = rope
    tab = pl.BlockSpec((ROW_TILE, LANES), lambda j, i: (_serpentine(j, i, ni), 0))
    return pl.pallas_call(
        functools.partial(_proj_heads_kernel, scale=scale),
        grid=(ncols // tn, ni),
        in_specs=[
            pl.BlockSpec((ROW_TILE, d), lambda j, i: (_serpentine(j, i, ni), 0)),
            pl.BlockSpec((1, d, tn), lambda j, i: (layer, 0, j + j0)),
            pl.BlockSpec((1, LANES), lambda j, i: (0, 0)),
            tab, tab, tab,
        ],
        out_specs=pl.BlockSpec((ROW_TILE, tn), lambda j, i: (_serpentine(j, i, ni), j)),
        out_shape=jax.ShapeDtypeStruct((t, ncols), BF16),
        scratch_shapes=[pltpu.VMEM((d, tn), BF16)],
        compiler_params=_params("parallel", "arbitrary"),
        name=name,
    )(u, w, gain.reshape(1, LANES), cc, se, so)


def _merge_kernel(ya_ref, yb_ref, yc_ref, wa_ref, wb_ref, wc_ref, ga_ref, gb_ref, gc_ref, o_ref,
                  wab, wbb, wcb):
    def body(ws):
        m = ga_ref[...].astype(F32) * jnp.dot(ya_ref[...], ws[0], preferred_element_type=F32)
        m += gb_ref[...].astype(F32) * jnp.dot(yb_ref[...], ws[1], preferred_element_type=F32)
        m += gc_ref[...].astype(F32) * jnp.dot(yc_ref[...], ws[2], preferred_element_type=F32)
        o_ref[...] = m.astype(o_ref.dtype)

    _with_bf16_weights((wa_ref, wb_ref, wc_ref), (wab, wbb, wcb), body)


def _merge(ya, yb, yc, wa, wb, wc, layer, gates):
    t, w = ya.shape
    d = wa.shape[-1]
    nj = d // COL_TILE
    ni = t // ROW_TILE
    y_spec = pl.BlockSpec((ROW_TILE, w), lambda j, i: (_serpentine(j, i, ni), 0))
    w_spec = pl.BlockSpec((1, w, COL_TILE), lambda j, i: (layer, 0, j))

    def g_spec(k):
        return pl.BlockSpec((ROW_TILE, COL_TILE), lambda j, i: (_serpentine(j, i, ni), j + k * nj))

    return pl.pallas_call(
        _merge_kernel,
        grid=(nj, ni),
        in_specs=[y_spec, y_spec, y_spec, w_spec, w_spec, w_spec, g_spec(0), g_spec(1), g_spec(2)],
        out_specs=pl.BlockSpec((ROW_TILE, COL_TILE), lambda j, i: (_serpentine(j, i, ni), j)),
        out_shape=jax.ShapeDtypeStruct((t, d), BF16),
        scratch_shapes=[pltpu.VMEM((w, COL_TILE), BF16)] * 3,
        compiler_params=_params("parallel", "arbitrary"),
        name="merge",
    )(ya, yb, yc, wa, wb, wc, gates, gates, gates)


def _lru_kernel(pa_ref, pg_ref, cw_ref, wa_ref, ba_ref, wx_ref, bx_ref, lam_ref, h0_ref,
                ya_ref, hT_ref, work, a_sc, b_sc, *, ts, chunk):
    xs = work.at[0]
    pad = SUBLANES
    win = chunk + 2 * pad
    n_chunks = ts // chunk
    zeros = jnp.zeros((pad, LANES), F32)
    xs[pl.ds(0, pad), :] = zeros
    xs[pl.ds(pad + ts, pad), :] = zeros

    def copy_in(c, carry):
        t0 = pl.multiple_of(c * chunk, chunk)
        xs[pl.ds(pad + t0, chunk), :] = pa_ref[pl.ds(t0, chunk), :]
        return carry

    lax.fori_loop(0, n_chunks, copy_in, 0)

    sp = [jax.nn.softplus(-lam_ref[d:d + 1, :]) for d in range(2)]

    def gates(c, carry):
        t0 = pl.multiple_of(c * chunk, chunk)
        xw = xs[pl.ds(t0, win), :]
        xa = None
        for k in range(CONV_A):
            sh = pltpu.roll(xw, (win + 1 - k) % win, 0) if k != 1 else xw
            term = sh[pad:pad + chunk, :] * cw_ref[k:k + 1, :]
            xa = term if xa is None else xa + term
        xb = xa.astype(BF16)
        for d in range(2):
            r = jax.nn.sigmoid(_dot(xb, wa_ref[d, 0]) + ba_ref[d:d + 1, :])
            i = jax.nn.sigmoid(_dot(xb, wx_ref[d, 0]) + bx_ref[d:d + 1, :])
            log_a = -LRU_C * r * sp[d]
            a = jnp.exp(log_a)
            b = jnp.sqrt(1.0 - a * a) * (i * xa)
            a_sc[d, pl.ds(t0, chunk), :] = a
            b_sc[d, pl.ds(t0, chunk), :] = b
        return carry

    lax.fori_loop(0, n_chunks, gates, 0)

    row = lax.broadcasted_iota(jnp.int32, (SUBLANES, LANES), 0)
    steps = (1, 2, 4)

    def scan(j, carry):
        cf, cb = carry
        tf = pl.multiple_of(j * SUBLANES, SUBLANES)
        tb = pl.multiple_of(ts - (j + 1) * SUBLANES, SUBLANES)
        af = a_sc[0, pl.ds(tf, SUBLANES), :]
        bf = b_sc[0, pl.ds(tf, SUBLANES), :]
        ab = a_sc[1, pl.ds(tb, SUBLANES), :]
        bb = b_sc[1, pl.ds(tb, SUBLANES), :]
        for s in steps:
            mf = row >= s
            bf = bf + af * jnp.where(mf, pltpu.roll(bf, s, 0), 0.0)
            af = af * jnp.where(mf, pltpu.roll(af, s, 0), 1.0)
            mb = row < SUBLANES - s
            bb = bb + ab * jnp.where(mb, pltpu.roll(bb, SUBLANES - s, 0), 0.0)
            ab = ab * jnp.where(mb, pltpu.roll(ab, SUBLANES - s, 0), 1.0)
        hf = bf + af * cf
        hb = bb + ab * cb
        work[0, pl.ds(tf, SUBLANES), :] = hf
        work[1, pl.ds(tb, SUBLANES), :] = hb
        cf = jnp.broadcast_to(hf[SUBLANES - 1:SUBLANES, :], (SUBLANES, LANES))
        cb = jnp.broadcast_to(hb[0:1, :], (SUBLANES, LANES))
        return cf, cb

    c0 = (jnp.broadcast_to(h0_ref[0:1, :], (SUBLANES, LANES)),
          jnp.broadcast_to(h0_ref[1:2, :], (SUBLANES, LANES)))
    cf, cb = lax.fori_loop(0, ts // SUBLANES, scan, c0, unroll=4)
    hT_ref[0:1, :] = cf[0:1, :]
    hT_ref[1:2, :] = cb[0:1, :]

    def finish(c, carry):
        t0 = pl.multiple_of(c * chunk, chunk)
        h = work[0, pl.ds(t0, chunk), :] + work[1, pl.ds(t0, chunk), :]
        g = jax.nn.gelu(pg_ref[pl.ds(t0, chunk), :], approximate=True)
        ya_ref[pl.ds(t0, chunk), :] = (h * g).astype(ya_ref.dtype)
        return carry

    lax.fori_loop(0, n_chunks, finish, 0)


def _lru(p_ax, p_ag, row0, ts, col_ax, col_ag, cw, wa, ba, wx, bx, lam, h0):
    w = cw.shape[-1]
    nblk = w // LANES
    rb = row0 // ts
    cax = col_ax // LANES
    cag = col_ag // LANES
    chunk = min(EW_ROWS, ts)
    kern = functools.partial(_lru_kernel, ts=ts, chunk=chunk)
    vec = pl.BlockSpec((2, LANES), lambda j: (0, j))
    mat = pl.BlockSpec((2, 1, LANES, LANES), lambda j: (0, j, 0, 0))
    return pl.pallas_call(
        kern,
        grid=(nblk,),
        in_specs=[
            pl.BlockSpec((ts, LANES), lambda j: (rb, cax + j)),
            pl.BlockSpec((ts, LANES), lambda j: (rb, cag + j)),
            pl.BlockSpec((CONV_A, LANES), lambda j: (0, j)),
            mat, vec, mat, vec, vec, vec,
        ],
        out_specs=[pl.BlockSpec((ts, LANES), lambda j: (0, j)),
                   pl.BlockSpec((2, LANES), lambda j: (0, j))],
        out_shape=[jax.ShapeDtypeStruct((ts, w), BF16),
                   jax.ShapeDtypeStruct((2, w), F32)],
        scratch_shapes=[pltpu.VMEM((2, ts + 2 * SUBLANES, LANES), F32),
                        pltpu.VMEM((2, ts, LANES), F32),
                        pltpu.VMEM((2, ts, LANES), F32)],
        compiler_params=_params("parallel"),
        name="rglru",
    )(p_ax, p_ag, cw, wa, ba, wx, bx, lam, h0)


def _conv3_kernel(x_ref, xp_ref, xn_ref, w_ref, o_ref):
    i = pl.program_id(0)
    first = i == 0
    last = i == pl.num_programs(0) - 1
    x = x_ref[...]
    r = x.shape[0]
    row = lax.broadcasted_iota(jnp.int32, (r, 1), 0)
    prev_row = jnp.where(first, 0.0, xp_ref[SUBLANES - 1:SUBLANES, :])
    next_row = jnp.where(last, 0.0, xn_ref[0:1, :])
    xm1 = jnp.where(row == 0, prev_row, pltpu.roll(x, 1, 0))
    xp1 = jnp.where(row == r - 1, next_row, pltpu.roll(x, r - 1, 0))
    o_ref[0] = w_ref[0:1, :] * xm1 + w_ref[1:2, :] * x + w_ref[2:3, :] * xp1


def _conv3(p_b, row0, ts, col_b, w3, wmix):
    r = min(EW_ROWS, ts)
    rb = row0 // r
    hb = r // SUBLANES
    cb = col_b // wmix
    n_r = ts // r
    return pl.pallas_call(
        _conv3_kernel,
        grid=(n_r, 3),
        in_specs=[
            pl.BlockSpec((r, wmix), lambda i, j: (rb + i, cb + j)),
            pl.BlockSpec((SUBLANES, wmix), lambda i, j: (jnp.maximum((rb + i) * hb - 1, 0), cb + j)),
            pl.BlockSpec((SUBLANES, wmix),
                         lambda i, j: (jnp.minimum((rb + i + 1) * hb, (rb + n_r) * hb - 1), cb + j)),
            pl.BlockSpec((CONV_B, wmix), lambda i, j: (0, j)),
        ],
        out_specs=pl.BlockSpec((1, r, wmix), lambda i, j: (j, i, 0)),
        out_shape=jax.ShapeDtypeStruct((3, ts, wmix), F32),
        compiler_params=_params("arbitrary", "arbitrary"),
        name="hyena_conv3",
    )(p_b, p_b, p_b, w3)


def _filter_kernel(z_ref, tl_ref, w1_ref, b1_ref, fr_ref, w2_ref, b2_ref, w3_ref, w3b_ref, ad_ref,
                   k_ref, ssq_ref, *, half_tiles):
    i = pl.program_id(0)
    hi = lax.Precision.HIGHEST
    fr = fr_ref[...]
    h = jnp.sin(fr * (jnp.dot(z_ref[...], w1_ref[...], precision=hi, preferred_element_type=F32)
                      + b1_ref[...]))
    h = jnp.sin(fr * (jnp.dot(h, w2_ref[...], precision=hi, preferred_element_type=F32) + b2_ref[...]))
    decay = jnp.exp(-tl_ref[...] * ad_ref[...])
    taps = _dot(h, w3_ref[0]) * decay
    r = taps.shape[0]
    row = lax.broadcasted_iota(jnp.int32, (r, 1), 0)
    k_ref[...] = taps

    @pl.when(i == 0)
    def _():
        back = _dot(h, w3b_ref[0]) * decay
        k_ref[...] = taps + jnp.where(row == 0, back, 0.0)
        ssq_ref[...] = jnp.zeros_like(ssq_ref)

    @pl.when(i == half_tiles)
    def _():
        k_ref[...] = jnp.where(row == 0, 0.0, taps)

    kk = k_ref[...]
    ssq_ref[...] += jnp.sum(kk * kk, axis=0, keepdims=True)


def _hyena_filter(seq, p):
    wmix = p["hy_skip"].shape[-1]
    hidden = p["hy_fw1"].shape[-1]
    r = min(EW_ROWS, seq)
    t_idx = jnp.arange(seq, dtype=F32)
    t_lin = t_idx / max(seq - 1, 1)
    bands = jnp.linspace(1e-4, HYENA_BANDS - 1, HYENA_BANDS, dtype=F32)
    ang = (2.0 * math.pi / seq) * t_idx[:, None] * bands[None, :]
    z = jnp.concatenate([t_lin[:, None], jnp.cos(ang), -jnp.sin(ang)], axis=-1)
    rev = lambda a: jnp.concatenate([a[:1], jnp.flip(a[1:], axis=0)], axis=0)
    zz = jnp.concatenate([z, rev(z)], axis=0)
    zz = jnp.pad(zz, ((0, 0), (0, LANES - HYENA_EMB)))
    tl = jnp.concatenate([t_lin, rev(t_lin)])[:, None]
    w1 = jnp.pad(p["hy_fw1"], ((0, LANES - HYENA_EMB), (0, 0)))
    w3 = p["hy_fw3"].reshape(hidden, HYENA_ORDER, 2, wmix).transpose(2, 0, 1, 3)
    w3 = w3.reshape(2, hidden, HYENA_ORDER * wmix)
    deltas = jnp.linspace(math.log(HYENA_TARGET) / HYENA_SLOW_DECAY,
                          math.log(HYENA_TARGET) / HYENA_FAST_DECAY, wmix, dtype=F32)
    ad = jnp.tile(jnp.abs(deltas), HYENA_ORDER)[None, :]
    ow = HYENA_ORDER * wmix
    half = seq // r
    full = lambda shape: pl.BlockSpec(shape, lambda i: tuple(0 for _ in shape))
    return pl.pallas_call(
        functools.partial(_filter_kernel, half_tiles=half),
        grid=(2 * half,),
        in_specs=[
            pl.BlockSpec((r, LANES), lambda i: (i, 0)),
            pl.BlockSpec((r, 1), lambda i: (i, 0)),
            full((LANES, hidden)), full((1, hidden)), full((1, hidden)),
            full((hidden, hidden)), full((1, hidden)),
            pl.BlockSpec((1, hidden, ow), lambda i: (jnp.where(i >= half, 1, 0), 0, 0)),
            pl.BlockSpec((1, hidden, ow), lambda i: (1, 0, 0)),
            full((1, ow)),
        ],
        out_specs=[pl.BlockSpec((r, ow), lambda i: (i, 0)),
                   pl.BlockSpec((1, ow), lambda i: (0, 0))],
        out_shape=[jax.ShapeDtypeStruct((2 * seq, ow), F32),
                   jax.ShapeDtypeStruct((1, ow), F32)],
        compiler_params=_params("arbitrary"),
        name="hyena_filter",
    )(zz, tl, w1, p["hy_fb1"][None, :], p["hy_freq"][None, :], p["hy_fw2"], p["hy_fb2"][None, :],
      w3, w3, ad)


def _kron_fwd_kernel(f_ref, x_ref, o_ref):
    f = f_ref[...]
    halves = []
    for h in range(2):
        xh = x_ref[0, :, h * SUBLANES:(h + 1) * SUBLANES, :]
        xh = xh.reshape(xh.shape[0] * SUBLANES, xh.shape[2]).astype(BF16)
        r = jnp.dot(f, xh, preferred_element_type=F32)
        halves.append(r.reshape(r.shape[0] // SUBLANES, SUBLANES, r.shape[1]))
    o_ref[...] = jnp.concatenate(halves, axis=1).astype(o_ref.dtype)


def _kron_fwd(fk, x4, sel, name):
    _, nt1, nt2, w = x4.shape
    rows = fk.shape[0] // SUBLANES
    tw = 512 if w % 512 == 0 else LANES
    rt = 2 * SUBLANES
    return pl.pallas_call(
        _kron_fwd_kernel,
        grid=(nt2 // rt, w // tw),
        in_specs=[pl.BlockSpec(fk.shape, lambda i, j: (0, 0)),
                  pl.BlockSpec((1, nt1, rt, tw), lambda i, j: (sel, 0, i, j))],
        out_specs=pl.BlockSpec((rows, rt, tw), lambda i, j: (0, i, j)),
        out_shape=jax.ShapeDtypeStruct((rows, nt2, w), BF16),
        compiler_params=_params("parallel", "parallel"),
        name=name,
    )(fk, x4)


def _kron_inv_gate_kernel(g_ref, b_ref, x_ref, v_ref, sk_ref, o_ref):
    g = g_ref[...]
    b = b_ref[...].astype(F32)
    sk = sk_ref[...]
    halves = []
    for h in range(2):
        lo, hi = h * SUBLANES, (h + 1) * SUBLANES
        bh = b[:, lo:hi, :]
        bh = bh.reshape(bh.shape[0] * SUBLANES, bh.shape[2]).astype(BF16)
        y = jnp.dot(g, bh, preferred_element_type=F32)
        y = y.reshape(y.shape[0] // SUBLANES, SUBLANES, y.shape[1])
        halves.append(x_ref[0, :, lo:hi, :] * (y + sk * v_ref[0, :, lo:hi, :]))
    o_ref[0] = jnp.concatenate(halves, axis=1).astype(o_ref.dtype)


def _kron_inv_gate(gk, b3, x4, x_sel, v4, v_sel, sk, out_dtype, name):
    _, nt2, w = b3.shape
    nt1 = gk.shape[0] // SUBLANES
    tw = 512 if w % 512 == 0 else LANES
    rt = 2 * SUBLANES
    return pl.pallas_call(
        _kron_inv_gate_kernel,
        grid=(nt2 // rt, w // tw),
        in_specs=[pl.BlockSpec(gk.shape, lambda i, j: (0, 0)),
                  pl.BlockSpec((b3.shape[0], rt, tw), lambda i, j: (0, i, j)),
                  pl.BlockSpec((1, nt1, rt, tw), lambda i, j: (x_sel, 0, i, j)),
                  pl.BlockSpec((1, nt1, rt, tw), lambda i, j: (v_sel, 0, i, j)),
                  pl.BlockSpec((1, 1, tw), lambda i, j: (0, 0, j))],
        out_specs=pl.BlockSpec((1, nt1, rt, tw), lambda i, j: (0, 0, i, j)),
        out_shape=jax.ShapeDtypeStruct((1, nt1, nt2, w), out_dtype),
        compiler_params=_params("parallel", "parallel"),
        name=name,
    )(gk, b3, x4, v4, sk.reshape(1, 1, w))


def _bmm_scale_kernel(m_ref, a_ref, s_ref, o_ref, *, bpb):
    rows_in = a_ref.shape[0] // bpb
    rows_out = o_ref.shape[0] // bpb
    for b in range(bpb):
        y = _dot(m_ref[b], a_ref[b * rows_in:(b + 1) * rows_in, :]) * s_ref[...]
        o_ref[b * rows_out:(b + 1) * rows_out, :] = y.astype(o_ref.dtype)


def _bmm_scale(m2, a, scale, rows_in, name):
    nb, rows_out, _ = m2.shape
    n = a.shape[1]
    tn = n // 2 if (n // 2) % LANES == 0 else n
    bpb = 4 if nb % 4 == 0 else 1
    return pl.pallas_call(
        functools.partial(_bmm_scale_kernel, bpb=bpb),
        grid=(nb // bpb, n // tn),
        in_specs=[pl.BlockSpec((bpb, rows_out, rows_in), lambda b, j: (b, 0, 0)),
                  pl.BlockSpec((bpb * rows_in, tn), lambda b, j: (b, j)),
                  pl.BlockSpec((1, tn), lambda b, j: (0, j))],
        out_specs=pl.BlockSpec((bpb * rows_out, tn), lambda b, j: (b, j)),
        out_shape=jax.ShapeDtypeStruct((nb * rows_out, n), BF16),
        compiler_params=_params("parallel", "arbitrary"),
        name=name,
    )(m2, a, scale)


def _spectral_one(m2, a, k, m3):
    x = _dot(m2, a)
    f = x.shape[0] // 2
    xr, xi = x[:f], x[f:]
    kr, ki = k[:f].astype(F32), k[f:].astype(F32)
    y = jnp.concatenate([xr * kr - xi * ki, xr * ki + xi * kr], axis=0)
    return _dot(m3, y)


def _spectral_kernel(m2_ref, a_ref, k_ref, m3_ref, o_ref, *, bpb):
    rows_in = a_ref.shape[0] // bpb
    f2 = k_ref.shape[0] // bpb
    rows_out = o_ref.shape[0] // bpb
    for b in range(bpb):
        y = _spectral_one(m2_ref[b], a_ref[b * rows_in:(b + 1) * rows_in, :],
                          k_ref[b * f2:(b + 1) * f2, :], m3_ref[b])
        o_ref[b * rows_out:(b + 1) * rows_out, :] = y.astype(o_ref.dtype)


def _spectral_gate_kernel(m2_ref, a_ref, k_ref, m3_ref, x_ref, v_ref, sk_ref, o_ref):
    y = _spectral_one(m2_ref[0], a_ref[...], k_ref[...], m3_ref[0])
    o_ref[...] = (x_ref[...] * (y + sk_ref[...] * v_ref[...])).astype(o_ref.dtype)


def _spectral(m2, a, kspec, order, m3, out_dtype, gate=None, name="hyena_spectral"):
    nb, f2, rows_in = m2.shape
    rows_out = m3.shape[1]
    wmix = a.shape[1]
    bpb = 4 if (nb % 4 == 0 and gate is None) else 1
    in_specs = [pl.BlockSpec((bpb, f2, rows_in), lambda b: (b, 0, 0)),
                pl.BlockSpec((bpb * rows_in, wmix), lambda b: (b, 0)),
                pl.BlockSpec((bpb * f2, wmix), lambda b: (b, order)),
                pl.BlockSpec((bpb, rows_out, f2), lambda b: (b, 0, 0))]
    args = [m2, a, kspec, m3]
    kern = functools.partial(_spectral_kernel, bpb=bpb)
    if gate is not None:
        x, v, sk = gate
        blk = pl.BlockSpec((rows_out, wmix), lambda b: (b, 0))
        in_specs += [blk, blk, pl.BlockSpec((1, wmix), lambda b: (0, 0))]
        args += [x, v, sk]
        kern = _spectral_gate_kernel
    return pl.pallas_call(
        kern,
        grid=(nb // bpb,),
        in_specs=in_specs,
        out_specs=pl.BlockSpec((bpb * rows_out, wmix), lambda b: (b, 0)),
        out_shape=jax.ShapeDtypeStruct((nb * rows_out, wmix), out_dtype),
        compiler_params=_params("parallel"),
        name=name,
    )(*args)


def _dft_tables(seq):
    n = 2 * seq
    n2 = DFT_INNER
    n1 = n // n2
    i1 = jnp.arange(n1, dtype=jnp.int32)
    i2 = jnp.arange(n2, dtype=jnp.int32)
    ang1 = (2.0 * math.pi / n1) * ((i1[:, None] * i1[None, :]) % n1).astype(F32)
    c1, s1 = jnp.cos(ang1), jnp.sin(ang1)
    f1 = jnp.stack([c1, -s1], axis=1).reshape(2 * n1, n1)
    q = i1[:, None, None] + n1 * i2[None, :, None]
    ang = (2.0 * math.pi / n) * ((q * i2[None, None, :]) % n).astype(F32)
    tr, ti = jnp.cos(ang), -jnp.sin(ang)
    m2 = jnp.concatenate([jnp.concatenate([tr, -ti], axis=2),
                          jnp.concatenate([ti, tr], axis=2)], axis=1)
    trt, tit = jnp.swapaxes(tr, 1, 2), jnp.swapaxes(ti, 1, 2)
    m3 = jnp.concatenate([jnp.concatenate([trt, tit], axis=2),
                          jnp.concatenate([-tit, trt], axis=2)], axis=1)
    g = jnp.stack([c1, -s1], axis=2).reshape(n1, 2 * n1)[: n1 // 2] / n
    eye = jnp.eye(SUBLANES, dtype=F32)
    kron = lambda m: jnp.kron(m, eye).astype(BF16)
    return kron(f1[:, : n1 // 2]), kron(f1), m2.astype(BF16), m3.astype(BF16), kron(g)


def _direct_dft_tables(seq):
    n = 2 * seq
    i = jnp.arange(n, dtype=jnp.int32)
    ang = (2.0 * math.pi / n) * ((i[:, None] * i[None, :]) % n).astype(F32)
    c, s = jnp.cos(ang), jnp.sin(ang)
    fwd = jnp.concatenate([c, -s], axis=0)
    inv = jnp.concatenate([c[:seq], -s[:seq]], axis=1) / n
    return fwd[:, :seq].astype(BF16)[None], fwd.astype(BF16)[None], inv.astype(BF16)[None]


def _hyena_long(u3, taps, ssq, skip, tables):
    fk_half, fk_full, m2, m3, gk = tables
    _, seq, wmix = u3.shape
    n2 = DFT_INNER
    n1 = 2 * seq // n2
    ow = taps.shape[1]
    scale = lax.rsqrt(ssq + EPS)
    ak = _kron_fwd(fk_full, taps.reshape(1, n1, n2, ow), 0, "hyena_filter_dft1")
    kspec = _bmm_scale(m2, ak.reshape(n1 * 2 * n2, ow), scale, 2 * n2, "hyena_filter_dft2")
    u4 = u3.reshape(3, n1 // 2, n2, wmix)
    z4, z_sel = u4, 0
    for o in range(HYENA_ORDER):
        a = _kron_fwd(fk_half, z4, z_sel, "hyena_dft1")
        b = _spectral(m2, a.reshape(n1 * 2 * n2, wmix), kspec, o, m3, BF16)
        last = o == HYENA_ORDER - 1
        z4 = _kron_inv_gate(gk, b.reshape(2 * n1, n2, wmix), u4, 1 + o, z4, z_sel, skip[o],
                            BF16 if last else F32, "hyena_dft4_gate")
        z_sel = 0
    return z4.reshape(seq, wmix)


def _hyena_short(u3, taps, ssq, skip, tables):
    fwd_half, fwd_full, inv = tables
    scale = lax.rsqrt(ssq + EPS)
    kspec = _bmm_scale(fwd_full, taps, scale, taps.shape[0], "hyena_ctx_filter_dft")
    z = u3[0]
    for o in range(HYENA_ORDER):
        last = o == HYENA_ORDER - 1
        z = _spectral(fwd_half, z, kspec, o, inv, BF16 if last else F32,
                      gate=(u3[1 + o], z, skip[o][None, :]), name="hyena_ctx_spectral")
    return z


def _attn_kernel(q_ref, k_ref, v_ref, o_ref, m_sc, acc_sc, s_sc, p_sc, a_sc, *, rows):
    j = pl.program_id(2)
    _, tq, tk = s_sc.shape
    nlb = tk // LANES

    @pl.when(j == 0)
    def _():
        m_sc[...] = jnp.full_like(m_sc, -jnp.inf)
        acc_sc[...] = jnp.zeros_like(acc_sc)

    k = k_ref[...]
    v = v_ref[...]
    v1 = jnp.concatenate([v, jnp.ones_like(v)], axis=1)
    for g in range(GQA_GROUP):
        q = q_ref[:, g * HEAD_DIM:(g + 1) * HEAD_DIM]
        s_sc[g] = lax.dot_general(q, k, (((1,), (1,)), ((), ())), preferred_element_type=F32)

    for g in range(GQA_GROUP):
        for c in range(tq // rows):
            rs = slice(c * rows, (c + 1) * rows)
            blocks = [s_sc[g, rs, b * LANES:(b + 1) * LANES] for b in range(nlb)]
            bmax = blocks[0]
            for blk in blocks[1:]:
                bmax = jnp.maximum(bmax, blk)
            m_prev = m_sc[g, rs, :]
            m_new = jnp.maximum(m_prev, jnp.max(bmax, axis=1, keepdims=True))
            for b, blk in enumerate(blocks):
                p_sc[g, rs, b * LANES:(b + 1) * LANES] = jnp.exp2(blk - m_new).astype(BF16)
            m_sc[g, rs, :] = m_new
            a_sc[g, rs, :] = jnp.exp2(m_prev - m_new)

    for g in range(GQA_GROUP):
        alpha = jnp.concatenate([a_sc[g], a_sc[g]], axis=1)
        acc_sc[g] = alpha * acc_sc[g] + jnp.dot(p_sc[g], v1, preferred_element_type=F32)

    @pl.when(j == pl.num_programs(2) - 1)
    def _():
        for g in range(GQA_GROUP):
            acc = acc_sc[g]
            o_ref[:, g * HEAD_DIM:(g + 1) * HEAD_DIM] = (
                acc[:, :HEAD_DIM] / acc[:, HEAD_DIM:]).astype(o_ref.dtype)


def _attention(q, k, v, q_row0, n_q, k_row0, n_k, tq, tk):
    n_kv = k.shape[1] // HEAD_DIM
    gw = GQA_GROUP * HEAD_DIM
    qb, kb = q_row0 // tq, k_row0 // tk
    return pl.pallas_call(
        functools.partial(_attn_kernel, rows=2 * SUBLANES),
        grid=(n_kv, n_q // tq, n_k // tk),
        in_specs=[pl.BlockSpec((tq, gw), lambda h, i, j: (qb + i, h)),
                  pl.BlockSpec((tk, HEAD_DIM), lambda h, i, j: (kb + j, h)),
                  pl.BlockSpec((tk, HEAD_DIM), lambda h, i, j: (kb + j, h))],
        out_specs=pl.BlockSpec((tq, gw), lambda h, i, j: (i, h)),
        out_shape=jax.ShapeDtypeStruct((n_q, q.shape[1]), BF16),
        scratch_shapes=[pltpu.VMEM((GQA_GROUP, tq, LANES), F32),
                        pltpu.VMEM((GQA_GROUP, tq, 2 * HEAD_DIM), F32),
                        pltpu.VMEM((GQA_GROUP, tq, tk), F32),
                        pltpu.VMEM((GQA_GROUP, tq, tk), BF16),
                        pltpu.VMEM((GQA_GROUP, tq, LANES), F32)],
        compiler_params=_params("parallel", "parallel", "arbitrary"),
        name="attention",
    )(q, k, v)


def _rope_tables(seq, n_ctx):
    rows = seq // GRID_W
    row = jnp.repeat(jnp.arange(rows, dtype=F32), GRID_W)
    col = jnp.tile(jnp.arange(GRID_W, dtype=F32), rows)
    n_pairs = HEAD_DIM // 4
    inv = ROPE_THETA ** (-jnp.arange(n_pairs, dtype=F32) / n_pairs)
    ang = jnp.concatenate([row[:, None] * inv, col[:, None] * inv], axis=-1)
    ang = jnp.concatenate([ang, jnp.zeros((n_ctx, HEAD_DIM // 2), F32)], axis=0)
    c = jnp.repeat(jnp.cos(ang), 2, axis=1)
    s = jnp.repeat(jnp.sin(ang), 2, axis=1)
    even = (jnp.arange(HEAD_DIM) % 2 == 0)[None, :]
    return c, jnp.where(even, -s, 0.0), jnp.where(even, 0.0, s)


def kernel(x, c, ctx, c_ctx, w_mod_down, w_mod_up, b_mod, norm_ffn1, norm_mix, norm_ffn2,
           ffn1_w_in, ffn1_w_out, ffn2_w_in, ffn2_w_out, w_in, lru_conv, lru_w_a, lru_b_a,
           lru_w_x, lru_b_x, lru_lambda, hy_conv, hy_fw1, hy_fb1, hy_freq, hy_fw2, hy_fb2,
           hy_fw3, hy_skip, q_norm, k_norm, w_branch_a, w_branch_b, w_branch_c, w_out, final_norm):
    bsz, seq, d = x.shape
    assert bsz == 1 and c.shape[0] == 1 and ctx.shape[0] == 1
    n_ctx = ctx.shape[1]
    depth = w_in.shape[0]
    wmix = lru_conv.shape[-1]
    kvw = wmix // GQA_GROUP
    t_all = seq + n_ctx
    assert t_all % ROW_TILE == 0 and seq % EW_ROWS == 0 and n_ctx % EW_ROWS == 0

    col_ax = 0
    col_ck = col_ax + wmix
    col_cv = col_ck + kvw
    col_ag = col_cv + kvw
    col_b = col_ag + wmix
    col_cq = col_b + 3 * wmix
    col_g = col_cq + wmix

    xs = jnp.concatenate([x[0], ctx[0]], axis=0)
    cc = jnp.zeros((SUBLANES, d), F32).at[0].set(c[0]).at[1].set(c_ctx)
    mods_all = _modulation(cc, w_mod_down, w_mod_up, b_mod)

    ffn1_in, ffn1_out, ffn2_in, ffn2_out = ffn1_w_in, ffn1_w_out, ffn2_w_in, ffn2_w_out
    w_in_b, wba, wbb, wbc, w_out_b = w_in, w_branch_a, w_branch_b, w_branch_c, w_out

    rope = _rope_tables(seq, n_ctx)
    dft_lat = _dft_tables(seq)
    dft_ctx = _direct_dft_tables(n_ctx)
    q_scale = HEAD_DIM ** -0.5 * math.log2(math.e)
    tq_lat = next(t for t in (1024, 512, EW_ROWS) if seq % t == 0)
    tk_lat = next(t for t in (1408, ROW_TILE) if t_all % t == 0)

    for i in range(depth):
        ctx_out = i < depth - 1
        mods = mods_all[i]
        mods3 = mods.reshape(2 * N_MOD, 1, d)

        u = _norm_mod(xs, norm_ffn1[i], mods3, 0, seq)
        h = _ffn_up(u, ffn1_in, i)
        xs = _down(h, ffn1_out, i, xs, mods[:, 2], 0.5, seq)

        u = _norm_mod(xs, norm_mix[i], mods3, 3, seq)
        p_ax = _proj(u, w_in_b, i, col_ax, wmix, False, F32, "mixer_in_lru")
        p_b = _proj(u, w_in_b, i, col_ag, 4 * wmix, False, F32, "mixer_in_gelu_hyena")
        kh = _proj_heads(u, w_in_b, i, col_ck, kvw, k_norm[i], rope, 1.0, "mixer_in_k")
        vh = _proj(u, w_in_b, i, col_cv, kvw, False, BF16, "mixer_in_v")
        qh = _proj_heads(u, w_in_b, i, col_cq, wmix, q_norm[i], rope, q_scale, "mixer_in_q")
        gates = _proj(u, w_in_b, i, col_g, 3 * d, True, BF16, "mixer_gates")

        lru_args = (lru_conv[i], lru_w_a[i], lru_b_a[i], lru_w_x[i], lru_b_x[i], lru_lambda[i])
        ya_c, h_c = _lru(p_ax, p_b, seq, n_ctx, 0, 0, *lru_args, jnp.zeros((2, wmix), F32))
        ya_l, _ = _lru(p_ax, p_b, 0, seq, 0, 0, *lru_args, h_c)

        hp = {"hy_fw1": hy_fw1[i], "hy_fb1": hy_fb1[i], "hy_freq": hy_freq[i], "hy_fw2": hy_fw2[i],
              "hy_fb2": hy_fb2[i], "hy_fw3": hy_fw3[i], "hy_skip": hy_skip[i]}
        taps_l, ssq_l = _hyena_filter(seq, hp)
        u3_l = _conv3(p_b, 0, seq, wmix, hy_conv[i], wmix)
        yb_l = _hyena_long(u3_l, taps_l, ssq_l, hy_skip[i], dft_lat)

        yc_l = _attention(qh, kh, vh, 0, seq, 0, t_all, tq_lat, tk_lat)

        if ctx_out:
            taps_c, ssq_c = _hyena_filter(n_ctx, hp)
            u3_c = _conv3(p_b, seq, n_ctx, wmix, hy_conv[i], wmix)
            yb_c = _hyena_short(u3_c, taps_c, ssq_c, hy_skip[i], dft_ctx)
            yc_c = _attention(qh, kh, vh, seq, n_ctx, seq, n_ctx, n_ctx, n_ctx)
        else:
            yb_c = jnp.zeros((n_ctx, wmix), BF16)
            yc_c = jnp.zeros((n_ctx, wmix), BF16)

        ya = jnp.concatenate([ya_l, ya_c], axis=0)
        yb = jnp.concatenate([yb_l, yb_c], axis=0)
        yc = jnp.concatenate([yc_l, yc_c], axis=0)
        m = _merge(ya, yb, yc, wba, wbb, wbc, i, gates)
        xs = _down(m, w_out_b, i, xs, mods[:, 5], 1.0, seq)

        u = _norm_mod(xs, norm_ffn2[i], mods3, 6, seq)
        h = _ffn_up(u, ffn2_in, i)
        xs = _down(h, ffn2_out, i, xs, mods[:, 8], 0.5, seq)

    return _final_norm(xs, final_norm, seq)[None]
```

```python
import functools
import math

import jax
import jax.numpy as jnp
from jax import lax
from jax.experimental import pallas as pl
from jax.experimental.pallas import tpu as pltpu

F32 = jnp.float32
BF16 = jnp.bfloat16

HEAD_DIM = 128
LANES = 128
SUBLANES = 8
GQA_GROUP = 3
GRID_W = 64
ROPE_THETA = 10000.0
LRU_C = 8.0
CONV_A = 4
CONV_B = 3
HYENA_ORDER = 2
HYENA_BANDS = 16
HYENA_EMB = 2 * HYENA_BANDS + 1
HYENA_FAST_DECAY = 0.3
HYENA_SLOW_DECAY = 1.5
HYENA_TARGET = 1e-2
N_MOD = 9
EPS = 1e-6
DFT_INNER = 128
VMEM_LIMIT = 56 * 1024 * 1024

ROW_TILE = 768
COL_TILE = 512
EW_ROWS = 256


def _params(*sem):
    return pltpu.CompilerParams(dimension_semantics=sem, vmem_limit_bytes=VMEM_LIMIT)


def _dot(a, b):
    return jnp.dot(a.astype(BF16), b.astype(BF16), preferred_element_type=F32)


def _mod_kernel(c_ref, wd_ref, wu_ref, b_ref, o_ref):
    c = c_ref[...]
    s = c * jax.nn.sigmoid(c)
    t = _dot(s, wd_ref[0])
    o_ref[0] = _dot(t, wu_ref[0]) + b_ref[0]


def _modulation(cc, w_down, w_up, b_mod):
    depth, d, rank = w_down.shape
    out = pl.pallas_call(
        _mod_kernel,
        grid=(depth, N_MOD),
        in_specs=[
            pl.BlockSpec((SUBLANES, d), lambda l, j: (0, 0)),
            pl.BlockSpec((1, d, rank), lambda l, j: (l, 0, 0)),
            pl.BlockSpec((1, rank, d), lambda l, j: (l, 0, j)),
            pl.BlockSpec((1, 1, d), lambda l, j: (l, 0, j)),
        ],
        out_specs=pl.BlockSpec((1, SUBLANES, d), lambda l, j: (l, 0, j)),
        out_shape=jax.ShapeDtypeStruct((depth, SUBLANES, N_MOD * d), F32),
        compiler_params=_params("arbitrary", "arbitrary"),
        name="modulation",
    )(cc, w_down, w_up, b_mod.reshape(depth, 1, N_MOD * d))
    return out.reshape(depth, SUBLANES, N_MOD, d)[:, :2]


def _norm_mod_kernel(x_ref, g_ref, sh_ref, sc_ref, o_ref):
    x = x_ref[...]
    y = x * lax.rsqrt(jnp.mean(x * x, axis=-1, keepdims=True) + EPS)
    y = y * g_ref[...]
    o_ref[...] = (y * (1.0 + sc_ref[0]) + sh_ref[0]).astype(o_ref.dtype)


def _norm_mod(x, g, mods, idx, n_lat):
    t, d = x.shape
    nl = n_lat // EW_ROWS

    def sel(i, k):
        return (jnp.where(i >= nl, N_MOD, 0) + k, 0, 0)

    return pl.pallas_call(
        _norm_mod_kernel,
        grid=(t // EW_ROWS,),
        in_specs=[
            pl.BlockSpec((EW_ROWS, d), lambda i: (i, 0)),
            pl.BlockSpec((1, d), lambda i: (0, 0)),
            pl.BlockSpec((1, 1, d), lambda i: sel(i, idx)),
            pl.BlockSpec((1, 1, d), lambda i: sel(i, idx + 1)),
        ],
        out_specs=pl.BlockSpec((EW_ROWS, d), lambda i: (i, 0)),
        out_shape=jax.ShapeDtypeStruct((t, d), BF16),
        compiler_params=_params("parallel"),
        name="norm_mod",
    )(x, g.reshape(1, d), mods, mods)


def _final_norm_kernel(x_ref, g_ref, o_ref):
    x = x_ref[...]
    y = x * lax.rsqrt(jnp.mean(x * x, axis=-1, keepdims=True) + EPS)
    o_ref[...] = y * g_ref[...]


def _final_norm(x, g, n_lat):
    t, d = x.shape
    return pl.pallas_call(
        _final_norm_kernel,
        grid=(n_lat // EW_ROWS,),
        in_specs=[pl.BlockSpec((EW_ROWS, d), lambda i: (i, 0)),
                  pl.BlockSpec((1, d), lambda i: (0, 0))],
        out_specs=pl.BlockSpec((EW_ROWS, d), lambda i: (i, 0)),
        out_shape=jax.ShapeDtypeStruct((n_lat, d), F32),
        compiler_params=_params("parallel"),
        name="final_norm",
    )(x, g.reshape(1, d))


def _tall_row_tile(t):
    return next(tm for tm in (1408, ROW_TILE) if t % tm == 0)


def _serpentine(j, i, n_i):
    return jnp.where(j % 2 == 0, i, n_i - 1 - i)


def _with_bf16_weights(w_refs, wb_refs, body):
    first = pl.program_id(1) == 0

    @pl.when(first)
    def _():
        ws = []
        for w_ref, wb in zip(w_refs, wb_refs):
            wv = w_ref[0].astype(BF16)
            wb[...] = wv
            ws.append(wv)
        body(ws)

    @pl.when(jnp.logical_not(first))
    def _():
        body([wb[...] for wb in wb_refs])


def _ffn_up_kernel(u_ref, wg_ref, wu_ref, o_ref, wgb, wub):
    def body(ws):
        u = u_ref[...]
        a = jnp.dot(u, ws[0], preferred_element_type=F32)
        b = jnp.dot(u, ws[1], preferred_element_type=F32)
        o_ref[...] = (a * jax.nn.sigmoid(a) * b).astype(o_ref.dtype)

    _with_bf16_weights((wg_ref, wu_ref), (wgb, wub), body)


def _ffn_up(u, w_gu, layer):
    t, d = u.shape
    f = w_gu.shape[-1] // 2
    tn = COL_TILE // 2
    nj = f // tn
    tm = _tall_row_tile(t)
    ni = t // tm
    return pl.pallas_call(
        _ffn_up_kernel,
        grid=(nj, ni),
        in_specs=[
            pl.BlockSpec((tm, d), lambda j, i: (_serpentine(j, i, ni), 0)),
            pl.BlockSpec((1, d, tn), lambda j, i: (layer, 0, j)),
            pl.BlockSpec((1, d, tn), lambda j, i: (layer, 0, j + nj)),
        ],
        out_specs=pl.BlockSpec((tm, tn), lambda j, i: (_serpentine(j, i, ni), j)),
        out_shape=jax.ShapeDtypeStruct((t, f), BF16),
        scratch_shapes=[pltpu.VMEM((d, tn), BF16), pltpu.VMEM((d, tn), BF16)],
        compiler_params=_params("parallel", "arbitrary"),
        name="ffn_up",
    )(u, w_gu, w_gu)


def _down_kernel(h_ref, w_ref, x_ref, g_ref, o_ref, wb, *, coef, n_lat, ni):
    def body(ws):
        acc = jnp.dot(h_ref[...], ws[0], preferred_element_type=F32)
        tm = acc.shape[0]
        tile = _serpentine(pl.program_id(0), pl.program_id(1), ni)
        row = tile * tm + lax.broadcasted_iota(jnp.int32, (tm, 1), 0)
        g = jnp.where(row >= n_lat, g_ref[1:2, :], g_ref[0:1, :])
        o_ref[...] = x_ref[...] + coef * g * acc

    _with_bf16_weights((w_ref,), (wb,), body)


def _down(h, w, layer, x, gates, coef, n_lat):
    t, k = h.shape
    d = w.shape[-1]
    ni = t // ROW_TILE
    return pl.pallas_call(
        functools.partial(_down_kernel, coef=coef, n_lat=n_lat, ni=ni),
        grid=(d // COL_TILE, ni),
        in_specs=[
            pl.BlockSpec((ROW_TILE, k), lambda j, i: (_serpentine(j, i, ni), 0)),
            pl.BlockSpec((1, k, COL_TILE), lambda j, i: (layer, 0, j)),
            pl.BlockSpec((ROW_TILE, COL_TILE), lambda j, i: (_serpentine(j, i, ni), j)),
            pl.BlockSpec((2, COL_TILE), lambda j, i: (0, j)),
        ],
        out_specs=pl.BlockSpec((ROW_TILE, COL_TILE), lambda j, i: (_serpentine(j, i, ni), j)),
        out_shape=jax.ShapeDtypeStruct((t, d), F32),
        input_output_aliases={2: 0},
        scratch_shapes=[pltpu.VMEM((k, COL_TILE), BF16)],
        compiler_params=_params("parallel", "arbitrary"),
        name="down_residual",
    )(h, w, x, gates)


def _proj_kernel(u_ref, w_ref, o_ref, wb, *, sigmoid):
    def body(ws):
        acc = jnp.dot(u_ref[...], ws[0], preferred_element_type=F32)
        if sigmoid:
            acc = jax.nn.sigmoid(acc)
        o_ref[...] = acc.astype(o_ref.dtype)

    _with_bf16_weights((w_ref,), (wb,), body)


def _col_tile(col0, ncols):
    return next(t for t in (COL_TILE, 256, LANES) if col0 % t == 0 and ncols % t == 0)


def _proj(u, w, layer, col0, ncols, sigmoid, out_dtype, name):
    t, d = u.shape
    tn = _col_tile(col0, ncols)
    j0 = col0 // tn
    tm = _tall_row_tile(t)
    ni = t // tm
    return pl.pallas_call(
        functools.partial(_proj_kernel, sigmoid=sigmoid),
        grid=(ncols // tn, ni),
        in_specs=[
            pl.BlockSpec((tm, d), lambda j, i: (_serpentine(j, i, ni), 0)),
            pl.BlockSpec((1, d, tn), lambda j, i: (layer, 0, j + j0)),
        ],
        out_specs=pl.BlockSpec((tm, tn), lambda j, i: (_serpentine(j, i, ni), j)),
        out_shape=jax.ShapeDtypeStruct((t, ncols), out_dtype),
        scratch_shapes=[pltpu.VMEM((d, tn), BF16)],
        compiler_params=_params("parallel", "arbitrary"),
        name=name,
    )(u, w)


def _proj_heads_kernel(u_ref, w_ref, g_ref, cc_ref, se_ref, so_ref, o_ref, wb, *, scale):
    def body(ws):
        u = u_ref[...]
        w = ws[0]
        step = min(2 * HEAD_DIM, w.shape[1])
        for c0 in range(0, w.shape[1], step):
            acc = jnp.dot(u, w[:, c0:c0 + step], preferred_element_type=F32)
            for hh in range(step // HEAD_DIM):
                y = acc[:, hh * HEAD_DIM:(hh + 1) * HEAD_DIM]
                y = y * lax.rsqrt(jnp.mean(y * y, axis=-1, keepdims=True) + EPS) * g_ref[...]
                y = (y * cc_ref[...] + pltpu.roll(y, LANES - 1, 1) * se_ref[...]
                     + pltpu.roll(y, 1, 1) * so_ref[...])
                lo = c0 + hh * HEAD_DIM
                o_ref[:, lo:lo + HEAD_DIM] = (y * scale).astype(o_ref.dtype)

    _with_bf16_weights((w_ref,), (wb,), body)


def _proj_heads(u, w, layer, col0, ncols, gain, rope, scale, name):
    t, d = u.shape
    tn = _col_tile(col0, ncols)
    j0 = col0 // tn
    tm = _tall_row_tile(t)
    ni = t // tm
    cc, se, so = rope
    tab = pl.BlockSpec((tm, LANES), lambda j, i: (_serpentine(j, i, ni), 0))
    return pl.pallas_call(
        functools.partial(_proj_heads_kernel, scale=scale),
        grid=(ncols // tn, ni),
        in_specs=[
            pl.BlockSpec((tm, d), lambda j, i: (_serpentine(j, i, ni), 0)),
            pl.BlockSpec((1, d, tn), lambda j, i: (layer, 0, j + j0)),
            pl.BlockSpec((1, LANES), lambda j, i: (0, 0)),
            tab, tab, tab,
        ],
        out_specs=pl.BlockSpec((tm, tn), lambda j, i: (_serpentine(j, i, ni), j)),
        out_shape=jax.ShapeDtypeStruct((t, ncols), BF16),
        scratch_shapes=[pltpu.VMEM((d, tn), BF16)],
        compiler_params=_params("parallel", "arbitrary"),
        name=name,
    )(u, w, gain.reshape(1, LANES), cc, se, so)


def _merge_kernel(ya_ref, yb_ref, yc_ref, wa_ref, wb_ref, wc_ref, ga_ref, gb_ref, gc_ref, o_ref,
                  wab, wbb, wcb):
    def body(ws):
        m = ga_ref[...].astype(F32) * jnp.dot(ya_ref[...], ws[0], preferred_element_type=F32)
        m += gb_ref[...].astype(F32) * jnp.dot(yb_ref[...], ws[1], preferred_element_type=F32)
        m += gc_ref[...].astype(F32) * jnp.dot(yc_ref[...], ws[2], preferred_element_type=F32)
        o_ref[...] = m.astype(o_ref.dtype)

    _with_bf16_weights((wa_ref, wb_ref, wc_ref), (wab, wbb, wcb), body)


def _merge(ya, yb, yc, wa, wb, wc, layer, gates):
    t, w = ya.shape
    d = wa.shape[-1]
    nj = d // COL_TILE
    ni = t // ROW_TILE
    y_spec = pl.BlockSpec((ROW_TILE, w), lambda j, i: (_serpentine(j, i, ni), 0))
    w_spec = pl.BlockSpec((1, w, COL_TILE), lambda j, i: (layer, 0, j))

    def g_spec(k):
        return pl.BlockSpec((ROW_TILE, COL_TILE), lambda j, i: (_serpentine(j, i, ni), j + k * nj))

    return pl.pallas_call(
        _merge_kernel,
        grid=(nj, ni),
        in_specs=[y_spec, y_spec, y_spec, w_spec, w_spec, w_spec, g_spec(0), g_spec(1), g_spec(2)],
        out_specs=pl.BlockSpec((ROW_TILE, COL_TILE), lambda j, i: (_serpentine(j, i, ni), j)),
        out_shape=jax.ShapeDtypeStruct((t, d), BF16),
        scratch_shapes=[pltpu.VMEM((w, COL_TILE), BF16)] * 3,
        compiler_params=_params("parallel", "arbitrary"),
        name="merge",
    )(ya, yb, yc, wa, wb, wc, gates, gates, gates)


def _lru_kernel(pa_ref, pg_ref, cw_ref, wa_ref, ba_ref, wx_ref, bx_ref, lam_ref, h0_ref,
                ya_ref, hT_ref, work, a_sc, b_sc, *, ts, chunk):
    xs = work.at[0]
    pad = SUBLANES
    win = chunk + 2 * pad
    n_chunks = ts // chunk
    zeros = jnp.zeros((pad, LANES), F32)
    xs[pl.ds(0, pad), :] = zeros
    xs[pl.ds(pad + ts, pad), :] = zeros

    def copy_in(c, carry):
        t0 = pl.multiple_of(c * chunk, chunk)
        xs[pl.ds(pad + t0, chunk), :] = pa_ref[pl.ds(t0, chunk), :]
        return carry

    lax.fori_loop(0, n_chunks, copy_in, 0)

    sp = [jax.nn.softplus(-lam_ref[d:d + 1, :]) for d in range(2)]

    def gates(c, carry):
        t0 = pl.multiple_of(c * chunk, chunk)
        xw = xs[pl.ds(t0, win), :]
        xa = None
        for k in range(CONV_A):
            sh = pltpu.roll(xw, (win + 1 - k) % win, 0) if k != 1 else xw
            term = sh[pad:pad + chunk, :] * cw_ref[k:k + 1, :]
            xa = term if xa is None else xa + term
        xb = xa.astype(BF16)
        for d in range(2):
            r = jax.nn.sigmoid(_dot(xb, wa_ref[d, 0]) + ba_ref[d:d + 1, :])
            i = jax.nn.sigmoid(_dot(xb, wx_ref[d, 0]) + bx_ref[d:d + 1, :])
            log_a = -LRU_C * r * sp[d]
            a = jnp.exp(log_a)
            b = jnp.sqrt(1.0 - a * a) * (i * xa)
            a_sc[d, pl.ds(t0, chunk), :] = a
            b_sc[d, pl.ds(t0, chunk), :] = b
        return carry

    lax.fori_loop(0, n_chunks, gates, 0)

    row = lax.broadcasted_iota(jnp.int32, (SUBLANES, LANES), 0)
    steps = (1, 2, 4)

    def scan(j, carry):
        cf, cb = carry
        tf = pl.multiple_of(j * SUBLANES, SUBLANES)
        tb = pl.multiple_of(ts - (j + 1) * SUBLANES, SUBLANES)
        af = a_sc[0, pl.ds(tf, SUBLANES), :]
        bf = b_sc[0, pl.ds(tf, SUBLANES), :]
        ab = a_sc[1, pl.ds(tb, SUBLANES), :]
        bb = b_sc[1, pl.ds(tb, SUBLANES), :]
        for s in steps:
            mf = row >= s
            bf = bf + af * jnp.where(mf, pltpu.roll(bf, s, 0), 0.0)
            af = af * jnp.where(mf, pltpu.roll(af, s, 0), 1.0)
            mb = row < SUBLANES - s
            bb = bb + ab * jnp.where(mb, pltpu.roll(bb, SUBLANES - s, 0), 0.0)
            ab = ab * jnp.where(mb, pltpu.roll(ab, SUBLANES - s, 0), 1.0)
        hf = bf + af * cf
        hb = bb + ab * cb
        work[0, pl.ds(tf, SUBLANES), :] = hf
        work[1, pl.ds(tb, SUBLANES), :] = hb
        cf = jnp.broadcast_to(hf[SUBLANES - 1:SUBLANES, :], (SUBLANES, LANES))
        cb = jnp.broadcast_to(hb[0:1, :], (SUBLANES, LANES))
        return cf, cb

    c0 = (jnp.broadcast_to(h0_ref[0:1, :], (SUBLANES, LANES)),
          jnp.broadcast_to(h0_ref[1:2, :], (SUBLANES, LANES)))
    cf, cb = lax.fori_loop(0, ts // SUBLANES, scan, c0, unroll=4)
    hT_ref[0:1, :] = cf[0:1, :]
    hT_ref[1:2, :] = cb[0:1, :]

    def finish(c, carry):
        t0 = pl.multiple_of(c * chunk, chunk)
        h = work[0, pl.ds(t0, chunk), :] + work[1, pl.ds(t0, chunk), :]
        g = jax.nn.gelu(pg_ref[pl.ds(t0, chunk), :], approximate=True)
        ya_ref[pl.ds(t0, chunk), :] = (h * g).astype(ya_ref.dtype)
        return carry

    lax.fori_loop(0, n_chunks, finish, 0)


def _lru(p_ax, p_ag, row0, ts, col_ax, col_ag, cw, wa, ba, wx, bx, lam, h0):
    w = cw.shape[-1]
    nblk = w // LANES
    rb = row0 // ts
    cax = col_ax // LANES
    cag = col_ag // LANES
    chunk = min(EW_ROWS, ts)
    kern = functools.partial(_lru_kernel, ts=ts, chunk=chunk)
    vec = pl.BlockSpec((2, LANES), lambda j: (0, j))
    mat = pl.BlockSpec((2, 1, LANES, LANES), lambda j: (0, j, 0, 0))
    return pl.pallas_call(
        kern,
        grid=(nblk,),
        in_specs=[
            pl.BlockSpec((ts, LANES), lambda j: (rb, cax + j)),
            pl.BlockSpec((ts, LANES), lambda j: (rb, cag + j)),
            pl.BlockSpec((CONV_A, LANES), lambda j: (0, j)),
            mat, vec, mat, vec, vec, vec,
        ],
        out_specs=[pl.BlockSpec((ts, LANES), lambda j: (0, j)),
                   pl.BlockSpec((2, LANES), lambda j: (0, j))],
        out_shape=[jax.ShapeDtypeStruct((ts, w), BF16),
                   jax.ShapeDtypeStruct((2, w), F32)],
        scratch_shapes=[pltpu.VMEM((2, ts + 2 * SUBLANES, LANES), F32),
                        pltpu.VMEM((2, ts, LANES), F32),
                        pltpu.VMEM((2, ts, LANES), F32)],
        compiler_params=_params("parallel"),
        name="rglru",
    )(p_ax, p_ag, cw, wa, ba, wx, bx, lam, h0)


def _conv3_kernel(x_ref, xp_ref, xn_ref, w_ref, o_ref):
    i = pl.program_id(0)
    first = i == 0
    last = i == pl.num_programs(0) - 1
    x = x_ref[...]
    r = x.shape[0]
    row = lax.broadcasted_iota(jnp.int32, (r, 1), 0)
    prev_row = jnp.where(first, 0.0, xp_ref[SUBLANES - 1:SUBLANES, :])
    next_row = jnp.where(last, 0.0, xn_ref[0:1, :])
    xm1 = jnp.where(row == 0, prev_row, pltpu.roll(x, 1, 0))
    xp1 = jnp.where(row == r - 1, next_row, pltpu.roll(x, r - 1, 0))
    o_ref[0] = w_ref[0:1, :] * xm1 + w_ref[1:2, :] * x + w_ref[2:3, :] * xp1


def _conv3(p_b, row0, ts, col_b, w3, wmix):
    r = min(2 * EW_ROWS, ts)
    rb = row0 // r
    hb = r // SUBLANES
    cb = col_b // wmix
    n_r = ts // r
    return pl.pallas_call(
        _conv3_kernel,
        grid=(n_r, 3),
        in_specs=[
            pl.BlockSpec((r, wmix), lambda i, j: (rb + i, cb + j)),
            pl.BlockSpec((SUBLANES, wmix), lambda i, j: (jnp.maximum((rb + i) * hb - 1, 0), cb + j)),
            pl.BlockSpec((SUBLANES, wmix),
                         lambda i, j: (jnp.minimum((rb + i + 1) * hb, (rb + n_r) * hb - 1), cb + j)),
            pl.BlockSpec((CONV_B, wmix), lambda i, j: (0, j)),
        ],
        out_specs=pl.BlockSpec((1, r, wmix), lambda i, j: (j, i, 0)),
        out_shape=jax.ShapeDtypeStruct((3, ts, wmix), F32),
        compiler_params=_params("arbitrary", "arbitrary"),
        name="hyena_conv3",
    )(p_b, p_b, p_b, w3)


def _filter_kernel(z_ref, tl_ref, w1_ref, b1_ref, fr_ref, w2_ref, b2_ref, w3_ref, w3b_ref, ad_ref,
                   k_ref, ssq_ref, *, half_tiles):
    i = pl.program_id(0)
    hi = lax.Precision.HIGHEST
    fr = fr_ref[...]
    h = jnp.sin(fr * (jnp.dot(z_ref[...], w1_ref[...], precision=hi, preferred_element_type=F32)
                      + b1_ref[...]))
    h = jnp.sin(fr * (jnp.dot(h, w2_ref[...], precision=hi, preferred_element_type=F32) + b2_ref[...]))
    decay = jnp.exp(-tl_ref[...] * ad_ref[...])
    taps = _dot(h, w3_ref[0]) * decay
    r = taps.shape[0]
    row = lax.broadcasted_iota(jnp.int32, (r, 1), 0)
    k_ref[...] = taps

    @pl.when(i == 0)
    def _():
        back = _dot(h, w3b_ref[0]) * decay
        k_ref[...] = taps + jnp.where(row == 0, back, 0.0)
        ssq_ref[...] = jnp.zeros_like(ssq_ref)

    @pl.when(i == half_tiles)
    def _():
        k_ref[...] = jnp.where(row == 0, 0.0, taps)

    kk = k_ref[...]
    ssq_ref[...] += jnp.sum(kk * kk, axis=0, keepdims=True)


def _hyena_filter(seq, p):
    wmix = p["hy_skip"].shape[-1]
    hidden = p["hy_fw1"].shape[-1]
    r = min(2 * EW_ROWS, seq)
    t_idx = jnp.arange(seq, dtype=F32)
    t_lin = t_idx / max(seq - 1, 1)
    bands = jnp.linspace(1e-4, HYENA_BANDS - 1, HYENA_BANDS, dtype=F32)
    ang = (2.0 * math.pi / seq) * t_idx[:, None] * bands[None, :]
    z = jnp.concatenate([t_lin[:, None], jnp.cos(ang), -jnp.sin(ang)], axis=-1)
    rev = lambda a: jnp.concatenate([a[:1], jnp.flip(a[1:], axis=0)], axis=0)
    zz = jnp.concatenate([z, rev(z)], axis=0)
    zz = jnp.pad(zz, ((0, 0), (0, LANES - HYENA_EMB)))
    tl = jnp.concatenate([t_lin, rev(t_lin)])[:, None]
    w1 = jnp.pad(p["hy_fw1"], ((0, LANES - HYENA_EMB), (0, 0)))
    w3 = p["hy_fw3"].reshape(hidden, HYENA_ORDER, 2, wmix).transpose(2, 0, 1, 3)
    w3 = w3.reshape(2, hidden, HYENA_ORDER * wmix)
    deltas = jnp.linspace(math.log(HYENA_TARGET) / HYENA_SLOW_DECAY,
                          math.log(HYENA_TARGET) / HYENA_FAST_DECAY, wmix, dtype=F32)
    ad = jnp.tile(jnp.abs(deltas), HYENA_ORDER)[None, :]
    ow = HYENA_ORDER * wmix
    half = seq // r
    full = lambda shape: pl.BlockSpec(shape, lambda i: tuple(0 for _ in shape))
    return pl.pallas_call(
        functools.partial(_filter_kernel, half_tiles=half),
        grid=(2 * half,),
        in_specs=[
            pl.BlockSpec((r, LANES), lambda i: (i, 0)),
            pl.BlockSpec((r, 1), lambda i: (i, 0)),
            full((LANES, hidden)), full((1, hidden)), full((1, hidden)),
            full((hidden, hidden)), full((1, hidden)),
            pl.BlockSpec((1, hidden, ow), lambda i: (jnp.where(i >= half, 1, 0), 0, 0)),
            pl.BlockSpec((1, hidden, ow), lambda i: (1, 0, 0)),
            full((1, ow)),
        ],
        out_specs=[pl.BlockSpec((r, ow), lambda i: (i, 0)),
                   pl.BlockSpec((1, ow), lambda i: (0, 0))],
        out_shape=[jax.ShapeDtypeStruct((2 * seq, ow), F32),
                   jax.ShapeDtypeStruct((1, ow), F32)],
        compiler_params=_params("arbitrary"),
        name="hyena_filter",
    )(zz, tl, w1, p["hy_fb1"][None, :], p["hy_freq"][None, :], p["hy_fw2"], p["hy_fb2"][None, :],
      w3, w3, ad)


def _kron_fwd_kernel(f_ref, x_ref, o_ref):
    f = f_ref[...]
    halves = []
    for h in range(2):
        xh = x_ref[0, :, h * SUBLANES:(h + 1) * SUBLANES, :]
        xh = xh.reshape(xh.shape[0] * SUBLANES, xh.shape[2]).astype(BF16)
        r = jnp.dot(f, xh, preferred_element_type=F32)
        halves.append(r.reshape(r.shape[0] // SUBLANES, SUBLANES, r.shape[1]))
    o_ref[...] = jnp.concatenate(halves, axis=1).astype(o_ref.dtype)


def _kron_fwd(fk, x4, sel, name):
    _, nt1, nt2, w = x4.shape
    rows = fk.shape[0] // SUBLANES
    tw = 512 if w % 512 == 0 else LANES
    rt = 2 * SUBLANES
    return pl.pallas_call(
        _kron_fwd_kernel,
        grid=(nt2 // rt, w // tw),
        in_specs=[pl.BlockSpec(fk.shape, lambda i, j: (0, 0)),
                  pl.BlockSpec((1, nt1, rt, tw), lambda i, j: (sel, 0, i, j))],
        out_specs=pl.BlockSpec((rows, rt, tw), lambda i, j: (0, i, j)),
        out_shape=jax.ShapeDtypeStruct((rows, nt2, w), BF16),
        compiler_params=_params("parallel", "parallel"),
        name=name,
    )(fk, x4)


def _kron_inv_gate_kernel(g_ref, b_ref, x_ref, v_ref, sk_ref, o_ref):
    g = g_ref[...]
    b = b_ref[...].astype(F32)
    sk = sk_ref[...]
    halves = []
    for h in range(2):
        lo, hi = h * SUBLANES, (h + 1) * SUBLANES
        bh = b[:, lo:hi, :]
        bh = bh.reshape(bh.shape[0] * SUBLANES, bh.shape[2]).astype(BF16)
        y = jnp.dot(g, bh, preferred_element_type=F32)
        y = y.reshape(y.shape[0] // SUBLANES, SUBLANES, y.shape[1])
        halves.append(x_ref[0, :, lo:hi, :] * (y + sk * v_ref[0, :, lo:hi, :]))
    o_ref[0] = jnp.concatenate(halves, axis=1).astype(o_ref.dtype)


def _kron_inv_gate(gk, b3, x4, x_sel, v4, v_sel, sk, out_dtype, name):
    _, nt2, w = b3.shape
    nt1 = gk.shape[0] // SUBLANES
    tw = 512 if w % 512 == 0 else LANES
    rt = 2 * SUBLANES
    return pl.pallas_call(
        _kron_inv_gate_kernel,
        grid=(nt2 // rt, w // tw),
        in_specs=[pl.BlockSpec(gk.shape, lambda i, j: (0, 0)),
                  pl.BlockSpec((b3.shape[0], rt, tw), lambda i, j: (0, i, j)),
                  pl.BlockSpec((1, nt1, rt, tw), lambda i, j: (x_sel, 0, i, j)),
                  pl.BlockSpec((1, nt1, rt, tw), lambda i, j: (v_sel, 0, i, j)),
                  pl.BlockSpec((1, 1, tw), lambda i, j: (0, 0, j))],
        out_specs=pl.BlockSpec((1, nt1, rt, tw), lambda i, j: (0, 0, i, j)),
        out_shape=jax.ShapeDtypeStruct((1, nt1, nt2, w), out_dtype),
        compiler_params=_params("parallel", "parallel"),
        name=name,
    )(gk, b3, x4, v4, sk.reshape(1, 1, w))


def _bmm_scale_kernel(m_ref, a_ref, s_ref, o_ref, *, bpb):
    rows_in = a_ref.shape[0] // bpb
    rows_out = o_ref.shape[0] // bpb
    for b in range(bpb):
        y = _dot(m_ref[b], a_ref[b * rows_in:(b + 1) * rows_in, :]) * s_ref[...]
        o_ref[b * rows_out:(b + 1) * rows_out, :] = y.astype(o_ref.dtype)


def _bmm_scale(m2, a, scale, rows_in, name):
    nb, rows_out, _ = m2.shape
    n = a.shape[1]
    tn = n // 2 if (n // 2) % LANES == 0 else n
    bpb = 4 if nb % 4 == 0 else 1
    return pl.pallas_call(
        functools.partial(_bmm_scale_kernel, bpb=bpb),
        grid=(nb // bpb, n // tn),
        in_specs=[pl.BlockSpec((bpb, rows_out, rows_in), lambda b, j: (b, 0, 0)),
                  pl.BlockSpec((bpb * rows_in, tn), lambda b, j: (b, j)),
                  pl.BlockSpec((1, tn), lambda b, j: (0, j))],
        out_specs=pl.BlockSpec((bpb * rows_out, tn), lambda b, j: (b, j)),
        out_shape=jax.ShapeDtypeStruct((nb * rows_out, n), BF16),
        compiler_params=_params("parallel", "arbitrary"),
        name=name,
    )(m2, a, scale)


def _spectral_one(m2, a, k, m3):
    x = _dot(m2, a)
    f = x.shape[0] // 2
    xr, xi = x[:f], x[f:]
    kr, ki = k[:f].astype(F32), k[f:].astype(F32)
    y = jnp.concatenate([xr * kr - xi * ki, xr * ki + xi * kr], axis=0)
    return _dot(m3, y)


def _spectral_kernel(m2_ref, a_ref, k_ref, m3_ref, o_ref, *, bpb):
    rows_in = a_ref.shape[0] // bpb
    f2 = k_ref.shape[0] // bpb
    rows_out = o_ref.shape[0] // bpb
    for b in range(bpb):
        y = _spectral_one(m2_ref[b], a_ref[b * rows_in:(b + 1) * rows_in, :],
                          k_ref[b * f2:(b + 1) * f2, :], m3_ref[b])
        o_ref[b * rows_out:(b + 1) * rows_out, :] = y.astype(o_ref.dtype)


def _spectral_gate_kernel(m2_ref, a_ref, k_ref, m3_ref, x_ref, v_ref, sk_ref, o_ref):
    y = _spectral_one(m2_ref[0], a_ref[...], k_ref[...], m3_ref[0])
    o_ref[...] = (x_ref[...] * (y + sk_ref[...] * v_ref[...])).astype(o_ref.dtype)


def _spectral(m2, a, kspec, order, m3, out_dtype, gate=None, name="hyena_spectral"):
    nb, f2, rows_in = m2.shape
    rows_out = m3.shape[1]
    wmix = a.shape[1]
    bpb = 4 if (nb % 4 == 0 and gate is None) else 1
    in_specs = [pl.BlockSpec((bpb, f2, rows_in), lambda b: (b, 0, 0)),
                pl.BlockSpec((bpb * rows_in, wmix), lambda b: (b, 0)),
                pl.BlockSpec((bpb * f2, wmix), lambda b: (b, order)),
                pl.BlockSpec((bpb, rows_out, f2), lambda b: (b, 0, 0))]
    args = [m2, a, kspec, m3]
    kern = functools.partial(_spectral_kernel, bpb=bpb)
    if gate is not None:
        x, v, sk = gate
        blk = pl.BlockSpec((rows_out, wmix), lambda b: (b, 0))
        in_specs += [blk, blk, pl.BlockSpec((1, wmix), lambda b: (0, 0))]
        args += [x, v, sk]
        kern = _spectral_gate_kernel
    return pl.pallas_call(
        kern,
        grid=(nb // bpb,),
        in_specs=in_specs,
        out_specs=pl.BlockSpec((bpb * rows_out, wmix), lambda b: (b, 0)),
        out_shape=jax.ShapeDtypeStruct((nb * rows_out, wmix), out_dtype),
        compiler_params=_params("parallel"),
        name=name,
    )(*args)


def _dft_tables(seq):
    n = 2 * seq
    n2 = DFT_INNER
    n1 = n // n2
    i1 = jnp.arange(n1, dtype=jnp.int32)
    i2 = jnp.arange(n2, dtype=jnp.int32)
    ang1 = (2.0 * math.pi / n1) * ((i1[:, None] * i1[None, :]) % n1).astype(F32)
    c1, s1 = jnp.cos(ang1), jnp.sin(ang1)
    f1 = jnp.stack([c1, -s1], axis=1).reshape(2 * n1, n1)
    q = i1[:, None, None] + n1 * i2[None, :, None]
    ang = (2.0 * math.pi / n) * ((q * i2[None, None, :]) % n).astype(F32)
    tr, ti = jnp.cos(ang), -jnp.sin(ang)
    m2 = jnp.concatenate([jnp.concatenate([tr, -ti], axis=2),
                          jnp.concatenate([ti, tr], axis=2)], axis=1)
    trt, tit = jnp.swapaxes(tr, 1, 2), jnp.swapaxes(ti, 1, 2)
    m3 = jnp.concatenate([jnp.concatenate([trt, tit], axis=2),
                          jnp.concatenate([-tit, trt], axis=2)], axis=1)
    g = jnp.stack([c1, -s1], axis=2).reshape(n1, 2 * n1)[: n1 // 2] / n
    eye = jnp.eye(SUBLANES, dtype=F32)
    kron = lambda m: jnp.kron(m, eye).astype(BF16)
    return kron(f1[:, : n1 // 2]), kron(f1), m2.astype(BF16), m3.astype(BF16), kron(g)


def _direct_dft_tables(seq):
    n = 2 * seq
    i = jnp.arange(n, dtype=jnp.int32)
    ang = (2.0 * math.pi / n) * ((i[:, None] * i[None, :]) % n).astype(F32)
    c, s = jnp.cos(ang), jnp.sin(ang)
    fwd = jnp.concatenate([c, -s], axis=0)
    inv = jnp.concatenate([c[:seq], -s[:seq]], axis=1) / n
    return fwd[:, :seq].astype(BF16)[None], fwd.astype(BF16)[None], inv.astype(BF16)[None]


def _hyena_long(u3, taps, ssq, skip, tables):
    fk_half, fk_full, m2, m3, gk = tables
    _, seq, wmix = u3.shape
    n2 = DFT_INNER
    n1 = 2 * seq // n2
    ow = taps.shape[1]
    scale = lax.rsqrt(ssq + EPS)
    ak = _kron_fwd(fk_full, taps.reshape(1, n1, n2, ow), 0, "hyena_filter_dft1")
    kspec = _bmm_scale(m2, ak.reshape(n1 * 2 * n2, ow), scale, 2 * n2, "hyena_filter_dft2")
    u4 = u3.reshape(3, n1 // 2, n2, wmix)
    z4, z_sel = u4, 0
    for o in range(HYENA_ORDER):
        a = _kron_fwd(fk_half, z4, z_sel, "hyena_dft1")
        b = _spectral(m2, a.reshape(n1 * 2 * n2, wmix), kspec, o, m3, BF16)
        last = o == HYENA_ORDER - 1
        z4 = _kron_inv_gate(gk, b.reshape(2 * n1, n2, wmix), u4, 1 + o, z4, z_sel, skip[o],
                            BF16 if last else F32, "hyena_dft4_gate")
        z_sel = 0
    return z4.reshape(seq, wmix)


def _hyena_short(u3, taps, ssq, skip, tables):
    fwd_half, fwd_full, inv = tables
    scale = lax.rsqrt(ssq + EPS)
    kspec = _bmm_scale(fwd_full, taps, scale, taps.shape[0], "hyena_ctx_filter_dft")
    z = u3[0]
    for o in range(HYENA_ORDER):
        last = o == HYENA_ORDER - 1
        z = _spectral(fwd_half, z, kspec, o, inv, BF16 if last else F32,
                      gate=(u3[1 + o], z, skip[o][None, :]), name="hyena_ctx_spectral")
    return z


def _attn_kernel(q_ref, k_ref, v_ref, o_ref, m_sc, acc_sc, s_sc, p_sc, a_sc, *, rows):
    j = pl.program_id(2)
    _, tq, tk = s_sc.shape
    nlb = tk // LANES

    @pl.when(j == 0)
    def _():
        m_sc[...] = jnp.full_like(m_sc, -jnp.inf)
        acc_sc[...] = jnp.zeros_like(acc_sc)

    k = k_ref[...]
    v = v_ref[...]
    v1 = jnp.concatenate([v, jnp.ones_like(v)], axis=1)
    for g in range(GQA_GROUP):
        q = q_ref[:, g * HEAD_DIM:(g + 1) * HEAD_DIM]
        s_sc[g] = lax.dot_general(q, k, (((1,), (1,)), ((), ())), preferred_element_type=F32)

    for g in range(GQA_GROUP):
        for c in range(tq // rows):
            rs = slice(c * rows, (c + 1) * rows)
            blocks = [s_sc[g, rs, b * LANES:(b + 1) * LANES] for b in range(nlb)]
            bmax = blocks[0]
            for blk in blocks[1:]:
                bmax = jnp.maximum(bmax, blk)
            m_prev = m_sc[g, rs, :]
            m_new = jnp.maximum(m_prev, jnp.max(bmax, axis=1, keepdims=True))
            for b, blk in enumerate(blocks):
                p_sc[g, rs, b * LANES:(b + 1) * LANES] = jnp.exp2(blk - m_new).astype(BF16)
            m_sc[g, rs, :] = m_new
            a_sc[g, rs, :] = jnp.exp2(m_prev - m_new)

    for g in range(GQA_GROUP):
        alpha = jnp.concatenate([a_sc[g], a_sc[g]], axis=1)
        acc_sc[g] = alpha * acc_sc[g] + jnp.dot(p_sc[g], v1, preferred_element_type=F32)

    @pl.when(j == pl.num_programs(2) - 1)
    def _():
        for g in range(GQA_GROUP):
            acc = acc_sc[g]
            o_ref[:, g * HEAD_DIM:(g + 1) * HEAD_DIM] = (
                acc[:, :HEAD_DIM] / acc[:, HEAD_DIM:]).astype(o_ref.dtype)


def _attention(q, k, v, q_row0, n_q, k_row0, n_k, tq, tk):
    n_kv = k.shape[1] // HEAD_DIM
    gw = GQA_GROUP * HEAD_DIM
    qb, kb = q_row0 // tq, k_row0 // tk
    return pl.pallas_call(
        functools.partial(_attn_kernel, rows=2 * SUBLANES),
        grid=(n_kv, n_q // tq, n_k // tk),
        in_specs=[pl.BlockSpec((tq, gw), lambda h, i, j: (qb + i, h)),
                  pl.BlockSpec((tk, HEAD_DIM), lambda h, i, j: (kb + j, h)),
                  pl.BlockSpec((tk, HEAD_DIM), lambda h, i, j: (kb + j, h))],
        out_specs=pl.BlockSpec((tq, gw), lambda h, i, j: (i, h)),
        out_shape=jax.ShapeDtypeStruct((n_q, q.shape[1]), BF16),
        scratch_shapes=[pltpu.VMEM((GQA_GROUP, tq, LANES), F32),
                        pltpu.VMEM((GQA_GROUP, tq, 2 * HEAD_DIM), F32),
                        pltpu.VMEM((GQA_GROUP, tq, tk), F32),
                        pltpu.VMEM((GQA_GROUP, tq, tk), BF16),
                        pltpu.VMEM((GQA_GROUP, tq, LANES), F32)],
        compiler_params=_params("parallel", "parallel", "arbitrary"),
        name="attention",
    )(q, k, v)


def _rope_tables(seq, n_ctx):
    rows = seq // GRID_W
    row = jnp.repeat(jnp.arange(rows, dtype=F32), GRID_W)
    col = jnp.tile(jnp.arange(GRID_W, dtype=F32), rows)
    n_pairs = HEAD_DIM // 4
    inv = ROPE_THETA ** (-jnp.arange(n_pairs, dtype=F32) / n_pairs)
    ang = jnp.concatenate([row[:, None] * inv, col[:, None] * inv], axis=-1)
    ang = jnp.concatenate([ang, jnp.zeros((n_ctx, HEAD_DIM // 2), F32)], axis=0)
    c = jnp.repeat(jnp.cos(ang), 2, axis=1)
    s = jnp.repeat(jnp.sin(ang), 2, axis=1)
    even = (jnp.arange(HEAD_DIM) % 2 == 0)[None, :]
    return c, jnp.where(even, -s, 0.0), jnp.where(even, 0.0, s)


def kernel(x, c, ctx, c_ctx, w_mod_down, w_mod_up, b_mod, norm_ffn1, norm_mix, norm_ffn2,
           ffn1_w_in, ffn1_w_out, ffn2_w_in, ffn2_w_out, w_in, lru_conv, lru_w_a, lru_b_a,
           lru_w_x, lru_b_x, lru_lambda, hy_conv, hy_fw1, hy_fb1, hy_freq, hy_fw2, hy_fb2,
           hy_fw3, hy_skip, q_norm, k_norm, w_branch_a, w_branch_b, w_branch_c, w_out, final_norm):
    bsz, seq, d = x.shape
    assert bsz == 1 and c.shape[0] == 1 and ctx.shape[0] == 1
    n_ctx = ctx.shape[1]
    depth = w_in.shape[0]
    wmix = lru_conv.shape[-1]
    kvw = wmix // GQA_GROUP
    t_all = seq + n_ctx
    assert t_all % ROW_TILE == 0 and seq % EW_ROWS == 0 and n_ctx % EW_ROWS == 0

    col_ax = 0
    col_ck = col_ax + wmix
    col_cv = col_ck + kvw
    col_ag = col_cv + kvw
    col_b = col_ag + wmix
    col_cq = col_b + 3 * wmix
    col_g = col_cq + wmix

    xs = jnp.concatenate([x[0], ctx[0]], axis=0)
    cc = jnp.zeros((SUBLANES, d), F32).at[0].set(c[0]).at[1].set(c_ctx)
    mods_all = _modulation(cc, w_mod_down, w_mod_up, b_mod)

    ffn1_in, ffn1_out, ffn2_in, ffn2_out = ffn1_w_in, ffn1_w_out, ffn2_w_in, ffn2_w_out
    w_in_b, wba, wbb, wbc, w_out_b = w_in, w_branch_a, w_branch_b, w_branch_c, w_out

    rope = _rope_tables(seq, n_ctx)
    dft_lat = _dft_tables(seq)
    dft_ctx = _direct_dft_tables(n_ctx)
    q_scale = HEAD_DIM ** -0.5 * math.log2(math.e)
    tq_lat = next(t for t in (512, EW_ROWS) if seq % t == 0)
    tk_lat = next(t for t in (2816, 1408, ROW_TILE) if t_all % t == 0)

    for i in range(depth):
        ctx_out = i < depth - 1
        mods = mods_all[i]
        mods3 = mods.reshape(2 * N_MOD, 1, d)

        u = _norm_mod(xs, norm_ffn1[i], mods3, 0, seq)
        h = _ffn_up(u, ffn1_in, i)
        xs = _down(h, ffn1_out, i, xs, mods[:, 2], 0.5, seq)

        u = _norm_mod(xs, norm_mix[i], mods3, 3, seq)
        p_ax = _proj(u, w_in_b, i, col_ax, wmix, False, F32, "mixer_in_lru")
        p_b = _proj(u, w_in_b, i, col_ag, 4 * wmix, False, F32, "mixer_in_gelu_hyena")
        kh = _proj_heads(u, w_in_b, i, col_ck, kvw, k_norm[i], rope, 1.0, "mixer_in_k")
        vh = _proj(u, w_in_b, i, col_cv, kvw, False, BF16, "mixer_in_v")
        qh = _proj_heads(u, w_in_b, i, col_cq, wmix, q_norm[i], rope, q_scale, "mixer_in_q")
        gates = _proj(u, w_in_b, i, col_g, 3 * d, True, BF16, "mixer_gates")

        lru_args = (lru_conv[i], lru_w_a[i], lru_b_a[i], lru_w_x[i], lru_b_x[i], lru_lambda[i])
        ya_c, h_c = _lru(p_ax, p_b, seq, n_ctx, 0, 0, *lru_args, jnp.zeros((2, wmix), F32))
        ya_l, _ = _lru(p_ax, p_b, 0, seq, 0, 0, *lru_args, h_c)

        hp = {"hy_fw1": hy_fw1[i], "hy_fb1": hy_fb1[i], "hy_freq": hy_freq[i], "hy_fw2": hy_fw2[i],
              "hy_fb2": hy_fb2[i], "hy_fw3": hy_fw3[i], "hy_skip": hy_skip[i]}
        taps_l, ssq_l = _hyena_filter(seq, hp)
        u3_l = _conv3(p_b, 0, seq, wmix, hy_conv[i], wmix)
        yb_l = _hyena_long(u3_l, taps_l, ssq_l, hy_skip[i], dft_lat)

        yc_l = _attention(qh, kh, vh, 0, seq, 0, t_all, tq_lat, tk_lat)

        if ctx_out:
            taps_c, ssq_c = _hyena_filter(n_ctx, hp)
            u3_c = _conv3(p_b, seq, n_ctx, wmix, hy_conv[i], wmix)
            yb_c = _hyena_short(u3_c, taps_c, ssq_c, hy_skip[i], dft_ctx)
            yc_c = _attention(qh, kh, vh, seq, n_ctx, seq, n_ctx, n_ctx, n_ctx)
        else:
            yb_c = jnp.zeros((n_ctx, wmix), BF16)
            yc_c = jnp.zeros((n_ctx, wmix), BF16)

        ya = jnp.concatenate([ya_l, ya_c], axis=0)
        yb = jnp.concatenate([yb_l, yb_c], axis=0)
        yc = jnp.concatenate([yc_l, yc_c], axis=0)
        m = _merge(ya, yb, yc, wba, wbb, wbc, i, gates)
        xs = _down(m, w_out_b, i, xs, mods[:, 5], 1.0, seq)

        u = _norm_mod(xs, norm_ffn2[i], mods3, 6, seq)
        h = _ffn_up(u, ffn2_in, i)
        xs = _down(h, ffn2_out, i, xs, mods[:, 8], 0.5, seq)

    return _final_norm(xs, final_norm, seq)[None]
```

```python
import functools
import math

import jax
import jax.numpy as jnp
from jax import lax
from jax.experimental import pallas as pl
from jax.experimental.pallas import tpu as pltpu

F32 = jnp.float32
BF16 = jnp.bfloat16

HEAD_DIM = 128
LANES = 128
SUBLANES = 8
GQA_GROUP = 3
GRID_W = 64
ROPE_THETA = 10000.0
LRU_C = 8.0
CONV_A = 4
CONV_B = 3
HYENA_ORDER = 2
HYENA_BANDS = 16
HYENA_EMB = 2 * HYENA_BANDS + 1
HYENA_FAST_DECAY = 0.3
HYENA_SLOW_DECAY = 1.5
HYENA_TARGET = 1e-2
N_MOD = 9
EPS = 1e-6
DFT_INNER = 128
VMEM_LIMIT = 56 * 1024 * 1024

ROW_TILE = 768
COL_TILE = 512
EW_ROWS = 256


def _params(*sem):
    return pltpu.CompilerParams(dimension_semantics=sem, vmem_limit_bytes=VMEM_LIMIT)


def _dot(a, b):
    return jnp.dot(a.astype(BF16), b.astype(BF16), preferred_element_type=F32)


def _mod_kernel(c_ref, wd_ref, wu_ref, b_ref, o_ref):
    c = c_ref[...]
    s = c * jax.nn.sigmoid(c)
    t = _dot(s, wd_ref[0])
    o_ref[0] = _dot(t, wu_ref[0]) + b_ref[0]


def _modulation(cc, w_down, w_up, b_mod):
    depth, d, rank = w_down.shape
    out = pl.pallas_call(
        _mod_kernel,
        grid=(depth, N_MOD),
        in_specs=[
            pl.BlockSpec((SUBLANES, d), lambda l, j: (0, 0)),
            pl.BlockSpec((1, d, rank), lambda l, j: (l, 0, 0)),
            pl.BlockSpec((1, rank, d), lambda l, j: (l, 0, j)),
            pl.BlockSpec((1, 1, d), lambda l, j: (l, 0, j)),
        ],
        out_specs=pl.BlockSpec((1, SUBLANES, d), lambda l, j: (l, 0, j)),
        out_shape=jax.ShapeDtypeStruct((depth, SUBLANES, N_MOD * d), F32),
        compiler_params=_params("arbitrary", "arbitrary"),
        name="modulation",
    )(cc, w_down, w_up, b_mod.reshape(depth, 1, N_MOD * d))
    return out.reshape(depth, SUBLANES, N_MOD, d)[:, :2]


def _norm_mod_kernel(x_ref, g_ref, sh_ref, sc_ref, o_ref):
    x = x_ref[...]
    y = x * lax.rsqrt(jnp.mean(x * x, axis=-1, keepdims=True) + EPS)
    y = y * g_ref[...]
    o_ref[...] = (y * (1.0 + sc_ref[0]) + sh_ref[0]).astype(o_ref.dtype)


def _norm_mod(x, g, mods, idx, n_lat):
    t, d = x.shape
    nl = n_lat // EW_ROWS

    def sel(i, k):
        return (jnp.where(i >= nl, N_MOD, 0) + k, 0, 0)

    return pl.pallas_call(
        _norm_mod_kernel,
        grid=(t // EW_ROWS,),
        in_specs=[
            pl.BlockSpec((EW_ROWS, d), lambda i: (i, 0)),
            pl.BlockSpec((1, d), lambda i: (0, 0)),
            pl.BlockSpec((1, 1, d), lambda i: sel(i, idx)),
            pl.BlockSpec((1, 1, d), lambda i: sel(i, idx + 1)),
        ],
        out_specs=pl.BlockSpec((EW_ROWS, d), lambda i: (i, 0)),
        out_shape=jax.ShapeDtypeStruct((t, d), BF16),
        compiler_params=_params("parallel"),
        name="norm_mod",
    )(x, g.reshape(1, d), mods, mods)


def _final_norm_kernel(x_ref, g_ref, o_ref):
    x = x_ref[...]
    y = x * lax.rsqrt(jnp.mean(x * x, axis=-1, keepdims=True) + EPS)
    o_ref[...] = y * g_ref[...]


def _final_norm(x, g, n_lat):
    t, d = x.shape
    return pl.pallas_call(
        _final_norm_kernel,
        grid=(n_lat // EW_ROWS,),
        in_specs=[pl.BlockSpec((EW_ROWS, d), lambda i: (i, 0)),
                  pl.BlockSpec((1, d), lambda i: (0, 0))],
        out_specs=pl.BlockSpec((EW_ROWS, d), lambda i: (i, 0)),
        out_shape=jax.ShapeDtypeStruct((n_lat, d), F32),
        compiler_params=_params("parallel"),
        name="final_norm",
    )(x, g.reshape(1, d))


def _tall_row_tile(t):
    return next(tm for tm in (1408, ROW_TILE) if t % tm == 0)


def _serpentine(j, i, n_i):
    return jnp.where(j % 2 == 0, i, n_i - 1 - i)


def _with_bf16_weights(w_refs, wb_refs, body):
    first = pl.program_id(1) == 0

    @pl.when(first)
    def _():
        ws = []
        for w_ref, wb in zip(w_refs, wb_refs):
            wv = w_ref[0].astype(BF16)
            wb[...] = wv
            ws.append(wv)
        body(ws)

    @pl.when(jnp.logical_not(first))
    def _():
        body([wb[...] for wb in wb_refs])


def _ffn_up_kernel(u_ref, wg_ref, wu_ref, o_ref, wgb, wub):
    def body(ws):
        u = u_ref[...]
        a = jnp.dot(u, ws[0], preferred_element_type=F32)
        b = jnp.dot(u, ws[1], preferred_element_type=F32)
        o_ref[...] = (a * jax.nn.sigmoid(a) * b).astype(o_ref.dtype)

    _with_bf16_weights((wg_ref, wu_ref), (wgb, wub), body)


def _ffn_up(u, w_gu, layer):
    t, d = u.shape
    f = w_gu.shape[-1] // 2
    tn = COL_TILE // 2
    nj = f // tn
    tm = _tall_row_tile(t)
    ni = t // tm
    return pl.pallas_call(
        _ffn_up_kernel,
        grid=(nj, ni),
        in_specs=[
            pl.BlockSpec((tm, d), lambda j, i: (_serpentine(j, i, ni), 0)),
            pl.BlockSpec((1, d, tn), lambda j, i: (layer, 0, j)),
            pl.BlockSpec((1, d, tn), lambda j, i: (layer, 0, j + nj)),
        ],
        out_specs=pl.BlockSpec((tm, tn), lambda j, i: (_serpentine(j, i, ni), j)),
        out_shape=jax.ShapeDtypeStruct((t, f), BF16),
        scratch_shapes=[pltpu.VMEM((d, tn), BF16), pltpu.VMEM((d, tn), BF16)],
        compiler_params=_params("parallel", "arbitrary"),
        name="ffn_up",
    )(u, w_gu, w_gu)


def _down_kernel(h_ref, w_ref, x_ref, g_ref, o_ref, wb, *, coef, n_lat, ni):
    def body(ws):
        acc = jnp.dot(h_ref[...], ws[0], preferred_element_type=F32)
        tm = acc.shape[0]
        tile = _serpentine(pl.program_id(0), pl.program_id(1), ni)
        row = tile * tm + lax.broadcasted_iota(jnp.int32, (tm, 1), 0)
        g = jnp.where(row >= n_lat, g_ref[1:2, :], g_ref[0:1, :])
        o_ref[...] = x_ref[...] + coef * g * acc

    _with_bf16_weights((w_ref,), (wb,), body)


def _down(h, w, layer, x, gates, coef, n_lat):
    t, k = h.shape
    d = w.shape[-1]
    ni = t // ROW_TILE
    return pl.pallas_call(
        functools.partial(_down_kernel, coef=coef, n_lat=n_lat, ni=ni),
        grid=(d // COL_TILE, ni),
        in_specs=[
            pl.BlockSpec((ROW_TILE, k), lambda j, i: (_serpentine(j, i, ni), 0)),
            pl.BlockSpec((1, k, COL_TILE), lambda j, i: (layer, 0, j)),
            pl.BlockSpec((ROW_TILE, COL_TILE), lambda j, i: (_serpentine(j, i, ni), j)),
            pl.BlockSpec((2, COL_TILE), lambda j, i: (0, j)),
        ],
        out_specs=pl.BlockSpec((ROW_TILE, COL_TILE), lambda j, i: (_serpentine(j, i, ni), j)),
        out_shape=jax.ShapeDtypeStruct((t, d), F32),
        input_output_aliases={2: 0},
        scratch_shapes=[pltpu.VMEM((k, COL_TILE), BF16)],
        compiler_params=_params("parallel", "arbitrary"),
        name="down_residual",
    )(h, w, x, gates)


def _proj_kernel(u_ref, w_ref, o_ref, wb, *, sigmoid):
    def body(ws):
        acc = jnp.dot(u_ref[...], ws[0], preferred_element_type=F32)
        if sigmoid:
            acc = jax.nn.sigmoid(acc)
        o_ref[...] = acc.astype(o_ref.dtype)

    _with_bf16_weights((w_ref,), (wb,), body)


def _col_tile(col0, ncols):
    return next(t for t in (COL_TILE, 256, LANES) if col0 % t == 0 and ncols % t == 0)


def _proj(u, w, layer, col0, ncols, sigmoid, out_dtype, name):
    t, d = u.shape
    tn = _col_tile(col0, ncols)
    j0 = col0 // tn
    tm = _tall_row_tile(t)
    ni = t // tm
    return pl.pallas_call(
        functools.partial(_proj_kernel, sigmoid=sigmoid),
        grid=(ncols // tn, ni),
        in_specs=[
            pl.BlockSpec((tm, d), lambda j, i: (_serpentine(j, i, ni), 0)),
            pl.BlockSpec((1, d, tn), lambda j, i: (layer, 0, j + j0)),
        ],
        out_specs=pl.BlockSpec((tm, tn), lambda j, i: (_serpentine(j, i, ni), j)),
        out_shape=jax.ShapeDtypeStruct((t, ncols), out_dtype),
        scratch_shapes=[pltpu.VMEM((d, tn), BF16)],
        compiler_params=_params("parallel", "arbitrary"),
        name=name,
    )(u, w)


def _head_prep_kernel(t_ref, g_ref, cc_ref, se_ref, so_ref, o_ref, *, scale):
    ones = jnp.ones((HEAD_DIM, LANES), BF16)
    for hh in range(t_ref.shape[1] // HEAD_DIM):
        cols = slice(hh * HEAD_DIM, (hh + 1) * HEAD_DIM)
        y = t_ref[:, cols]
        sq = y * y
        hi = sq.astype(BF16)
        lo = (sq - hi.astype(F32)).astype(BF16)
        ssq = (jnp.dot(hi, ones, preferred_element_type=F32)
               + jnp.dot(lo, ones, preferred_element_type=F32))
        y = y * lax.rsqrt(ssq * (1.0 / HEAD_DIM) + EPS) * g_ref[...]
        y = (y * cc_ref[...] + pltpu.roll(y, LANES - 1, 1) * se_ref[...]
             + pltpu.roll(y, 1, 1) * so_ref[...])
        o_ref[:, cols] = (y * scale).astype(o_ref.dtype)


def _head_prep(p, gain, rope, scale, name):
    t, ncols = p.shape
    cc, se, so = rope
    tab = pl.BlockSpec((EW_ROWS, LANES), lambda i: (i, 0))
    blk = pl.BlockSpec((EW_ROWS, ncols), lambda i: (i, 0))
    return pl.pallas_call(
        functools.partial(_head_prep_kernel, scale=scale),
        grid=(t // EW_ROWS,),
        in_specs=[blk, pl.BlockSpec((1, LANES), lambda i: (0, 0)), tab, tab, tab],
        out_specs=blk,
        out_shape=jax.ShapeDtypeStruct((t, ncols), BF16),
        compiler_params=_params("parallel"),
        name=name,
    )(p, gain.reshape(1, LANES), cc, se, so)


def _merge_kernel(ya_ref, yb_ref, yc_ref, wa_ref, wb_ref, wc_ref, ga_ref, gb_ref, gc_ref, o_ref,
                  wab, wbb, wcb):
    def body(ws):
        m = ga_ref[...].astype(F32) * jnp.dot(ya_ref[...], ws[0], preferred_element_type=F32)
        m += gb_ref[...].astype(F32) * jnp.dot(yb_ref[...], ws[1], preferred_element_type=F32)
        m += gc_ref[...].astype(F32) * jnp.dot(yc_ref[...], ws[2], preferred_element_type=F32)
        o_ref[...] = m.astype(o_ref.dtype)

    _with_bf16_weights((wa_ref, wb_ref, wc_ref), (wab, wbb, wcb), body)


def _merge(ya, yb, yc, wa, wb, wc, layer, gates):
    t, w = ya.shape
    d = wa.shape[-1]
    nj = d // COL_TILE
    ni = t // ROW_TILE
    y_spec = pl.BlockSpec((ROW_TILE, w), lambda j, i: (_serpentine(j, i, ni), 0))
    w_spec = pl.BlockSpec((1, w, COL_TILE), lambda j, i: (layer, 0, j))

    def g_spec(k):
        return pl.BlockSpec((ROW_TILE, COL_TILE), lambda j, i: (_serpentine(j, i, ni), j + k * nj))

    return pl.pallas_call(
        _merge_kernel,
        grid=(nj, ni),
        in_specs=[y_spec, y_spec, y_spec, w_spec, w_spec, w_spec, g_spec(0), g_spec(1), g_spec(2)],
        out_specs=pl.BlockSpec((ROW_TILE, COL_TILE), lambda j, i: (_serpentine(j, i, ni), j)),
        out_shape=jax.ShapeDtypeStruct((t, d), BF16),
        scratch_shapes=[pltpu.VMEM((w, COL_TILE), BF16)] * 3,
        compiler_params=_params("parallel", "arbitrary"),
        name="merge",
    )(ya, yb, yc, wa, wb, wc, gates, gates, gates)


def _lru_kernel(pa_ref, pg_ref, cw_ref, wa_ref, ba_ref, wx_ref, bx_ref, lam_ref, h0_ref,
                ya_ref, hT_ref, work, a_sc, b_sc, *, ts, chunk):
    xs = work.at[0]
    pad = SUBLANES
    win = chunk + 2 * pad
    n_chunks = ts // chunk
    zeros = jnp.zeros((pad, LANES), F32)
    xs[pl.ds(0, pad), :] = zeros
    xs[pl.ds(pad + ts, pad), :] = zeros

    def copy_in(c, carry):
        t0 = pl.multiple_of(c * chunk, chunk)
        xs[pl.ds(pad + t0, chunk), :] = pa_ref[pl.ds(t0, chunk), :]
        return carry

    lax.fori_loop(0, n_chunks, copy_in, 0)

    sp = [jax.nn.softplus(-lam_ref[d:d + 1, :]) for d in range(2)]

    def gates(c, carry):
        t0 = pl.multiple_of(c * chunk, chunk)
        xw = xs[pl.ds(t0, win), :]
        xa = None
        for k in range(CONV_A):
            sh = pltpu.roll(xw, (win + 1 - k) % win, 0) if k != 1 else xw
            term = sh[pad:pad + chunk, :] * cw_ref[k:k + 1, :]
            xa = term if xa is None else xa + term
        xb = xa.astype(BF16)
        for d in range(2):
            r = jax.nn.sigmoid(_dot(xb, wa_ref[d, 0]) + ba_ref[d:d + 1, :])
            i = jax.nn.sigmoid(_dot(xb, wx_ref[d, 0]) + bx_ref[d:d + 1, :])
            log_a = -LRU_C * r * sp[d]
            a = jnp.exp(log_a)
            b = jnp.sqrt(1.0 - a * a) * (i * xa)
            a_sc[d, pl.ds(t0, chunk), :] = a
            b_sc[d, pl.ds(t0, chunk), :] = b
        return carry

    lax.fori_loop(0, n_chunks, gates, 0)

    row = lax.broadcasted_iota(jnp.int32, (SUBLANES, LANES), 0)
    steps = (1, 2, 4)

    def scan(j, carry):
        cf, cb = carry
        tf = pl.multiple_of(j * SUBLANES, SUBLANES)
        tb = pl.multiple_of(ts - (j + 1) * SUBLANES, SUBLANES)
        af = a_sc[0, pl.ds(tf, SUBLANES), :]
        bf = b_sc[0, pl.ds(tf, SUBLANES), :]
        ab = a_sc[1, pl.ds(tb, SUBLANES), :]
        bb = b_sc[1, pl.ds(tb, SUBLANES), :]
        for s in steps:
            mf = row >= s
            bf = bf + af * jnp.where(mf, pltpu.roll(bf, s, 0), 0.0)
            af = af * jnp.where(mf, pltpu.roll(af, s, 0), 1.0)
            mb = row < SUBLANES - s
            bb = bb + ab * jnp.where(mb, pltpu.roll(bb, SUBLANES - s, 0), 0.0)
            ab = ab * jnp.where(mb, pltpu.roll(ab, SUBLANES - s, 0), 1.0)
        hf = bf + af * cf
        hb = bb + ab * cb
        work[0, pl.ds(tf, SUBLANES), :] = hf
        work[1, pl.ds(tb, SUBLANES), :] = hb
        cf = jnp.broadcast_to(hf[SUBLANES - 1:SUBLANES, :], (SUBLANES, LANES))
        cb = jnp.broadcast_to(hb[0:1, :], (SUBLANES, LANES))
        return cf, cb

    c0 = (jnp.broadcast_to(h0_ref[0:1, :], (SUBLANES, LANES)),
          jnp.broadcast_to(h0_ref[1:2, :], (SUBLANES, LANES)))
    cf, cb = lax.fori_loop(0, ts // SUBLANES, scan, c0, unroll=4)
    hT_ref[0:1, :] = cf[0:1, :]
    hT_ref[1:2, :] = cb[0:1, :]

    def finish(c, carry):
        t0 = pl.multiple_of(c * chunk, chunk)
        h = work[0, pl.ds(t0, chunk), :] + work[1, pl.ds(t0, chunk), :]
        g = jax.nn.gelu(pg_ref[pl.ds(t0, chunk), :], approximate=True)
        ya_ref[pl.ds(t0, chunk), :] = (h * g).astype(ya_ref.dtype)
        return carry

    lax.fori_loop(0, n_chunks, finish, 0)


def _lru(p_ax, p_ag, row0, ts, col_ax, col_ag, cw, wa, ba, wx, bx, lam, h0):
    w = cw.shape[-1]
    nblk = w // LANES
    rb = row0 // ts
    cax = col_ax // LANES
    cag = col_ag // LANES
    chunk = min(EW_ROWS, ts)
    kern = functools.partial(_lru_kernel, ts=ts, chunk=chunk)
    vec = pl.BlockSpec((2, LANES), lambda j: (0, j))
    mat = pl.BlockSpec((2, 1, LANES, LANES), lambda j: (0, j, 0, 0))
    return pl.pallas_call(
        kern,
        grid=(nblk,),
        in_specs=[
            pl.BlockSpec((ts, LANES), lambda j: (rb, cax + j)),
            pl.BlockSpec((ts, LANES), lambda j: (rb, cag + j)),
            pl.BlockSpec((CONV_A, LANES), lambda j: (0, j)),
            mat, vec, mat, vec, vec, vec,
        ],
        out_specs=[pl.BlockSpec((ts, LANES), lambda j: (0, j)),
                   pl.BlockSpec((2, LANES), lambda j: (0, j))],
        out_shape=[jax.ShapeDtypeStruct((ts, w), BF16),
                   jax.ShapeDtypeStruct((2, w), F32)],
        scratch_shapes=[pltpu.VMEM((2, ts + 2 * SUBLANES, LANES), F32),
                        pltpu.VMEM((2, ts, LANES), F32),
                        pltpu.VMEM((2, ts, LANES), F32)],
        compiler_params=_params("parallel"),
        name="rglru",
    )(p_ax, p_ag, cw, wa, ba, wx, bx, lam, h0)


def _conv3_kernel(x_ref, xp_ref, xn_ref, w_ref, o_ref):
    i = pl.program_id(0)
    first = i == 0
    last = i == pl.num_programs(0) - 1
    x = x_ref[...]
    r = x.shape[0]
    row = lax.broadcasted_iota(jnp.int32, (r, 1), 0)
    prev_row = jnp.where(first, 0.0, xp_ref[SUBLANES - 1:SUBLANES, :])
    next_row = jnp.where(last, 0.0, xn_ref[0:1, :])
    xm1 = jnp.where(row == 0, prev_row, pltpu.roll(x, 1, 0))
    xp1 = jnp.where(row == r - 1, next_row, pltpu.roll(x, r - 1, 0))
    o_ref[0] = w_ref[0:1, :] * xm1 + w_ref[1:2, :] * x + w_ref[2:3, :] * xp1


def _conv3(p_b, row0, ts, col_b, w3, wmix):
    r = min(2 * EW_ROWS, ts)
    rb = row0 // r
    hb = r // SUBLANES
    cb = col_b // wmix
    n_r = ts // r
    return pl.pallas_call(
        _conv3_kernel,
        grid=(n_r, 3),
        in_specs=[
            pl.BlockSpec((r, wmix), lambda i, j: (rb + i, cb + j)),
            pl.BlockSpec((SUBLANES, wmix), lambda i, j: (jnp.maximum((rb + i) * hb - 1, 0), cb + j)),
            pl.BlockSpec((SUBLANES, wmix),
                         lambda i, j: (jnp.minimum((rb + i + 1) * hb, (rb + n_r) * hb - 1), cb + j)),
            pl.BlockSpec((CONV_B, wmix), lambda i, j: (0, j)),
        ],
        out_specs=pl.BlockSpec((1, r, wmix), lambda i, j: (j, i, 0)),
        out_shape=jax.ShapeDtypeStruct((3, ts, wmix), F32),
        compiler_params=_params("arbitrary", "arbitrary"),
        name="hyena_conv3",
    )(p_b, p_b, p_b, w3)


def _filter_kernel(z_ref, tl_ref, w1_ref, b1_ref, fr_ref, w2_ref, b2_ref, w3_ref, w3b_ref, ad_ref,
                   k_ref, ssq_ref, *, half_tiles):
    i = pl.program_id(0)
    hi = lax.Precision.HIGHEST
    fr = fr_ref[...]
    h = jnp.sin(fr * (jnp.dot(z_ref[...], w1_ref[...], precision=hi, preferred_element_type=F32)
                      + b1_ref[...]))
    h = jnp.sin(fr * (jnp.dot(h, w2_ref[...], precision=hi, preferred_element_type=F32) + b2_ref[...]))
    decay = jnp.exp(-tl_ref[...] * ad_ref[...])
    taps = _dot(h, w3_ref[0]) * decay
    r = taps.shape[0]
    row = lax.broadcasted_iota(jnp.int32, (r, 1), 0)
    k_ref[...] = taps

    @pl.when(i == 0)
    def _():
        back = _dot(h, w3b_ref[0]) * decay
        k_ref[...] = taps + jnp.where(row == 0, back, 0.0)
        ssq_ref[...] = jnp.zeros_like(ssq_ref)

    @pl.when(i == half_tiles)
    def _():
        k_ref[...] = jnp.where(row == 0, 0.0, taps)

    kk = k_ref[...]
    ssq_ref[...] += jnp.sum(kk * kk, axis=0, keepdims=True)


def _hyena_filter(seq, p):
    wmix = p["hy_skip"].shape[-1]
    hidden = p["hy_fw1"].shape[-1]
    r = min(2 * EW_ROWS, seq)
    t_idx = jnp.arange(seq, dtype=F32)
    t_lin = t_idx / max(seq - 1, 1)
    bands = jnp.linspace(1e-4, HYENA_BANDS - 1, HYENA_BANDS, dtype=F32)
    ang = (2.0 * math.pi / seq) * t_idx[:, None] * bands[None, :]
    z = jnp.concatenate([t_lin[:, None], jnp.cos(ang), -jnp.sin(ang)], axis=-1)
    rev = lambda a: jnp.concatenate([a[:1], jnp.flip(a[1:], axis=0)], axis=0)
    zz = jnp.concatenate([z, rev(z)], axis=0)
    zz = jnp.pad(zz, ((0, 0), (0, LANES - HYENA_EMB)))
    tl = jnp.concatenate([t_lin, rev(t_lin)])[:, None]
    w1 = jnp.pad(p["hy_fw1"], ((0, LANES - HYENA_EMB), (0, 0)))
    w3 = p["hy_fw3"].reshape(hidden, HYENA_ORDER, 2, wmix).transpose(2, 0, 1, 3)
    w3 = w3.reshape(2, hidden, HYENA_ORDER * wmix)
    deltas = jnp.linspace(math.log(HYENA_TARGET) / HYENA_SLOW_DECAY,
                          math.log(HYENA_TARGET) / HYENA_FAST_DECAY, wmix, dtype=F32)
    ad = jnp.tile(jnp.abs(deltas), HYENA_ORDER)[None, :]
    ow = HYENA_ORDER * wmix
    half = seq // r
    full = lambda shape: pl.BlockSpec(shape, lambda i: tuple(0 for _ in shape))
    return pl.pallas_call(
        functools.partial(_filter_kernel, half_tiles=half),
        grid=(2 * half,),
        in_specs=[
            pl.BlockSpec((r, LANES), lambda i: (i, 0)),
            pl.BlockSpec((r, 1), lambda i: (i, 0)),
            full((LANES, hidden)), full((1, hidden)), full((1, hidden)),
            full((hidden, hidden)), full((1, hidden)),
            pl.BlockSpec((1, hidden, ow), lambda i: (jnp.where(i >= half, 1, 0), 0, 0)),
            pl.BlockSpec((1, hidden, ow), lambda i: (1, 0, 0)),
            full((1, ow)),
        ],
        out_specs=[pl.BlockSpec((r, ow), lambda i: (i, 0)),
                   pl.BlockSpec((1, ow), lambda i: (0, 0))],
        out_shape=[jax.ShapeDtypeStruct((2 * seq, ow), F32),
                   jax.ShapeDtypeStruct((1, ow), F32)],
        compiler_params=_params("arbitrary"),
        name="hyena_filter",
    )(zz, tl, w1, p["hy_fb1"][None, :], p["hy_freq"][None, :], p["hy_fw2"], p["hy_fb2"][None, :],
      w3, w3, ad)


def _kron_fwd_kernel(f_ref, x_ref, o_ref):
    f = f_ref[...]
    halves = []
    for h in range(2):
        xh = x_ref[0, :, h * SUBLANES:(h + 1) * SUBLANES, :]
        xh = xh.reshape(xh.shape[0] * SUBLANES, xh.shape[2]).astype(BF16)
        r = jnp.dot(f, xh, preferred_element_type=F32)
        halves.append(r.reshape(r.shape[0] // SUBLANES, SUBLANES, r.shape[1]))
    o_ref[...] = jnp.concatenate(halves, axis=1).astype(o_ref.dtype)


def _kron_fwd(fk, x4, sel, name):
    _, nt1, nt2, w = x4.shape
    rows = fk.shape[0] // SUBLANES
    tw = 512 if w % 512 == 0 else LANES
    rt = 2 * SUBLANES
    return pl.pallas_call(
        _kron_fwd_kernel,
        grid=(nt2 // rt, w // tw),
        in_specs=[pl.BlockSpec(fk.shape, lambda i, j: (0, 0)),
                  pl.BlockSpec((1, nt1, rt, tw), lambda i, j: (sel, 0, i, j))],
        out_specs=pl.BlockSpec((rows, rt, tw), lambda i, j: (0, i, j)),
        out_shape=jax.ShapeDtypeStruct((rows, nt2, w), BF16),
        compiler_params=_params("parallel", "parallel"),
        name=name,
    )(fk, x4)


def _kron_inv_gate_kernel(g_ref, b_ref, x_ref, v_ref, sk_ref, o_ref):
    g = g_ref[...]
    b = b_ref[...].astype(F32)
    sk = sk_ref[...]
    halves = []
    for h in range(2):
        lo, hi = h * SUBLANES, (h + 1) * SUBLANES
        bh = b[:, lo:hi, :]
        bh = bh.reshape(bh.shape[0] * SUBLANES, bh.shape[2]).astype(BF16)
        y = jnp.dot(g, bh, preferred_element_type=F32)
        y = y.reshape(y.shape[0] // SUBLANES, SUBLANES, y.shape[1])
        halves.append(x_ref[0, :, lo:hi, :] * (y + sk * v_ref[0, :, lo:hi, :]))
    o_ref[0] = jnp.concatenate(halves, axis=1).astype(o_ref.dtype)


def _kron_inv_gate(gk, b3, x4, x_sel, v4, v_sel, sk, out_dtype, name):
    _, nt2, w = b3.shape
    nt1 = gk.shape[0] // SUBLANES
    tw = 512 if w % 512 == 0 else LANES
    rt = 2 * SUBLANES
    return pl.pallas_call(
        _kron_inv_gate_kernel,
        grid=(nt2 // rt, w // tw),
        in_specs=[pl.BlockSpec(gk.shape, lambda i, j: (0, 0)),
                  pl.BlockSpec((b3.shape[0], rt, tw), lambda i, j: (0, i, j)),
                  pl.BlockSpec((1, nt1, rt, tw), lambda i, j: (x_sel, 0, i, j)),
                  pl.BlockSpec((1, nt1, rt, tw), lambda i, j: (v_sel, 0, i, j)),
                  pl.BlockSpec((1, 1, tw), lambda i, j: (0, 0, j))],
        out_specs=pl.BlockSpec((1, nt1, rt, tw), lambda i, j: (0, 0, i, j)),
        out_shape=jax.ShapeDtypeStruct((1, nt1, nt2, w), out_dtype),
        compiler_params=_params("parallel", "parallel"),
        name=name,
    )(gk, b3, x4, v4, sk.reshape(1, 1, w))


def _bmm_scale_kernel(m_ref, a_ref, s_ref, o_ref, *, bpb):
    rows_in = a_ref.shape[0] // bpb
    rows_out = o_ref.shape[0] // bpb
    for b in range(bpb):
        y = _dot(m_ref[b], a_ref[b * rows_in:(b + 1) * rows_in, :]) * s_ref[...]
        o_ref[b * rows_out:(b + 1) * rows_out, :] = y.astype(o_ref.dtype)


def _bmm_scale(m2, a, scale, rows_in, name):
    nb, rows_out, _ = m2.shape
    n = a.shape[1]
    tn = n // 2 if (n // 2) % LANES == 0 else n
    bpb = 4 if nb % 4 == 0 else 1
    return pl.pallas_call(
        functools.partial(_bmm_scale_kernel, bpb=bpb),
        grid=(nb // bpb, n // tn),
        in_specs=[pl.BlockSpec((bpb, rows_out, rows_in), lambda b, j: (b, 0, 0)),
                  pl.BlockSpec((bpb * rows_in, tn), lambda b, j: (b, j)),
                  pl.BlockSpec((1, tn), lambda b, j: (0, j))],
        out_specs=pl.BlockSpec((bpb * rows_out, tn), lambda b, j: (b, j)),
        out_shape=jax.ShapeDtypeStruct((nb * rows_out, n), BF16),
        compiler_params=_params("parallel", "arbitrary"),
        name=name,
    )(m2, a, scale)


def _spectral_one(m2, a, k, m3):
    x = _dot(m2, a)
    f = x.shape[0] // 2
    xr, xi = x[:f], x[f:]
    kr, ki = k[:f].astype(F32), k[f:].astype(F32)
    y = jnp.concatenate([xr * kr - xi * ki, xr * ki + xi * kr], axis=0)
    return _dot(m3, y)


def _spectral_kernel(m2_ref, a_ref, k_ref, m3_ref, o_ref, *, bpb):
    rows_in = a_ref.shape[0] // bpb
    f2 = k_ref.shape[0] // bpb
    rows_out = o_ref.shape[0] // bpb
    for b in range(bpb):
        y = _spectral_one(m2_ref[b], a_ref[b * rows_in:(b + 1) * rows_in, :],
                          k_ref[b * f2:(b + 1) * f2, :], m3_ref[b])
        o_ref[b * rows_out:(b + 1) * rows_out, :] = y.astype(o_ref.dtype)


def _spectral_gate_kernel(m2_ref, a_ref, k_ref, m3_ref, x_ref, v_ref, sk_ref, o_ref):
    y = _spectral_one(m2_ref[0], a_ref[...], k_ref[...], m3_ref[0])
    o_ref[...] = (x_ref[...] * (y + sk_ref[...] * v_ref[...])).astype(o_ref.dtype)


def _spectral(m2, a, kspec, order, m3, out_dtype, gate=None, name="hyena_spectral"):
    nb, f2, rows_in = m2.shape
    rows_out = m3.shape[1]
    wmix = a.shape[1]
    bpb = 4 if (nb % 4 == 0 and gate is None) else 1
    in_specs = [pl.BlockSpec((bpb, f2, rows_in), lambda b: (b, 0, 0)),
                pl.BlockSpec((bpb * rows_in, wmix), lambda b: (b, 0)),
                pl.BlockSpec((bpb * f2, wmix), lambda b: (b, order)),
                pl.BlockSpec((bpb, rows_out, f2), lambda b: (b, 0, 0))]
    args = [m2, a, kspec, m3]
    kern = functools.partial(_spectral_kernel, bpb=bpb)
    if gate is not None:
        x, v, sk = gate
        blk = pl.BlockSpec((rows_out, wmix), lambda b: (b, 0))
        in_specs += [blk, blk, pl.BlockSpec((1, wmix), lambda b: (0, 0))]
        args += [x, v, sk]
        kern = _spectral_gate_kernel
    return pl.pallas_call(
        kern,
        grid=(nb // bpb,),
        in_specs=in_specs,
        out_specs=pl.BlockSpec((bpb * rows_out, wmix), lambda b: (b, 0)),
        out_shape=jax.ShapeDtypeStruct((nb * rows_out, wmix), out_dtype),
        compiler_params=_params("parallel"),
        name=name,
    )(*args)


def _dft_tables(seq):
    n = 2 * seq
    n2 = DFT_INNER
    n1 = n // n2
    i1 = jnp.arange(n1, dtype=jnp.int32)
    i2 = jnp.arange(n2, dtype=jnp.int32)
    ang1 = (2.0 * math.pi / n1) * ((i1[:, None] * i1[None, :]) % n1).astype(F32)
    c1, s1 = jnp.cos(ang1), jnp.sin(ang1)
    f1 = jnp.stack([c1, -s1], axis=1).reshape(2 * n1, n1)
    q = i1[:, None, None] + n1 * i2[None, :, None]
    ang = (2.0 * math.pi / n) * ((q * i2[None, None, :]) % n).astype(F32)
    tr, ti = jnp.cos(ang), -jnp.sin(ang)
    m2 = jnp.concatenate([jnp.concatenate([tr, -ti], axis=2),
                          jnp.concatenate([ti, tr], axis=2)], axis=1)
    trt, tit = jnp.swapaxes(tr, 1, 2), jnp.swapaxes(ti, 1, 2)
    m3 = jnp.concatenate([jnp.concatenate([trt, tit], axis=2),
                          jnp.concatenate([-tit, trt], axis=2)], axis=1)
    g = jnp.stack([c1, -s1], axis=2).reshape(n1, 2 * n1)[: n1 // 2] / n
    eye = jnp.eye(SUBLANES, dtype=F32)
    kron = lambda m: jnp.kron(m, eye).astype(BF16)
    return kron(f1[:, : n1 // 2]), kron(f1), m2.astype(BF16), m3.astype(BF16), kron(g)


def _direct_dft_tables(seq):
    n = 2 * seq
    i = jnp.arange(n, dtype=jnp.int32)
    ang = (2.0 * math.pi / n) * ((i[:, None] * i[None, :]) % n).astype(F32)
    c, s = jnp.cos(ang), jnp.sin(ang)
    fwd = jnp.concatenate([c, -s], axis=0)
    inv = jnp.concatenate([c[:seq], -s[:seq]], axis=1) / n
    return fwd[:, :seq].astype(BF16)[None], fwd.astype(BF16)[None], inv.astype(BF16)[None]


def _hyena_long(u3, taps, ssq, skip, tables):
    fk_half, fk_full, m2, m3, gk = tables
    _, seq, wmix = u3.shape
    n2 = DFT_INNER
    n1 = 2 * seq // n2
    ow = taps.shape[1]
    scale = lax.rsqrt(ssq + EPS)
    ak = _kron_fwd(fk_full, taps.reshape(1, n1, n2, ow), 0, "hyena_filter_dft1")
    kspec = _bmm_scale(m2, ak.reshape(n1 * 2 * n2, ow), scale, 2 * n2, "hyena_filter_dft2")
    u4 = u3.reshape(3, n1 // 2, n2, wmix)
    z4, z_sel = u4, 0
    for o in range(HYENA_ORDER):
        a = _kron_fwd(fk_half, z4, z_sel, "hyena_dft1")
        b = _spectral(m2, a.reshape(n1 * 2 * n2, wmix), kspec, o, m3, BF16)
        last = o == HYENA_ORDER - 1
        z4 = _kron_inv_gate(gk, b.reshape(2 * n1, n2, wmix), u4, 1 + o, z4, z_sel, skip[o],
                            BF16 if last else F32, "hyena_dft4_gate")
        z_sel = 0
    return z4.reshape(seq, wmix)


def _hyena_short(u3, taps, ssq, skip, tables):
    fwd_half, fwd_full, inv = tables
    scale = lax.rsqrt(ssq + EPS)
    kspec = _bmm_scale(fwd_full, taps, scale, taps.shape[0], "hyena_ctx_filter_dft")
    z = u3[0]
    for o in range(HYENA_ORDER):
        last = o == HYENA_ORDER - 1
        z = _spectral(fwd_half, z, kspec, o, inv, BF16 if last else F32,
                      gate=(u3[1 + o], z, skip[o][None, :]), name="hyena_ctx_spectral")
    return z


def _attn_kernel(q_ref, k_ref, v_ref, o_ref, m_sc, acc_sc, s_sc, p_sc, a_sc, *, rows):
    j = pl.program_id(2)
    _, tq, tk = s_sc.shape
    nlb = tk // LANES

    @pl.when(j == 0)
    def _():
        m_sc[...] = jnp.full_like(m_sc, -jnp.inf)
        acc_sc[...] = jnp.zeros_like(acc_sc)

    k = k_ref[...]
    v = v_ref[...]
    v1 = jnp.concatenate([v, jnp.ones_like(v)], axis=1)
    for g in range(GQA_GROUP):
        q = q_ref[:, g * HEAD_DIM:(g + 1) * HEAD_DIM]
        s_sc[g] = lax.dot_general(q, k, (((1,), (1,)), ((), ())), preferred_element_type=F32)

    for g in range(GQA_GROUP):
        for c in range(tq // rows):
            rs = slice(c * rows, (c + 1) * rows)
            blocks = [s_sc[g, rs, b * LANES:(b + 1) * LANES] for b in range(nlb)]
            bmax = blocks[0]
            for blk in blocks[1:]:
                bmax = jnp.maximum(bmax, blk)
            m_prev = m_sc[g, rs, :]
            m_new = jnp.maximum(m_prev, jnp.max(bmax, axis=1, keepdims=True))
            for b, blk in enumerate(blocks):
                p_sc[g, rs, b * LANES:(b + 1) * LANES] = jnp.exp2(blk - m_new).astype(BF16)
            m_sc[g, rs, :] = m_new
            a_sc[g, rs, :] = jnp.exp2(m_prev - m_new)

    for g in range(GQA_GROUP):
        alpha = jnp.concatenate([a_sc[g], a_sc[g]], axis=1)
        acc_sc[g] = alpha * acc_sc[g] + jnp.dot(p_sc[g], v1, preferred_element_type=F32)

    @pl.when(j == pl.num_programs(2) - 1)
    def _():
        for g in range(GQA_GROUP):
            acc = acc_sc[g]
            o_ref[:, g * HEAD_DIM:(g + 1) * HEAD_DIM] = (
                acc[:, :HEAD_DIM] / acc[:, HEAD_DIM:]).astype(o_ref.dtype)


def _attention(q, k, v, q_row0, n_q, k_row0, n_k, tq, tk):
    n_kv = k.shape[1] // HEAD_DIM
    gw = GQA_GROUP * HEAD_DIM
    qb, kb = q_row0 // tq, k_row0 // tk
    return pl.pallas_call(
        functools.partial(_attn_kernel, rows=2 * SUBLANES),
        grid=(n_kv, n_q // tq, n_k // tk),
        in_specs=[pl.BlockSpec((tq, gw), lambda h, i, j: (qb + i, h)),
                  pl.BlockSpec((tk, HEAD_DIM), lambda h, i, j: (kb + j, h)),
                  pl.BlockSpec((tk, HEAD_DIM), lambda h, i, j: (kb + j, h))],
        out_specs=pl.BlockSpec((tq, gw), lambda h, i, j: (i, h)),
        out_shape=jax.ShapeDtypeStruct((n_q, q.shape[1]), BF16),
        scratch_shapes=[pltpu.VMEM((GQA_GROUP, tq, LANES), F32),
                        pltpu.VMEM((GQA_GROUP, tq, 2 * HEAD_DIM), F32),
                        pltpu.VMEM((GQA_GROUP, tq, tk), F32),
                        pltpu.VMEM((GQA_GROUP, tq, tk), BF16),
                        pltpu.VMEM((GQA_GROUP, tq, LANES), F32)],
        compiler_params=_params("parallel", "parallel", "arbitrary"),
        name="attention",
    )(q, k, v)


def _rope_tables(seq, n_ctx):
    rows = seq // GRID_W
    row = jnp.repeat(jnp.arange(rows, dtype=F32), GRID_W)
    col = jnp.tile(jnp.arange(GRID_W, dtype=F32), rows)
    n_pairs = HEAD_DIM // 4
    inv = ROPE_THETA ** (-jnp.arange(n_pairs, dtype=F32) / n_pairs)
    ang = jnp.concatenate([row[:, None] * inv, col[:, None] * inv], axis=-1)
    ang = jnp.concatenate([ang, jnp.zeros((n_ctx, HEAD_DIM // 2), F32)], axis=0)
    c = jnp.repeat(jnp.cos(ang), 2, axis=1)
    s = jnp.repeat(jnp.sin(ang), 2, axis=1)
    even = (jnp.arange(HEAD_DIM) % 2 == 0)[None, :]
    return c, jnp.where(even, -s, 0.0), jnp.where(even, 0.0, s)


def kernel(x, c, ctx, c_ctx, w_mod_down, w_mod_up, b_mod, norm_ffn1, norm_mix, norm_ffn2,
           ffn1_w_in, ffn1_w_out, ffn2_w_in, ffn2_w_out, w_in, lru_conv, lru_w_a, lru_b_a,
           lru_w_x, lru_b_x, lru_lambda, hy_conv, hy_fw1, hy_fb1, hy_freq, hy_fw2, hy_fb2,
           hy_fw3, hy_skip, q_norm, k_norm, w_branch_a, w_branch_b, w_branch_c, w_out, final_norm):
    bsz, seq, d = x.shape
    assert bsz == 1 and c.shape[0] == 1 and ctx.shape[0] == 1
    n_ctx = ctx.shape[1]
    depth = w_in.shape[0]
    wmix = lru_conv.shape[-1]
    kvw = wmix // GQA_GROUP
    t_all = seq + n_ctx
    assert t_all % ROW_TILE == 0 and seq % EW_ROWS == 0 and n_ctx % EW_ROWS == 0

    col_ax = 0
    col_ck = col_ax + wmix
    col_cv = col_ck + kvw
    col_ag = col_cv + kvw
    col_b = col_ag + wmix
    col_cq = col_b + 3 * wmix
    col_g = col_cq + wmix

    xs = jnp.concatenate([x[0], ctx[0]], axis=0)
    cc = jnp.zeros((SUBLANES, d), F32).at[0].set(c[0]).at[1].set(c_ctx)
    mods_all = _modulation(cc, w_mod_down, w_mod_up, b_mod)

    ffn1_in, ffn1_out, ffn2_in, ffn2_out = ffn1_w_in, ffn1_w_out, ffn2_w_in, ffn2_w_out
    w_in_b, wba, wbb, wbc, w_out_b = w_in, w_branch_a, w_branch_b, w_branch_c, w_out

    rope = _rope_tables(seq, n_ctx)
    dft_lat = _dft_tables(seq)
    dft_ctx = _direct_dft_tables(n_ctx)
    q_scale = HEAD_DIM ** -0.5 * math.log2(math.e)
    tq_lat = next(t for t in (512, EW_ROWS) if seq % t == 0)
    tk_lat = next(t for t in (2816, 1408, ROW_TILE) if t_all % t == 0)

    for i in range(depth):
        ctx_out = i < depth - 1
        mods = mods_all[i]
        mods3 = mods.reshape(2 * N_MOD, 1, d)

        u = _norm_mod(xs, norm_ffn1[i], mods3, 0, seq)
        h = _ffn_up(u, ffn1_in, i)
        xs = _down(h, ffn1_out, i, xs, mods[:, 2], 0.5, seq)

        u = _norm_mod(xs, norm_mix[i], mods3, 3, seq)
        p_ax = _proj(u, w_in_b, i, col_ax, wmix, False, F32, "mixer_in_lru")
        p_b = _proj(u, w_in_b, i, col_ag, 4 * wmix, False, F32, "mixer_in_gelu_hyena")
        kh = _head_prep(_proj(u, w_in_b, i, col_ck, kvw, False, F32, "mixer_in_k"),
                        k_norm[i], rope, 1.0, "k_prep")
        vh = _proj(u, w_in_b, i, col_cv, kvw, False, BF16, "mixer_in_v")
        qh = _head_prep(_proj(u, w_in_b, i, col_cq, wmix, False, F32, "mixer_in_q"),
                        q_norm[i], rope, q_scale, "q_prep")
        gates = _proj(u, w_in_b, i, col_g, 3 * d, True, BF16, "mixer_gates")

        lru_args = (lru_conv[i], lru_w_a[i], lru_b_a[i], lru_w_x[i], lru_b_x[i], lru_lambda[i])
        ya_c, h_c = _lru(p_ax, p_b, seq, n_ctx, 0, 0, *lru_args, jnp.zeros((2, wmix), F32))
        ya_l, _ = _lru(p_ax, p_b, 0, seq, 0, 0, *lru_args, h_c)

        hp = {"hy_fw1": hy_fw1[i], "hy_fb1": hy_fb1[i], "hy_freq": hy_freq[i], "hy_fw2": hy_fw2[i],
              "hy_fb2": hy_fb2[i], "hy_fw3": hy_fw3[i], "hy_skip": hy_skip[i]}
        taps_l, ssq_l = _hyena_filter(seq, hp)
        u3_l = _conv3(p_b, 0, seq, wmix, hy_conv[i], wmix)
        yb_l = _hyena_long(u3_l, taps_l, ssq_l, hy_skip[i], dft_lat)

        yc_l = _attention(qh, kh, vh, 0, seq, 0, t_all, tq_lat, tk_lat)

        if ctx_out:
            taps_c, ssq_c = _hyena_filter(n_ctx, hp)
            u3_c = _conv3(p_b, seq, n_ctx, wmix, hy_conv[i], wmix)
            yb_c = _hyena_short(u3_c, taps_c, ssq_c, hy_skip[i], dft_ctx)
            yc_c = _attention(qh, kh, vh, seq, n_ctx, seq, n_ctx, n_ctx, n_ctx)
        else:
            yb_c = jnp.zeros((n_ctx, wmix), BF16)
            yc_c = jnp.zeros((n_ctx, wmix), BF16)

        ya = jnp.concatenate([ya_l, ya_c], axis=0)
        yb = jnp.concatenate([yb_l, yb_c], axis=0)
        yc = jnp.concatenate([yc_l, yc_c], axis=0)
        m = _merge(ya, yb, yc, wba, wbb, wbc, i, gates)
        xs = _down(m, w_out_b, i, xs, mods[:, 5], 1.0, seq)

        u = _norm_mod(xs, norm_ffn2[i], mods3, 6, seq)
        h = _ffn_up(u, ffn2_in, i)
        xs = _down(h, ffn2_out, i, xs, mods[:, 8], 0.5, seq)

    return _final_norm(xs, final_norm, seq)[None]
```

```python
import functools
import math

import jax
import jax.numpy as jnp
from jax import lax
from jax.experimental import pallas as pl
from jax.experimental.pallas import tpu as pltpu

F32 = jnp.float32
BF16 = jnp.bfloat16

HEAD_DIM = 128
LANES = 128
SUBLANES = 8
GQA_GROUP = 3
GRID_W = 64
ROPE_THETA = 10000.0
LRU_C = 8.0
CONV_A = 4
CONV_B = 3
HYENA_ORDER = 2
HYENA_BANDS = 16
HYENA_EMB = 2 * HYENA_BANDS + 1
HYENA_FAST_DECAY = 0.3
HYENA_SLOW_DECAY = 1.5
HYENA_TARGET = 1e-2
N_MOD = 9
EPS = 1e-6
DFT_INNER = 128
VMEM_LIMIT = 56 * 1024 * 1024

ROW_TILE = 768
COL_TILE = 512
EW_ROWS = 256


def _params(*sem):
    return pltpu.CompilerParams(dimension_semantics=sem, vmem_limit_bytes=VMEM_LIMIT)


def _dot(a, b):
    return jnp.dot(a.astype(BF16), b.astype(BF16), preferred_element_type=F32)


def _mod_kernel(c_ref, wd_ref, wu_ref, b_ref, o_ref):
    c = c_ref[...]
    s = c * jax.nn.sigmoid(c)
    t = _dot(s, wd_ref[0])
    o_ref[0] = _dot(t, wu_ref[0]) + b_ref[0]


def _modulation(cc, w_down, w_up, b_mod):
    depth, d, rank = w_down.shape
    out = pl.pallas_call(
        _mod_kernel,
        grid=(depth, N_MOD),
        in_specs=[
            pl.BlockSpec((SUBLANES, d), lambda l, j: (0, 0)),
            pl.BlockSpec((1, d, rank), lambda l, j: (l, 0, 0)),
            pl.BlockSpec((1, rank, d), lambda l, j: (l, 0, j)),
            pl.BlockSpec((1, 1, d), lambda l, j: (l, 0, j)),
        ],
        out_specs=pl.BlockSpec((1, SUBLANES, d), lambda l, j: (l, 0, j)),
        out_shape=jax.ShapeDtypeStruct((depth, SUBLANES, N_MOD * d), F32),
        compiler_params=_params("arbitrary", "arbitrary"),
        name="modulation",
    )(cc, w_down, w_up, b_mod.reshape(depth, 1, N_MOD * d))
    return out.reshape(depth, SUBLANES, N_MOD, d)[:, :2]


def _norm_mod_kernel(x_ref, g_ref, sh_ref, sc_ref, o_ref):
    x = x_ref[...]
    y = x * lax.rsqrt(jnp.mean(x * x, axis=-1, keepdims=True) + EPS)
    y = y * g_ref[...]
    o_ref[...] = (y * (1.0 + sc_ref[0]) + sh_ref[0]).astype(o_ref.dtype)


def _norm_mod(x, g, mods, idx, n_lat):
    t, d = x.shape
    nl = n_lat // EW_ROWS

    def sel(i, k):
        return (jnp.where(i >= nl, N_MOD, 0) + k, 0, 0)

    return pl.pallas_call(
        _norm_mod_kernel,
        grid=(t // EW_ROWS,),
        in_specs=[
            pl.BlockSpec((EW_ROWS, d), lambda i: (i, 0)),
            pl.BlockSpec((1, d), lambda i: (0, 0)),
            pl.BlockSpec((1, 1, d), lambda i: sel(i, idx)),
            pl.BlockSpec((1, 1, d), lambda i: sel(i, idx + 1)),
        ],
        out_specs=pl.BlockSpec((EW_ROWS, d), lambda i: (i, 0)),
        out_shape=jax.ShapeDtypeStruct((t, d), BF16),
        compiler_params=_params("parallel"),
        name="norm_mod",
    )(x, g.reshape(1, d), mods, mods)


def _final_norm_kernel(x_ref, g_ref, o_ref):
    x = x_ref[...]
    y = x * lax.rsqrt(jnp.mean(x * x, axis=-1, keepdims=True) + EPS)
    o_ref[...] = y * g_ref[...]


def _final_norm(x, g, n_lat):
    t, d = x.shape
    return pl.pallas_call(
        _final_norm_kernel,
        grid=(n_lat // EW_ROWS,),
        in_specs=[pl.BlockSpec((EW_ROWS, d), lambda i: (i, 0)),
                  pl.BlockSpec((1, d), lambda i: (0, 0))],
        out_specs=pl.BlockSpec((EW_ROWS, d), lambda i: (i, 0)),
        out_shape=jax.ShapeDtypeStruct((n_lat, d), F32),
        compiler_params=_params("parallel"),
        name="final_norm",
    )(x, g.reshape(1, d))


def _tall_row_tile(t):
    return next(tm for tm in (1408, ROW_TILE) if t % tm == 0)


def _serpentine(j, i, n_i):
    return jnp.where(j % 2 == 0, i, n_i - 1 - i)


def _with_bf16_weights(w_refs, wb_refs, body):
    first = pl.program_id(1) == 0

    @pl.when(first)
    def _():
        ws = []
        for w_ref, wb in zip(w_refs, wb_refs):
            wv = w_ref[0].astype(BF16)
            wb[...] = wv
            ws.append(wv)
        body(ws)

    @pl.when(jnp.logical_not(first))
    def _():
        body([wb[...] for wb in wb_refs])


def _ffn_up_kernel(u_ref, wg_ref, wu_ref, o_ref, wgb, wub):
    def body(ws):
        u = u_ref[...]
        a = jnp.dot(u, ws[0], preferred_element_type=F32)
        b = jnp.dot(u, ws[1], preferred_element_type=F32)
        o_ref[...] = (a * jax.nn.sigmoid(a) * b).astype(o_ref.dtype)

    _with_bf16_weights((wg_ref, wu_ref), (wgb, wub), body)


def _ffn_up(u, w_gu, layer):
    t, d = u.shape
    f = w_gu.shape[-1] // 2
    tn = COL_TILE // 2
    nj = f // tn
    tm = _tall_row_tile(t)
    ni = t // tm
    return pl.pallas_call(
        _ffn_up_kernel,
        grid=(nj, ni),
        in_specs=[
            pl.BlockSpec((tm, d), lambda j, i: (_serpentine(j, i, ni), 0)),
            pl.BlockSpec((1, d, tn), lambda j, i: (layer, 0, j)),
            pl.BlockSpec((1, d, tn), lambda j, i: (layer, 0, j + nj)),
        ],
        out_specs=pl.BlockSpec((tm, tn), lambda j, i: (_serpentine(j, i, ni), j)),
        out_shape=jax.ShapeDtypeStruct((t, f), BF16),
        scratch_shapes=[pltpu.VMEM((d, tn), BF16), pltpu.VMEM((d, tn), BF16)],
        compiler_params=_params("parallel", "arbitrary"),
        name="ffn_up",
    )(u, w_gu, w_gu)


def _down_kernel(h_ref, w_ref, x_ref, g_ref, o_ref, wb, *, coef, n_lat, ni):
    def body(ws):
        acc = jnp.dot(h_ref[...], ws[0], preferred_element_type=F32)
        tm = acc.shape[0]
        tile = _serpentine(pl.program_id(0), pl.program_id(1), ni)
        row = tile * tm + lax.broadcasted_iota(jnp.int32, (tm, 1), 0)
        g = jnp.where(row >= n_lat, g_ref[1:2, :], g_ref[0:1, :])
        o_ref[...] = x_ref[...] + coef * g * acc

    _with_bf16_weights((w_ref,), (wb,), body)


def _down(h, w, layer, x, gates, coef, n_lat):
    t, k = h.shape
    d = w.shape[-1]
    ni = t // ROW_TILE
    return pl.pallas_call(
        functools.partial(_down_kernel, coef=coef, n_lat=n_lat, ni=ni),
        grid=(d // COL_TILE, ni),
        in_specs=[
            pl.BlockSpec((ROW_TILE, k), lambda j, i: (_serpentine(j, i, ni), 0)),
            pl.BlockSpec((1, k, COL_TILE), lambda j, i: (layer, 0, j)),
            pl.BlockSpec((ROW_TILE, COL_TILE), lambda j, i: (_serpentine(j, i, ni), j)),
            pl.BlockSpec((2, COL_TILE), lambda j, i: (0, j)),
        ],
        out_specs=pl.BlockSpec((ROW_TILE, COL_TILE), lambda j, i: (_serpentine(j, i, ni), j)),
        out_shape=jax.ShapeDtypeStruct((t, d), F32),
        input_output_aliases={2: 0},
        scratch_shapes=[pltpu.VMEM((k, COL_TILE), BF16)],
        compiler_params=_params("parallel", "arbitrary"),
        name="down_residual",
    )(h, w, x, gates)


def _proj_kernel(u_ref, w_ref, o_ref, wb, *, sigmoid):
    def body(ws):
        acc = jnp.dot(u_ref[...], ws[0], preferred_element_type=F32)
        if sigmoid:
            acc = jax.nn.sigmoid(acc)
        o_ref[...] = acc.astype(o_ref.dtype)

    _with_bf16_weights((w_ref,), (wb,), body)


def _col_tile(col0, ncols):
    return next(t for t in (COL_TILE, 256, LANES) if col0 % t == 0 and ncols % t == 0)


def _proj(u, w, layer, col0, ncols, sigmoid, out_dtype, name):
    t, d = u.shape
    tn = _col_tile(col0, ncols)
    j0 = col0 // tn
    tm = _tall_row_tile(t)
    ni = t // tm
    return pl.pallas_call(
        functools.partial(_proj_kernel, sigmoid=sigmoid),
        grid=(ncols // tn, ni),
        in_specs=[
            pl.BlockSpec((tm, d), lambda j, i: (_serpentine(j, i, ni), 0)),
            pl.BlockSpec((1, d, tn), lambda j, i: (layer, 0, j + j0)),
        ],
        out_specs=pl.BlockSpec((tm, tn), lambda j, i: (_serpentine(j, i, ni), j)),
        out_shape=jax.ShapeDtypeStruct((t, ncols), out_dtype),
        scratch_shapes=[pltpu.VMEM((d, tn), BF16)],
        compiler_params=_params("parallel", "arbitrary"),
        name=name,
    )(u, w)


def _head_prep_kernel(t_ref, g_ref, cc_ref, se_ref, so_ref, o_ref, *, scale):
    ones = jnp.ones((HEAD_DIM, LANES), BF16)
    for hh in range(t_ref.shape[1] // HEAD_DIM):
        cols = slice(hh * HEAD_DIM, (hh + 1) * HEAD_DIM)
        y = t_ref[:, cols]
        sq = y * y
        hi = sq.astype(BF16)
        lo = (sq - hi.astype(F32)).astype(BF16)
        ssq = (jnp.dot(hi, ones, preferred_element_type=F32)
               + jnp.dot(lo, ones, preferred_element_type=F32))
        y = y * lax.rsqrt(ssq * (1.0 / HEAD_DIM) + EPS) * g_ref[...]
        y = (y * cc_ref[...] + pltpu.roll(y, LANES - 1, 1) * se_ref[...]
             + pltpu.roll(y, 1, 1) * so_ref[...])
        o_ref[:, cols] = (y * scale).astype(o_ref.dtype)


def _head_prep(p, gain, rope, scale, name):
    t, ncols = p.shape
    cc, se, so = rope
    tab = pl.BlockSpec((EW_ROWS, LANES), lambda i: (i, 0))
    blk = pl.BlockSpec((EW_ROWS, ncols), lambda i: (i, 0))
    return pl.pallas_call(
        functools.partial(_head_prep_kernel, scale=scale),
        grid=(t // EW_ROWS,),
        in_specs=[blk, pl.BlockSpec((1, LANES), lambda i: (0, 0)), tab, tab, tab],
        out_specs=blk,
        out_shape=jax.ShapeDtypeStruct((t, ncols), BF16),
        compiler_params=_params("parallel"),
        name=name,
    )(p, gain.reshape(1, LANES), cc, se, so)


def _merge_kernel(ya_ref, yb_ref, yc_ref, wa_ref, wb_ref, wc_ref, ga_ref, gb_ref, gc_ref, o_ref,
                  wab, wbb, wcb):
    def body(ws):
        m = ga_ref[...].astype(F32) * jnp.dot(ya_ref[...], ws[0], preferred_element_type=F32)
        m += gb_ref[...].astype(F32) * jnp.dot(yb_ref[...], ws[1], preferred_element_type=F32)
        m += gc_ref[...].astype(F32) * jnp.dot(yc_ref[...], ws[2], preferred_element_type=F32)
        o_ref[...] = m.astype(o_ref.dtype)

    _with_bf16_weights((wa_ref, wb_ref, wc_ref), (wab, wbb, wcb), body)


def _merge(ya, yb, yc, wa, wb, wc, layer, gates):
    t, w = ya.shape
    d = wa.shape[-1]
    nj = d // COL_TILE
    ni = t // ROW_TILE
    y_spec = pl.BlockSpec((ROW_TILE, w), lambda j, i: (_serpentine(j, i, ni), 0))
    w_spec = pl.BlockSpec((1, w, COL_TILE), lambda j, i: (layer, 0, j))

    def g_spec(k):
        return pl.BlockSpec((ROW_TILE, COL_TILE), lambda j, i: (_serpentine(j, i, ni), j + k * nj))

    return pl.pallas_call(
        _merge_kernel,
        grid=(nj, ni),
        in_specs=[y_spec, y_spec, y_spec, w_spec, w_spec, w_spec, g_spec(0), g_spec(1), g_spec(2)],
        out_specs=pl.BlockSpec((ROW_TILE, COL_TILE), lambda j, i: (_serpentine(j, i, ni), j)),
        out_shape=jax.ShapeDtypeStruct((t, d), BF16),
        scratch_shapes=[pltpu.VMEM((w, COL_TILE), BF16)] * 3,
        compiler_params=_params("parallel", "arbitrary"),
        name="merge",
    )(ya, yb, yc, wa, wb, wc, gates, gates, gates)


def _lru_kernel(pa_ref, pg_ref, cw_ref, wa_ref, ba_ref, wx_ref, bx_ref, lam_ref, h0_ref,
                ya_ref, hT_ref, work, a_sc, b_sc, *, ts, chunk):
    xs = work.at[0]
    pad = SUBLANES
    win = chunk + 2 * pad
    n_chunks = ts // chunk
    zeros = jnp.zeros((pad, LANES), F32)
    xs[pl.ds(0, pad), :] = zeros
    xs[pl.ds(pad + ts, pad), :] = zeros

    def copy_in(c, carry):
        t0 = pl.multiple_of(c * chunk, chunk)
        xs[pl.ds(pad + t0, chunk), :] = pa_ref[pl.ds(t0, chunk), :]
        return carry

    lax.fori_loop(0, n_chunks, copy_in, 0)

    sp = [jax.nn.softplus(-lam_ref[d:d + 1, :]) for d in range(2)]

    def gates(c, carry):
        t0 = pl.multiple_of(c * chunk, chunk)
        xw = xs[pl.ds(t0, win), :]
        xa = None
        for k in range(CONV_A):
            sh = pltpu.roll(xw, (win + 1 - k) % win, 0) if k != 1 else xw
            term = sh[pad:pad + chunk, :] * cw_ref[k:k + 1, :]
            xa = term if xa is None else xa + term
        xb = xa.astype(BF16)
        for d in range(2):
            r = 0.5 * jnp.tanh(0.5 * (_dot(xb, wa_ref[d, 0]) + ba_ref[d:d + 1, :])) + 0.5
            i = 0.5 * jnp.tanh(0.5 * (_dot(xb, wx_ref[d, 0]) + bx_ref[d:d + 1, :])) + 0.5
            log_a = -LRU_C * r * sp[d]
            a = jnp.exp(log_a)
            b = jnp.sqrt(1.0 - a * a) * (i * xa)
            a_sc[d, pl.ds(t0, chunk), :] = a
            b_sc[d, pl.ds(t0, chunk), :] = b
        return carry

    lax.fori_loop(0, n_chunks, gates, 0)

    row = lax.broadcasted_iota(jnp.int32, (SUBLANES, LANES), 0)
    steps = (1, 2, 4)

    def scan(j, carry):
        cf, cb = carry
        tf = pl.multiple_of(j * SUBLANES, SUBLANES)
        tb = pl.multiple_of(ts - (j + 1) * SUBLANES, SUBLANES)
        af = a_sc[0, pl.ds(tf, SUBLANES), :]
        bf = b_sc[0, pl.ds(tf, SUBLANES), :]
        ab = a_sc[1, pl.ds(tb, SUBLANES), :]
        bb = b_sc[1, pl.ds(tb, SUBLANES), :]
        for s in steps:
            mf = row >= s
            bf = bf + af * jnp.where(mf, pltpu.roll(bf, s, 0), 0.0)
            af = af * jnp.where(mf, pltpu.roll(af, s, 0), 1.0)
            mb = row < SUBLANES - s
            bb = bb + ab * jnp.where(mb, pltpu.roll(bb, SUBLANES - s, 0), 0.0)
            ab = ab * jnp.where(mb, pltpu.roll(ab, SUBLANES - s, 0), 1.0)
        hf = bf + af * cf
        hb = bb + ab * cb
        work[0, pl.ds(tf, SUBLANES), :] = hf
        work[1, pl.ds(tb, SUBLANES), :] = hb
        cf = jnp.broadcast_to(hf[SUBLANES - 1:SUBLANES, :], (SUBLANES, LANES))
        cb = jnp.broadcast_to(hb[0:1, :], (SUBLANES, LANES))
        return cf, cb

    c0 = (jnp.broadcast_to(h0_ref[0:1, :], (SUBLANES, LANES)),
          jnp.broadcast_to(h0_ref[1:2, :], (SUBLANES, LANES)))
    cf, cb = lax.fori_loop(0, ts // SUBLANES, scan, c0, unroll=4)
    hT_ref[0:1, :] = cf[0:1, :]
    hT_ref[1:2, :] = cb[0:1, :]

    def finish(c, carry):
        t0 = pl.multiple_of(c * chunk, chunk)
        h = work[0, pl.ds(t0, chunk), :] + work[1, pl.ds(t0, chunk), :]
        g = jax.nn.gelu(pg_ref[pl.ds(t0, chunk), :].astype(F32), approximate=True)
        ya_ref[pl.ds(t0, chunk), :] = (h * g).astype(ya_ref.dtype)
        return carry

    lax.fori_loop(0, n_chunks, finish, 0)


def _lru(p_ax, p_ag, row0, ts, col_ax, col_ag, cw, wa, ba, wx, bx, lam, h0):
    w = cw.shape[-1]
    nblk = w // LANES
    rb = row0 // ts
    cax = col_ax // LANES
    cag = col_ag // LANES
    chunk = min(EW_ROWS, ts)
    kern = functools.partial(_lru_kernel, ts=ts, chunk=chunk)
    vec = pl.BlockSpec((2, LANES), lambda j: (0, j))
    mat = pl.BlockSpec((2, 1, LANES, LANES), lambda j: (0, j, 0, 0))
    return pl.pallas_call(
        kern,
        grid=(nblk,),
        in_specs=[
            pl.BlockSpec((ts, LANES), lambda j: (rb, cax + j)),
            pl.BlockSpec((ts, LANES), lambda j: (rb, cag + j)),
            pl.BlockSpec((CONV_A, LANES), lambda j: (0, j)),
            mat, vec, mat, vec, vec, vec,
        ],
        out_specs=[pl.BlockSpec((ts, LANES), lambda j: (0, j)),
                   pl.BlockSpec((2, LANES), lambda j: (0, j))],
        out_shape=[jax.ShapeDtypeStruct((ts, w), BF16),
                   jax.ShapeDtypeStruct((2, w), F32)],
        scratch_shapes=[pltpu.VMEM((2, ts + 2 * SUBLANES, LANES), F32),
                        pltpu.VMEM((2, ts, LANES), F32),
                        pltpu.VMEM((2, ts, LANES), F32)],
        compiler_params=_params("parallel"),
        name="rglru",
    )(p_ax, p_ag, cw, wa, ba, wx, bx, lam, h0)


def _conv3_kernel(x_ref, xp_ref, xn_ref, w_ref, o_ref):
    i = pl.program_id(0)
    first = i == 0
    last = i == pl.num_programs(0) - 1
    x = x_ref[...].astype(F32)
    r = x.shape[0]
    row = lax.broadcasted_iota(jnp.int32, (r, 1), 0)
    halo = xp_ref.shape[0]
    prev_row = jnp.where(first, 0.0, xp_ref[...].astype(F32)[halo - 1:halo, :])
    next_row = jnp.where(last, 0.0, xn_ref[...].astype(F32)[0:1, :])
    xm1 = jnp.where(row == 0, prev_row, pltpu.roll(x, 1, 0))
    xp1 = jnp.where(row == r - 1, next_row, pltpu.roll(x, r - 1, 0))
    o_ref[0] = w_ref[0:1, :] * xm1 + w_ref[1:2, :] * x + w_ref[2:3, :] * xp1


def _conv3(p_b, row0, ts, col_b, w3, wmix):
    r = min(2 * EW_ROWS, ts)
    rb = row0 // r
    halo = 2 * SUBLANES
    hb = r // halo
    cb = col_b // wmix
    n_r = ts // r
    return pl.pallas_call(
        _conv3_kernel,
        grid=(n_r, 3),
        in_specs=[
            pl.BlockSpec((r, wmix), lambda i, j: (rb + i, cb + j)),
            pl.BlockSpec((halo, wmix), lambda i, j: (jnp.maximum((rb + i) * hb - 1, 0), cb + j)),
            pl.BlockSpec((halo, wmix),
                         lambda i, j: (jnp.minimum((rb + i + 1) * hb, (rb + n_r) * hb - 1), cb + j)),
            pl.BlockSpec((CONV_B, wmix), lambda i, j: (0, j)),
        ],
        out_specs=pl.BlockSpec((1, r, wmix), lambda i, j: (j, i, 0)),
        out_shape=jax.ShapeDtypeStruct((3, ts, wmix), F32),
        compiler_params=_params("arbitrary", "arbitrary"),
        name="hyena_conv3",
    )(p_b, p_b, p_b, w3)


def _filter_kernel(z_ref, tl_ref, w1_ref, b1_ref, fr_ref, w2_ref, b2_ref, w3_ref, w3b_ref, ad_ref,
                   k_ref, ssq_ref, *, half_tiles):
    i = pl.program_id(0)
    hi = lax.Precision.HIGHEST
    fr = fr_ref[...]
    h = jnp.sin(fr * (jnp.dot(z_ref[...], w1_ref[...], precision=hi, preferred_element_type=F32)
                      + b1_ref[...]))
    h = jnp.sin(fr * (jnp.dot(h, w2_ref[...], precision=hi, preferred_element_type=F32) + b2_ref[...]))
    decay = jnp.exp(-tl_ref[...] * ad_ref[...])
    taps = _dot(h, w3_ref[0]) * decay
    r = taps.shape[0]
    row = lax.broadcasted_iota(jnp.int32, (r, 1), 0)
    k_ref[...] = taps

    @pl.when(i == 0)
    def _():
        back = _dot(h, w3b_ref[0]) * decay
        k_ref[...] = taps + jnp.where(row == 0, back, 0.0)
        ssq_ref[...] = jnp.zeros_like(ssq_ref)

    @pl.when(i == half_tiles)
    def _():
        k_ref[...] = jnp.where(row == 0, 0.0, taps)

    kk = k_ref[...]
    ssq_ref[...] += jnp.sum(kk * kk, axis=0, keepdims=True)


def _hyena_filter(seq, p):
    wmix = p["hy_skip"].shape[-1]
    hidden = p["hy_fw1"].shape[-1]
    r = min(2 * EW_ROWS, seq)
    t_idx = jnp.arange(seq, dtype=F32)
    t_lin = t_idx / max(seq - 1, 1)
    bands = jnp.linspace(1e-4, HYENA_BANDS - 1, HYENA_BANDS, dtype=F32)
    ang = (2.0 * math.pi / seq) * t_idx[:, None] * bands[None, :]
    z = jnp.concatenate([t_lin[:, None], jnp.cos(ang), -jnp.sin(ang)], axis=-1)
    rev = lambda a: jnp.concatenate([a[:1], jnp.flip(a[1:], axis=0)], axis=0)
    zz = jnp.concatenate([z, rev(z)], axis=0)
    zz = jnp.pad(zz, ((0, 0), (0, LANES - HYENA_EMB)))
    tl = jnp.concatenate([t_lin, rev(t_lin)])[:, None]
    w1 = jnp.pad(p["hy_fw1"], ((0, LANES - HYENA_EMB), (0, 0)))
    w3 = p["hy_fw3"].reshape(hidden, HYENA_ORDER, 2, wmix).transpose(2, 0, 1, 3)
    w3 = w3.reshape(2, hidden, HYENA_ORDER * wmix)
    deltas = jnp.linspace(math.log(HYENA_TARGET) / HYENA_SLOW_DECAY,
                          math.log(HYENA_TARGET) / HYENA_FAST_DECAY, wmix, dtype=F32)
    ad = jnp.tile(jnp.abs(deltas), HYENA_ORDER)[None, :]
    ow = HYENA_ORDER * wmix
    half = seq // r
    full = lambda shape: pl.BlockSpec(shape, lambda i: tuple(0 for _ in shape))
    return pl.pallas_call(
        functools.partial(_filter_kernel, half_tiles=half),
        grid=(2 * half,),
        in_specs=[
            pl.BlockSpec((r, LANES), lambda i: (i, 0)),
            pl.BlockSpec((r, 1), lambda i: (i, 0)),
            full((LANES, hidden)), full((1, hidden)), full((1, hidden)),
            full((hidden, hidden)), full((1, hidden)),
            pl.BlockSpec((1, hidden, ow), lambda i: (jnp.where(i >= half, 1, 0), 0, 0)),
            pl.BlockSpec((1, hidden, ow), lambda i: (1, 0, 0)),
            full((1, ow)),
        ],
        out_specs=[pl.BlockSpec((r, ow), lambda i: (i, 0)),
                   pl.BlockSpec((1, ow), lambda i: (0, 0))],
        out_shape=[jax.ShapeDtypeStruct((2 * seq, ow), F32),
                   jax.ShapeDtypeStruct((1, ow), F32)],
        compiler_params=_params("arbitrary"),
        name="hyena_filter",
    )(zz, tl, w1, p["hy_fb1"][None, :], p["hy_freq"][None, :], p["hy_fw2"], p["hy_fb2"][None, :],
      w3, w3, ad)


def _kron_fwd_kernel(f_ref, x_ref, o_ref):
    f = f_ref[...]
    halves = []
    for h in range(2):
        xh = x_ref[0, :, h * SUBLANES:(h + 1) * SUBLANES, :]
        xh = xh.reshape(xh.shape[0] * SUBLANES, xh.shape[2]).astype(BF16)
        r = jnp.dot(f, xh, preferred_element_type=F32)
        halves.append(r.reshape(r.shape[0] // SUBLANES, SUBLANES, r.shape[1]))
    o_ref[...] = jnp.concatenate(halves, axis=1).astype(o_ref.dtype)


def _kron_fwd(fk, x4, sel, name):
    _, nt1, nt2, w = x4.shape
    rows = fk.shape[0] // SUBLANES
    tw = 512 if w % 512 == 0 else LANES
    rt = 2 * SUBLANES
    return pl.pallas_call(
        _kron_fwd_kernel,
        grid=(nt2 // rt, w // tw),
        in_specs=[pl.BlockSpec(fk.shape, lambda i, j: (0, 0)),
                  pl.BlockSpec((1, nt1, rt, tw), lambda i, j: (sel, 0, i, j))],
        out_specs=pl.BlockSpec((rows, rt, tw), lambda i, j: (0, i, j)),
        out_shape=jax.ShapeDtypeStruct((rows, nt2, w), BF16),
        compiler_params=_params("parallel", "parallel"),
        name=name,
    )(fk, x4)


def _kron_inv_gate_kernel(g_ref, b_ref, x_ref, v_ref, sk_ref, o_ref):
    g = g_ref[...]
    b = b_ref[...].astype(F32)
    sk = sk_ref[...]
    halves = []
    for h in range(2):
        lo, hi = h * SUBLANES, (h + 1) * SUBLANES
        bh = b[:, lo:hi, :]
        bh = bh.reshape(bh.shape[0] * SUBLANES, bh.shape[2]).astype(BF16)
        y = jnp.dot(g, bh, preferred_element_type=F32)
        y = y.reshape(y.shape[0] // SUBLANES, SUBLANES, y.shape[1])
        halves.append(x_ref[0, :, lo:hi, :] * (y + sk * v_ref[0, :, lo:hi, :]))
    o_ref[0] = jnp.concatenate(halves, axis=1).astype(o_ref.dtype)


def _kron_inv_gate(gk, b3, x4, x_sel, v4, v_sel, sk, out_dtype, name):
    _, nt2, w = b3.shape
    nt1 = gk.shape[0] // SUBLANES
    tw = 512 if w % 512 == 0 else LANES
    rt = 2 * SUBLANES
    return pl.pallas_call(
        _kron_inv_gate_kernel,
        grid=(nt2 // rt, w // tw),
        in_specs=[pl.BlockSpec(gk.shape, lambda i, j: (0, 0)),
                  pl.BlockSpec((b3.shape[0], rt, tw), lambda i, j: (0, i, j)),
                  pl.BlockSpec((1, nt1, rt, tw), lambda i, j: (x_sel, 0, i, j)),
                  pl.BlockSpec((1, nt1, rt, tw), lambda i, j: (v_sel, 0, i, j)),
                  pl.BlockSpec((1, 1, tw), lambda i, j: (0, 0, j))],
        out_specs=pl.BlockSpec((1, nt1, rt, tw), lambda i, j: (0, 0, i, j)),
        out_shape=jax.ShapeDtypeStruct((1, nt1, nt2, w), out_dtype),
        compiler_params=_params("parallel", "parallel"),
        name=name,
    )(gk, b3, x4, v4, sk.reshape(1, 1, w))


def _bmm_scale_kernel(m_ref, a_ref, s_ref, o_ref, *, bpb):
    rows_in = a_ref.shape[0] // bpb
    rows_out = o_ref.shape[0] // bpb
    for b in range(bpb):
        y = _dot(m_ref[b], a_ref[b * rows_in:(b + 1) * rows_in, :]) * s_ref[...]
        o_ref[b * rows_out:(b + 1) * rows_out, :] = y.astype(o_ref.dtype)


def _bmm_scale(m2, a, scale, rows_in, name):
    nb, rows_out, _ = m2.shape
    n = a.shape[1]
    tn = n // 2 if (n // 2) % LANES == 0 else n
    bpb = 4 if nb % 4 == 0 else 1
    return pl.pallas_call(
        functools.partial(_bmm_scale_kernel, bpb=bpb),
        grid=(nb // bpb, n // tn),
        in_specs=[pl.BlockSpec((bpb, rows_out, rows_in), lambda b, j: (b, 0, 0)),
                  pl.BlockSpec((bpb * rows_in, tn), lambda b, j: (b, j)),
                  pl.BlockSpec((1, tn), lambda b, j: (0, j))],
        out_specs=pl.BlockSpec((bpb * rows_out, tn), lambda b, j: (b, j)),
        out_shape=jax.ShapeDtypeStruct((nb * rows_out, n), BF16),
        compiler_params=_params("parallel", "arbitrary"),
        name=name,
    )(m2, a, scale)


def _spectral_one(m2, a, k, m3):
    x = _dot(m2, a)
    f = x.shape[0] // 2
    xr, xi = x[:f], x[f:]
    kr, ki = k[:f].astype(F32), k[f:].astype(F32)
    y = jnp.concatenate([xr * kr - xi * ki, xr * ki + xi * kr], axis=0)
    return _dot(m3, y)


def _spectral_kernel(m2_ref, a_ref, k_ref, m3_ref, o_ref, *, bpb):
    rows_in = a_ref.shape[0] // bpb
    f2 = k_ref.shape[0] // bpb
    rows_out = o_ref.shape[0] // bpb
    for b in range(bpb):
        y = _spectral_one(m2_ref[b], a_ref[b * rows_in:(b + 1) * rows_in, :],
                          k_ref[b * f2:(b + 1) * f2, :], m3_ref[b])
        o_ref[b * rows_out:(b + 1) * rows_out, :] = y.astype(o_ref.dtype)


def _spectral_gate_kernel(m2_ref, a_ref, k_ref, m3_ref, x_ref, v_ref, sk_ref, o_ref):
    y = _spectral_one(m2_ref[0], a_ref[...], k_ref[...], m3_ref[0])
    o_ref[...] = (x_ref[...] * (y + sk_ref[...] * v_ref[...])).astype(o_ref.dtype)


def _spectral(m2, a, kspec, order, m3, out_dtype, gate=None, name="hyena_spectral"):
    nb, f2, rows_in = m2.shape
    rows_out = m3.shape[1]
    wmix = a.shape[1]
    bpb = 4 if (nb % 4 == 0 and gate is None) else 1
    in_specs = [pl.BlockSpec((bpb, f2, rows_in), lambda b: (b, 0, 0)),
                pl.BlockSpec((bpb * rows_in, wmix), lambda b: (b, 0)),
                pl.BlockSpec((bpb * f2, wmix), lambda b: (b, order)),
                pl.BlockSpec((bpb, rows_out, f2), lambda b: (b, 0, 0))]
    args = [m2, a, kspec, m3]
    kern = functools.partial(_spectral_kernel, bpb=bpb)
    if gate is not None:
        x, v, sk = gate
        blk = pl.BlockSpec((rows_out, wmix), lambda b: (b, 0))
        in_specs += [blk, blk, pl.BlockSpec((1, wmix), lambda b: (0, 0))]
        args += [x, v, sk]
        kern = _spectral_gate_kernel
    return pl.pallas_call(
        kern,
        grid=(nb // bpb,),
        in_specs=in_specs,
        out_specs=pl.BlockSpec((bpb * rows_out, wmix), lambda b: (b, 0)),
        out_shape=jax.ShapeDtypeStruct((nb * rows_out, wmix), out_dtype),
        compiler_params=_params("parallel"),
        name=name,
    )(*args)


def _dft_tables(seq):
    n = 2 * seq
    n2 = DFT_INNER
    n1 = n // n2
    i1 = jnp.arange(n1, dtype=jnp.int32)
    i2 = jnp.arange(n2, dtype=jnp.int32)
    ang1 = (2.0 * math.pi / n1) * ((i1[:, None] * i1[None, :]) % n1).astype(F32)
    c1, s1 = jnp.cos(ang1), jnp.sin(ang1)
    f1 = jnp.stack([c1, -s1], axis=1).reshape(2 * n1, n1)
    q = i1[:, None, None] + n1 * i2[None, :, None]
    ang = (2.0 * math.pi / n) * ((q * i2[None, None, :]) % n).astype(F32)
    tr, ti = jnp.cos(ang), -jnp.sin(ang)
    m2 = jnp.concatenate([jnp.concatenate([tr, -ti], axis=2),
                          jnp.concatenate([ti, tr], axis=2)], axis=1)
    trt, tit = jnp.swapaxes(tr, 1, 2), jnp.swapaxes(ti, 1, 2)
    m3 = jnp.concatenate([jnp.concatenate([trt, tit], axis=2),
                          jnp.concatenate([-tit, trt], axis=2)], axis=1)
    g = jnp.stack([c1, -s1], axis=2).reshape(n1, 2 * n1)[: n1 // 2] / n
    eye = jnp.eye(SUBLANES, dtype=F32)
    kron = lambda m: jnp.kron(m, eye).astype(BF16)
    return kron(f1[:, : n1 // 2]), kron(f1), m2.astype(BF16), m3.astype(BF16), kron(g)


def _direct_dft_tables(seq):
    n = 2 * seq
    i = jnp.arange(n, dtype=jnp.int32)
    ang = (2.0 * math.pi / n) * ((i[:, None] * i[None, :]) % n).astype(F32)
    c, s = jnp.cos(ang), jnp.sin(ang)
    fwd = jnp.concatenate([c, -s], axis=0)
    inv = jnp.concatenate([c[:seq], -s[:seq]], axis=1) / n
    return fwd[:, :seq].astype(BF16)[None], fwd.astype(BF16)[None], inv.astype(BF16)[None]


def _hyena_long(u3, taps, ssq, skip, tables):
    fk_half, fk_full, m2, m3, gk = tables
    _, seq, wmix = u3.shape
    n2 = DFT_INNER
    n1 = 2 * seq // n2
    ow = taps.shape[1]
    scale = lax.rsqrt(ssq + EPS)
    ak = _kron_fwd(fk_full, taps.reshape(1, n1, n2, ow), 0, "hyena_filter_dft1")
    kspec = _bmm_scale(m2, ak.reshape(n1 * 2 * n2, ow), scale, 2 * n2, "hyena_filter_dft2")
    u4 = u3.reshape(3, n1 // 2, n2, wmix)
    z4, z_sel = u4, 0
    for o in range(HYENA_ORDER):
        a = _kron_fwd(fk_half, z4, z_sel, "hyena_dft1")
        b = _spectral(m2, a.reshape(n1 * 2 * n2, wmix), kspec, o, m3, BF16)
        last = o == HYENA_ORDER - 1
        z4 = _kron_inv_gate(gk, b.reshape(2 * n1, n2, wmix), u4, 1 + o, z4, z_sel, skip[o],
                            BF16 if last else F32, "hyena_dft4_gate")
        z_sel = 0
    return z4.reshape(seq, wmix)


def _hyena_short(u3, taps, ssq, skip, tables):
    fwd_half, fwd_full, inv = tables
    scale = lax.rsqrt(ssq + EPS)
    kspec = _bmm_scale(fwd_full, taps, scale, taps.shape[0], "hyena_ctx_filter_dft")
    z = u3[0]
    for o in range(HYENA_ORDER):
        last = o == HYENA_ORDER - 1
        z = _spectral(fwd_half, z, kspec, o, inv, BF16 if last else F32,
                      gate=(u3[1 + o], z, skip[o][None, :]), name="hyena_ctx_spectral")
    return z


def _attn_kernel(q_ref, k_ref, v_ref, o_ref, m_sc, acc_sc, s_sc, p_sc, a_sc, *, rows):
    j = pl.program_id(2)
    _, tq, tk = s_sc.shape
    nlb = tk // LANES

    @pl.when(j == 0)
    def _():
        m_sc[...] = jnp.full_like(m_sc, -jnp.inf)
        acc_sc[...] = jnp.zeros_like(acc_sc)

    k = k_ref[...]
    v = v_ref[...]
    v1 = jnp.concatenate([v, jnp.ones_like(v)], axis=1)
    for g in range(GQA_GROUP):
        q = q_ref[:, g * HEAD_DIM:(g + 1) * HEAD_DIM]
        s_sc[g] = lax.dot_general(q, k, (((1,), (1,)), ((), ())), preferred_element_type=F32)

    for g in range(GQA_GROUP):
        for c in range(tq // rows):
            rs = slice(c * rows, (c + 1) * rows)
            blocks = [s_sc[g, rs, b * LANES:(b + 1) * LANES] for b in range(nlb)]
            bmax = blocks[0]
            for blk in blocks[1:]:
                bmax = jnp.maximum(bmax, blk)
            m_prev = m_sc[g, rs, :]
            m_new = jnp.maximum(m_prev, jnp.max(bmax, axis=1, keepdims=True))
            for b, blk in enumerate(blocks):
                p_sc[g, rs, b * LANES:(b + 1) * LANES] = jnp.exp2(blk - m_new).astype(BF16)
            m_sc[g, rs, :] = m_new
            a_sc[g, rs, :] = jnp.exp2(m_prev - m_new)

    for g in range(GQA_GROUP):
        alpha = jnp.concatenate([a_sc[g], a_sc[g]], axis=1)
        acc_sc[g] = alpha * acc_sc[g] + jnp.dot(p_sc[g], v1, preferred_element_type=F32)

    @pl.when(j == pl.num_programs(2) - 1)
    def _():
        for g in range(GQA_GROUP):
            acc = acc_sc[g]
            o_ref[:, g * HEAD_DIM:(g + 1) * HEAD_DIM] = (
                acc[:, :HEAD_DIM] / acc[:, HEAD_DIM:]).astype(o_ref.dtype)


def _attention(q, k, v, q_row0, n_q, k_row0, n_k, tq, tk):
    n_kv = k.shape[1] // HEAD_DIM
    gw = GQA_GROUP * HEAD_DIM
    qb, kb = q_row0 // tq, k_row0 // tk
    return pl.pallas_call(
        functools.partial(_attn_kernel, rows=2 * SUBLANES),
        grid=(n_kv, n_q // tq, n_k // tk),
        in_specs=[pl.BlockSpec((tq, gw), lambda h, i, j: (qb + i, h)),
                  pl.BlockSpec((tk, HEAD_DIM), lambda h, i, j: (kb + j, h)),
                  pl.BlockSpec((tk, HEAD_DIM), lambda h, i, j: (kb + j, h))],
        out_specs=pl.BlockSpec((tq, gw), lambda h, i, j: (i, h)),
        out_shape=jax.ShapeDtypeStruct((n_q, q.shape[1]), BF16),
        scratch_shapes=[pltpu.VMEM((GQA_GROUP, tq, LANES), F32),
                        pltpu.VMEM((GQA_GROUP, tq, 2 * HEAD_DIM), F32),
                        pltpu.VMEM((GQA_GROUP, tq, tk), F32),
                        pltpu.VMEM((GQA_GROUP, tq, tk), BF16),
                        pltpu.VMEM((GQA_GROUP, tq, LANES), F32)],
        compiler_params=_params("parallel", "parallel", "arbitrary"),
        name="attention",
    )(q, k, v)


def _rope_tables(seq, n_ctx):
    rows = seq // GRID_W
    row = jnp.repeat(jnp.arange(rows, dtype=F32), GRID_W)
    col = jnp.tile(jnp.arange(GRID_W, dtype=F32), rows)
    n_pairs = HEAD_DIM // 4
    inv = ROPE_THETA ** (-jnp.arange(n_pairs, dtype=F32) / n_pairs)
    ang = jnp.concatenate([row[:, None] * inv, col[:, None] * inv], axis=-1)
    ang = jnp.concatenate([ang, jnp.zeros((n_ctx, HEAD_DIM // 2), F32)], axis=0)
    c = jnp.repeat(jnp.cos(ang), 2, axis=1)
    s = jnp.repeat(jnp.sin(ang), 2, axis=1)
    even = (jnp.arange(HEAD_DIM) % 2 == 0)[None, :]
    return c, jnp.where(even, -s, 0.0), jnp.where(even, 0.0, s)


def kernel(x, c, ctx, c_ctx, w_mod_down, w_mod_up, b_mod, norm_ffn1, norm_mix, norm_ffn2,
           ffn1_w_in, ffn1_w_out, ffn2_w_in, ffn2_w_out, w_in, lru_conv, lru_w_a, lru_b_a,
           lru_w_x, lru_b_x, lru_lambda, hy_conv, hy_fw1, hy_fb1, hy_freq, hy_fw2, hy_fb2,
           hy_fw3, hy_skip, q_norm, k_norm, w_branch_a, w_branch_b, w_branch_c, w_out, final_norm):
    bsz, seq, d = x.shape
    assert bsz == 1 and c.shape[0] == 1 and ctx.shape[0] == 1
    n_ctx = ctx.shape[1]
    depth = w_in.shape[0]
    wmix = lru_conv.shape[-1]
    kvw = wmix // GQA_GROUP
    t_all = seq + n_ctx
    assert t_all % ROW_TILE == 0 and seq % EW_ROWS == 0 and n_ctx % EW_ROWS == 0

    col_ax = 0
    col_ck = col_ax + wmix
    col_cv = col_ck + kvw
    col_ag = col_cv + kvw
    col_b = col_ag + wmix
    col_cq = col_b + 3 * wmix
    col_g = col_cq + wmix

    xs = jnp.concatenate([x[0], ctx[0]], axis=0)
    cc = jnp.zeros((SUBLANES, d), F32).at[0].set(c[0]).at[1].set(c_ctx)
    mods_all = _modulation(cc, w_mod_down, w_mod_up, b_mod)

    ffn1_in, ffn1_out, ffn2_in, ffn2_out = ffn1_w_in, ffn1_w_out, ffn2_w_in, ffn2_w_out
    w_in_b, wba, wbb, wbc, w_out_b = w_in, w_branch_a, w_branch_b, w_branch_c, w_out

    rope = _rope_tables(seq, n_ctx)
    dft_lat = _dft_tables(seq)
    dft_ctx = _direct_dft_tables(n_ctx)
    q_scale = HEAD_DIM ** -0.5 * math.log2(math.e)
    tq_lat = next(t for t in (512, EW_ROWS) if seq % t == 0)
    tk_lat = next(t for t in (2816, 1408, ROW_TILE) if t_all % t == 0)

    for i in range(depth):
        ctx_out = i < depth - 1
        mods = mods_all[i]
        mods3 = mods.reshape(2 * N_MOD, 1, d)

        u = _norm_mod(xs, norm_ffn1[i], mods3, 0, seq)
        h = _ffn_up(u, ffn1_in, i)
        xs = _down(h, ffn1_out, i, xs, mods[:, 2], 0.5, seq)

        u = _norm_mod(xs, norm_mix[i], mods3, 3, seq)
        p_ax = _proj(u, w_in_b, i, col_ax, wmix, False, F32, "mixer_in_lru")
        p_b = _proj(u, w_in_b, i, col_ag, 4 * wmix, False, BF16, "mixer_in_gelu_hyena")
        kh = _head_prep(_proj(u, w_in_b, i, col_ck, kvw, False, F32, "mixer_in_k"),
                        k_norm[i], rope, 1.0, "k_prep")
        vh = _proj(u, w_in_b, i, col_cv, kvw, False, BF16, "mixer_in_v")
        qh = _head_prep(_proj(u, w_in_b, i, col_cq, wmix, False, F32, "mixer_in_q"),
                        q_norm[i], rope, q_scale, "q_prep")
        gates = _proj(u, w_in_b, i, col_g, 3 * d, True, BF16, "mixer_gates")

        lru_args = (lru_conv[i], lru_w_a[i], lru_b_a[i], lru_w_x[i], lru_b_x[i], lru_lambda[i])
        ya_c, h_c = _lru(p_ax, p_b, seq, n_ctx, 0, 0, *lru_args, jnp.zeros((2, wmix), F32))
        ya_l, _ = _lru(p_ax, p_b, 0, seq, 0, 0, *lru_args, h_c)

        hp = {"hy_fw1": hy_fw1[i], "hy_fb1": hy_fb1[i], "hy_freq": hy_freq[i], "hy_fw2": hy_fw2[i],
              "hy_fb2": hy_fb2[i], "hy_fw3": hy_fw3[i], "hy_skip": hy_skip[i]}
        taps_l, ssq_l = _hyena_filter(seq, hp)
        u3_l = _conv3(p_b, 0, seq, wmix, hy_conv[i], wmix)
        yb_l = _hyena_long(u3_l, taps_l, ssq_l, hy_skip[i], dft_lat)

        yc_l = _attention(qh, kh, vh, 0, seq, 0, t_all, tq_lat, tk_lat)

        if ctx_out:
            taps_c, ssq_c = _hyena_filter(n_ctx, hp)
            u3_c = _conv3(p_b, seq, n_ctx, wmix, hy_conv[i], wmix)
            yb_c = _hyena_short(u3_c, taps_c, ssq_c, hy_skip[i], dft_ctx)
            yc_c = _attention(qh, kh, vh, seq, n_ctx, seq, n_ctx, n_ctx, n_ctx)
        else:
            yb_c = jnp.zeros((n_ctx, wmix), BF16)
            yc_c = jnp.zeros((n_ctx, wmix), BF16)

        ya = jnp.concatenate([ya_l, ya_c], axis=0)
        yb = jnp.concatenate([yb_l, yb_c], axis=0)
        yc = jnp.concatenate([yc_l, yc_c], axis=0)
        m = _merge(ya, yb, yc, wba, wbb, wbc, i, gates)
        xs = _down(m, w_out_b, i, xs, mods[:, 5], 1.0, seq)

        u = _norm_mod(xs, norm_ffn2[i], mods3, 6, seq)
        h = _ffn_up(u, ffn2_in, i)
        xs = _down(h, ffn2_out, i, xs, mods[:, 8], 0.5, seq)

    return _final_norm(xs, final_norm, seq)[None]
```

```python
import functools
import math

import jax
import jax.numpy as jnp
from jax import lax
from jax.experimental import pallas as pl
from jax.experimental.pallas import tpu as pltpu

F32 = jnp.float32
BF16 = jnp.bfloat16

HEAD_DIM = 128
LANES = 128
SUBLANES = 8
GQA_GROUP = 3
GRID_W = 64
ROPE_THETA = 10000.0
LRU_C = 8.0
CONV_A = 4
CONV_B = 3
HYENA_ORDER = 2
HYENA_BANDS = 16
HYENA_EMB = 2 * HYENA_BANDS + 1
HYENA_FAST_DECAY = 0.3
HYENA_SLOW_DECAY = 1.5
HYENA_TARGET = 1e-2
N_MOD = 9
EPS = 1e-6
DFT_INNER = 128
VMEM_LIMIT = 56 * 1024 * 1024

ROW_TILE = 768
COL_TILE = 512
EW_ROWS = 256


def _params(*sem):
    return pltpu.CompilerParams(dimension_semantics=sem, vmem_limit_bytes=VMEM_LIMIT)


def _dot(a, b):
    return jnp.dot(a.astype(BF16), b.astype(BF16), preferred_element_type=F32)


def _mod_kernel(c_ref, wd_ref, wu_ref, b_ref, o_ref):
    c = c_ref[...]
    s = c * jax.nn.sigmoid(c)
    t = _dot(s, wd_ref[0])
    o_ref[0] = _dot(t, wu_ref[0]) + b_ref[0]


def _modulation(cc, w_down, w_up, b_mod):
    depth, d, rank = w_down.shape
    out = pl.pallas_call(
        _mod_kernel,
        grid=(depth, N_MOD),
        in_specs=[
            pl.BlockSpec((SUBLANES, d), lambda l, j: (0, 0)),
            pl.BlockSpec((1, d, rank), lambda l, j: (l, 0, 0)),
            pl.BlockSpec((1, rank, d), lambda l, j: (l, 0, j)),
            pl.BlockSpec((1, 1, d), lambda l, j: (l, 0, j)),
        ],
        out_specs=pl.BlockSpec((1, SUBLANES, d), lambda l, j: (l, 0, j)),
        out_shape=jax.ShapeDtypeStruct((depth, SUBLANES, N_MOD * d), F32),
        compiler_params=_params("arbitrary", "arbitrary"),
        name="modulation",
    )(cc, w_down, w_up, b_mod.reshape(depth, 1, N_MOD * d))
    return out.reshape(depth, SUBLANES, N_MOD, d)[:, :2]


def _norm_mod_kernel(x_ref, g_ref, sh_ref, sc_ref, o_ref):
    x = x_ref[...]
    y = x * lax.rsqrt(jnp.mean(x * x, axis=-1, keepdims=True) + EPS)
    y = y * g_ref[...]
    o_ref[...] = (y * (1.0 + sc_ref[0]) + sh_ref[0]).astype(o_ref.dtype)


def _norm_mod(x, g, mods, idx, n_lat):
    t, d = x.shape
    nl = n_lat // EW_ROWS

    def sel(i, k):
        return (jnp.where(i >= nl, N_MOD, 0) + k, 0, 0)

    return pl.pallas_call(
        _norm_mod_kernel,
        grid=(t // EW_ROWS,),
        in_specs=[
            pl.BlockSpec((EW_ROWS, d), lambda i: (i, 0)),
            pl.BlockSpec((1, d), lambda i: (0, 0)),
            pl.BlockSpec((1, 1, d), lambda i: sel(i, idx)),
            pl.BlockSpec((1, 1, d), lambda i: sel(i, idx + 1)),
        ],
        out_specs=pl.BlockSpec((EW_ROWS, d), lambda i: (i, 0)),
        out_shape=jax.ShapeDtypeStruct((t, d), BF16),
        compiler_params=_params("parallel"),
        name="norm_mod",
    )(x, g.reshape(1, d), mods, mods)


def _final_norm_kernel(x_ref, g_ref, o_ref):
    x = x_ref[...]
    y = x * lax.rsqrt(jnp.mean(x * x, axis=-1, keepdims=True) + EPS)
    o_ref[...] = y * g_ref[...]


def _final_norm(x, g, n_lat):
    t, d = x.shape
    return pl.pallas_call(
        _final_norm_kernel,
        grid=(n_lat // EW_ROWS,),
        in_specs=[pl.BlockSpec((EW_ROWS, d), lambda i: (i, 0)),
                  pl.BlockSpec((1, d), lambda i: (0, 0))],
        out_specs=pl.BlockSpec((EW_ROWS, d), lambda i: (i, 0)),
        out_shape=jax.ShapeDtypeStruct((n_lat, d), F32),
        compiler_params=_params("parallel"),
        name="final_norm",
    )(x, g.reshape(1, d))


def _tall_row_tile(t):
    return next(tm for tm in (1408, ROW_TILE) if t % tm == 0)


def _serpentine(j, i, n_i):
    return jnp.where(j % 2 == 0, i, n_i - 1 - i)


def _with_bf16_weights(w_refs, wb_refs, body):
    first = pl.program_id(1) == 0

    @pl.when(first)
    def _():
        ws = []
        for w_ref, wb in zip(w_refs, wb_refs):
            wv = w_ref[0].astype(BF16)
            wb[...] = wv
            ws.append(wv)
        body(ws)

    @pl.when(jnp.logical_not(first))
    def _():
        body([wb[...] for wb in wb_refs])


def _ffn_up_kernel(u_ref, wg_ref, wu_ref, o_ref, wgb, wub):
    def body(ws):
        u = u_ref[...]
        a = jnp.dot(u, ws[0], preferred_element_type=F32)
        b = jnp.dot(u, ws[1], preferred_element_type=F32)
        o_ref[...] = (a * jax.nn.sigmoid(a) * b).astype(o_ref.dtype)

    _with_bf16_weights((wg_ref, wu_ref), (wgb, wub), body)


def _ffn_up(u, w_gu, layer):
    t, d = u.shape
    f = w_gu.shape[-1] // 2
    tn = COL_TILE // 2
    nj = f // tn
    tm = _tall_row_tile(t)
    ni = t // tm
    return pl.pallas_call(
        _ffn_up_kernel,
        grid=(nj, ni),
        in_specs=[
            pl.BlockSpec((tm, d), lambda j, i: (_serpentine(j, i, ni), 0)),
            pl.BlockSpec((1, d, tn), lambda j, i: (layer, 0, j)),
            pl.BlockSpec((1, d, tn), lambda j, i: (layer, 0, j + nj)),
        ],
        out_specs=pl.BlockSpec((tm, tn), lambda j, i: (_serpentine(j, i, ni), j)),
        out_shape=jax.ShapeDtypeStruct((t, f), BF16),
        scratch_shapes=[pltpu.VMEM((d, tn), BF16), pltpu.VMEM((d, tn), BF16)],
        compiler_params=_params("parallel", "arbitrary"),
        name="ffn_up",
    )(u, w_gu, w_gu)


def _down_kernel(h_ref, w_ref, x_ref, g_ref, o_ref, wb, *, coef, n_lat, ni):
    def body(ws):
        acc = jnp.dot(h_ref[...], ws[0], preferred_element_type=F32)
        tm = acc.shape[0]
        tile = _serpentine(pl.program_id(0), pl.program_id(1), ni)
        row = tile * tm + lax.broadcasted_iota(jnp.int32, (tm, 1), 0)
        g = jnp.where(row >= n_lat, g_ref[1:2, :], g_ref[0:1, :])
        o_ref[...] = x_ref[...] + coef * g * acc

    _with_bf16_weights((w_ref,), (wb,), body)


def _down(h, w, layer, x, gates, coef, n_lat):
    t, k = h.shape
    d = w.shape[-1]
    ni = t // ROW_TILE
    tn = next(c for c in (2 * COL_TILE, COL_TILE) if d % c == 0)
    return pl.pallas_call(
        functools.partial(_down_kernel, coef=coef, n_lat=n_lat, ni=ni),
        grid=(d // tn, ni),
        in_specs=[
            pl.BlockSpec((ROW_TILE, k), lambda j, i: (_serpentine(j, i, ni), 0)),
            pl.BlockSpec((1, k, tn), lambda j, i: (layer, 0, j), pipeline_mode=pl.Buffered(1)),
            pl.BlockSpec((ROW_TILE, tn), lambda j, i: (_serpentine(j, i, ni), j)),
            pl.BlockSpec((2, tn), lambda j, i: (0, j)),
        ],
        out_specs=pl.BlockSpec((ROW_TILE, tn), lambda j, i: (_serpentine(j, i, ni), j)),
        out_shape=jax.ShapeDtypeStruct((t, d), F32),
        input_output_aliases={2: 0},
        scratch_shapes=[pltpu.VMEM((k, tn), BF16)],
        compiler_params=_params("parallel", "arbitrary"),
        name="down_residual",
    )(h, w, x, gates)


def _proj_kernel(u_ref, w_ref, o_ref, wb, *, sigmoid):
    def body(ws):
        acc = jnp.dot(u_ref[...], ws[0], preferred_element_type=F32)
        if sigmoid:
            acc = jax.nn.sigmoid(acc)
        o_ref[...] = acc.astype(o_ref.dtype)

    _with_bf16_weights((w_ref,), (wb,), body)


def _col_tile(col0, ncols):
    return next(t for t in (COL_TILE, 256, LANES) if col0 % t == 0 and ncols % t == 0)


def _proj(u, w, layer, col0, ncols, sigmoid, out_dtype, name):
    t, d = u.shape
    tn = _col_tile(col0, ncols)
    j0 = col0 // tn
    tm = _tall_row_tile(t)
    ni = t // tm
    return pl.pallas_call(
        functools.partial(_proj_kernel, sigmoid=sigmoid),
        grid=(ncols // tn, ni),
        in_specs=[
            pl.BlockSpec((tm, d), lambda j, i: (_serpentine(j, i, ni), 0)),
            pl.BlockSpec((1, d, tn), lambda j, i: (layer, 0, j + j0)),
        ],
        out_specs=pl.BlockSpec((tm, tn), lambda j, i: (_serpentine(j, i, ni), j)),
        out_shape=jax.ShapeDtypeStruct((t, ncols), out_dtype),
        scratch_shapes=[pltpu.VMEM((d, tn), BF16)],
        compiler_params=_params("parallel", "arbitrary"),
        name=name,
    )(u, w)


def _head_prep_kernel(t_ref, g_ref, cc_ref, se_ref, so_ref, o_ref, *, scale):
    ones = jnp.ones((HEAD_DIM, LANES), BF16)
    for hh in range(t_ref.shape[1] // HEAD_DIM):
        cols = slice(hh * HEAD_DIM, (hh + 1) * HEAD_DIM)
        y = t_ref[:, cols]
        sq = y * y
        hi = sq.astype(BF16)
        lo = (sq - hi.astype(F32)).astype(BF16)
        ssq = (jnp.dot(hi, ones, preferred_element_type=F32)
               + jnp.dot(lo, ones, preferred_element_type=F32))
        y = y * lax.rsqrt(ssq * (1.0 / HEAD_DIM) + EPS) * g_ref[...]
        y = (y * cc_ref[...] + pltpu.roll(y, LANES - 1, 1) * se_ref[...]
             + pltpu.roll(y, 1, 1) * so_ref[...])
        o_ref[:, cols] = (y * scale).astype(o_ref.dtype)


def _head_prep(p, gain, rope, scale, name):
    t, ncols = p.shape
    cc, se, so = rope
    tab = pl.BlockSpec((EW_ROWS, LANES), lambda i: (i, 0))
    blk = pl.BlockSpec((EW_ROWS, ncols), lambda i: (i, 0))
    return pl.pallas_call(
        functools.partial(_head_prep_kernel, scale=scale),
        grid=(t // EW_ROWS,),
        in_specs=[blk, pl.BlockSpec((1, LANES), lambda i: (0, 0)), tab, tab, tab],
        out_specs=blk,
        out_shape=jax.ShapeDtypeStruct((t, ncols), BF16),
        compiler_params=_params("parallel"),
        name=name,
    )(p, gain.reshape(1, LANES), cc, se, so)


def _merge_kernel(ya_ref, yb_ref, yc_ref, wa_ref, wb_ref, wc_ref, ga_ref, gb_ref, gc_ref, o_ref,
                  wab, wbb, wcb):
    def body(ws):
        m = ga_ref[...].astype(F32) * jnp.dot(ya_ref[...], ws[0], preferred_element_type=F32)
        m += gb_ref[...].astype(F32) * jnp.dot(yb_ref[...], ws[1], preferred_element_type=F32)
        m += gc_ref[...].astype(F32) * jnp.dot(yc_ref[...], ws[2], preferred_element_type=F32)
        o_ref[...] = m.astype(o_ref.dtype)

    _with_bf16_weights((wa_ref, wb_ref, wc_ref), (wab, wbb, wcb), body)


def _merge(ya, yb, yc, wa, wb, wc, layer, gates):
    t, w = ya.shape
    d = wa.shape[-1]
    tn = next(c for c in (2 * COL_TILE, COL_TILE) if d % c == 0)
    nj = d // tn
    ni = t // ROW_TILE
    y_spec = pl.BlockSpec((ROW_TILE, w), lambda j, i: (_serpentine(j, i, ni), 0))
    w_spec = pl.BlockSpec((1, w, tn), lambda j, i: (layer, 0, j), pipeline_mode=pl.Buffered(1))

    def g_spec(k):
        return pl.BlockSpec((ROW_TILE, tn), lambda j, i: (_serpentine(j, i, ni), j + k * nj))

    return pl.pallas_call(
        _merge_kernel,
        grid=(nj, ni),
        in_specs=[y_spec, y_spec, y_spec, w_spec, w_spec, w_spec, g_spec(0), g_spec(1), g_spec(2)],
        out_specs=pl.BlockSpec((ROW_TILE, tn), lambda j, i: (_serpentine(j, i, ni), j)),
        out_shape=jax.ShapeDtypeStruct((t, d), BF16),
        scratch_shapes=[pltpu.VMEM((w, tn), BF16)] * 3,
        compiler_params=_params("parallel", "arbitrary"),
        name="merge",
    )(ya, yb, yc, wa, wb, wc, gates, gates, gates)


def _lru_kernel(pa_ref, pg_ref, cw_ref, wa_ref, ba_ref, wx_ref, bx_ref, lam_ref, h0_ref,
                ya_ref, hT_ref, work, a_sc, b_sc, *, ts, chunk):
    xs = work.at[0]
    pad = SUBLANES
    win = chunk + 2 * pad
    n_chunks = ts // chunk
    zeros = jnp.zeros((pad, LANES), F32)
    xs[pl.ds(0, pad), :] = zeros
    xs[pl.ds(pad + ts, pad), :] = zeros

    def copy_in(c, carry):
        t0 = pl.multiple_of(c * chunk, chunk)
        xs[pl.ds(pad + t0, chunk), :] = pa_ref[pl.ds(t0, chunk), :]
        return carry

    lax.fori_loop(0, n_chunks, copy_in, 0)

    sp = [jax.nn.softplus(-lam_ref[d:d + 1, :]) for d in range(2)]

    def gates(c, carry):
        t0 = pl.multiple_of(c * chunk, chunk)
        xw = xs[pl.ds(t0, win), :]
        xa = None
        for k in range(CONV_A):
            sh = pltpu.roll(xw, (win + 1 - k) % win, 0) if k != 1 else xw
            term = sh[pad:pad + chunk, :] * cw_ref[k:k + 1, :]
            xa = term if xa is None else xa + term
        xb = xa.astype(BF16)
        for d in range(2):
            r = jax.nn.sigmoid(_dot(xb, wa_ref[d, 0]) + ba_ref[d:d + 1, :])
            i = jax.nn.sigmoid(_dot(xb, wx_ref[d, 0]) + bx_ref[d:d + 1, :])
            log_a = -LRU_C * r * sp[d]
            a = jnp.exp(log_a)
            b = jnp.sqrt(1.0 - a * a) * (i * xa)
            a_sc[d, pl.ds(t0, chunk), :] = a
            b_sc[d, pl.ds(t0, chunk), :] = b
        return carry

    lax.fori_loop(0, n_chunks, gates, 0)

    row = lax.broadcasted_iota(jnp.int32, (SUBLANES, LANES), 0)
    steps = (1, 2, 4)

    def scan(j, carry):
        cf, cb = carry
        tf = pl.multiple_of(j * SUBLANES, SUBLANES)
        tb = pl.multiple_of(ts - (j + 1) * SUBLANES, SUBLANES)
        af = a_sc[0, pl.ds(tf, SUBLANES), :]
        bf = b_sc[0, pl.ds(tf, SUBLANES), :]
        ab = a_sc[1, pl.ds(tb, SUBLANES), :]
        bb = b_sc[1, pl.ds(tb, SUBLANES), :]
        for s in steps:
            mf = row >= s
            bf = bf + af * jnp.where(mf, pltpu.roll(bf, s, 0), 0.0)
            af = af * jnp.where(mf, pltpu.roll(af, s, 0), 1.0)
            mb = row < SUBLANES - s
            bb = bb + ab * jnp.where(mb, pltpu.roll(bb, SUBLANES - s, 0), 0.0)
            ab = ab * jnp.where(mb, pltpu.roll(ab, SUBLANES - s, 0), 1.0)
        hf = bf + af * cf
        hb = bb + ab * cb
        work[0, pl.ds(tf, SUBLANES), :] = hf
        work[1, pl.ds(tb, SUBLANES), :] = hb
        cf = jnp.broadcast_to(hf[SUBLANES - 1:SUBLANES, :], (SUBLANES, LANES))
        cb = jnp.broadcast_to(hb[0:1, :], (SUBLANES, LANES))
        return cf, cb

    c0 = (jnp.broadcast_to(h0_ref[0:1, :], (SUBLANES, LANES)),
          jnp.broadcast_to(h0_ref[1:2, :], (SUBLANES, LANES)))
    cf, cb = lax.fori_loop(0, ts // SUBLANES, scan, c0, unroll=4)
    hT_ref[0:1, :] = cf[0:1, :]
    hT_ref[1:2, :] = cb[0:1, :]

    def finish(c, carry):
        t0 = pl.multiple_of(c * chunk, chunk)
        h = work[0, pl.ds(t0, chunk), :] + work[1, pl.ds(t0, chunk), :]
        g = jax.nn.gelu(pg_ref[pl.ds(t0, chunk), :], approximate=True)
        ya_ref[pl.ds(t0, chunk), :] = (h * g).astype(ya_ref.dtype)
        return carry

    lax.fori_loop(0, n_chunks, finish, 0)


def _lru(p_ax, p_ag, row0, ts, col_ax, col_ag, cw, wa, ba, wx, bx, lam, h0):
    w = cw.shape[-1]
    nblk = w // LANES
    rb = row0 // ts
    cax = col_ax // LANES
    cag = col_ag // LANES
    chunk = min(EW_ROWS, ts)
    kern = functools.partial(_lru_kernel, ts=ts, chunk=chunk)
    vec = pl.BlockSpec((2, LANES), lambda j: (0, j))
    mat = pl.BlockSpec((2, 1, LANES, LANES), lambda j: (0, j, 0, 0))
    return pl.pallas_call(
        kern,
        grid=(nblk,),
        in_specs=[
            pl.BlockSpec((ts, LANES), lambda j: (rb, cax + j)),
            pl.BlockSpec((ts, LANES), lambda j: (rb, cag + j)),
            pl.BlockSpec((CONV_A, LANES), lambda j: (0, j)),
            mat, vec, mat, vec, vec, vec,
        ],
        out_specs=[pl.BlockSpec((ts, LANES), lambda j: (0, j)),
                   pl.BlockSpec((2, LANES), lambda j: (0, j))],
        out_shape=[jax.ShapeDtypeStruct((ts, w), BF16),
                   jax.ShapeDtypeStruct((2, w), F32)],
        scratch_shapes=[pltpu.VMEM((2, ts + 2 * SUBLANES, LANES), F32),
                        pltpu.VMEM((2, ts, LANES), F32),
                        pltpu.VMEM((2, ts, LANES), F32)],
        compiler_params=_params("parallel"),
        name="rglru",
    )(p_ax, p_ag, cw, wa, ba, wx, bx, lam, h0)


def _conv3_kernel(x_ref, xp_ref, xn_ref, w_ref, o_ref):
    i = pl.program_id(0)
    first = i == 0
    last = i == pl.num_programs(0) - 1
    x = x_ref[...]
    r = x.shape[0]
    row = lax.broadcasted_iota(jnp.int32, (r, 1), 0)
    prev_row = jnp.where(first, 0.0, xp_ref[SUBLANES - 1:SUBLANES, :])
    next_row = jnp.where(last, 0.0, xn_ref[0:1, :])
    xm1 = jnp.where(row == 0, prev_row, pltpu.roll(x, 1, 0))
    xp1 = jnp.where(row == r - 1, next_row, pltpu.roll(x, r - 1, 0))
    o_ref[0] = w_ref[0:1, :] * xm1 + w_ref[1:2, :] * x + w_ref[2:3, :] * xp1


def _conv3(p_b, row0, ts, col_b, w3, wmix):
    r = min(2 * EW_ROWS, ts)
    rb = row0 // r
    hb = r // SUBLANES
    cb = col_b // wmix
    n_r = ts // r
    return pl.pallas_call(
        _conv3_kernel,
        grid=(n_r, 3),
        in_specs=[
            pl.BlockSpec((r, wmix), lambda i, j: (rb + i, cb + j)),
            pl.BlockSpec((SUBLANES, wmix), lambda i, j: (jnp.maximum((rb + i) * hb - 1, 0), cb + j)),
            pl.BlockSpec((SUBLANES, wmix),
                         lambda i, j: (jnp.minimum((rb + i + 1) * hb, (rb + n_r) * hb - 1), cb + j)),
            pl.BlockSpec((CONV_B, wmix), lambda i, j: (0, j)),
        ],
        out_specs=pl.BlockSpec((1, r, wmix), lambda i, j: (j, i, 0)),
        out_shape=jax.ShapeDtypeStruct((3, ts, wmix), F32),
        compiler_params=_params("arbitrary", "arbitrary"),
        name="hyena_conv3",
    )(p_b, p_b, p_b, w3)


def _filter_kernel(z_ref, tl_ref, w1_ref, b1_ref, fr_ref, w2_ref, b2_ref, w3_ref, w3b_ref, ad_ref,
                   k_ref, ssq_ref, *, half_tiles):
    i = pl.program_id(0)
    hi = lax.Precision.HIGHEST
    fr = fr_ref[...]
    h = jnp.sin(fr * (jnp.dot(z_ref[...], w1_ref[...], precision=hi, preferred_element_type=F32)
                      + b1_ref[...]))
    h = jnp.sin(fr * (jnp.dot(h, w2_ref[...], precision=hi, preferred_element_type=F32) + b2_ref[...]))
    decay = jnp.exp(-tl_ref[...] * ad_ref[...])
    taps = _dot(h, w3_ref[0]) * decay
    r = taps.shape[0]
    row = lax.broadcasted_iota(jnp.int32, (r, 1), 0)
    k_ref[...] = taps

    @pl.when(i == 0)
    def _():
        back = _dot(h, w3b_ref[0]) * decay
        k_ref[...] = taps + jnp.where(row == 0, back, 0.0)
        ssq_ref[...] = jnp.zeros_like(ssq_ref)

    @pl.when(i == half_tiles)
    def _():
        k_ref[...] = jnp.where(row == 0, 0.0, taps)

    kk = k_ref[...]
    ssq_ref[...] += jnp.sum(kk * kk, axis=0, keepdims=True)


def _hyena_filter(seq, p):
    wmix = p["hy_skip"].shape[-1]
    hidden = p["hy_fw1"].shape[-1]
    r = min(2 * EW_ROWS, seq)
    t_idx = jnp.arange(seq, dtype=F32)
    t_lin = t_idx / max(seq - 1, 1)
    bands = jnp.linspace(1e-4, HYENA_BANDS - 1, HYENA_BANDS, dtype=F32)
    ang = (2.0 * math.pi / seq) * t_idx[:, None] * bands[None, :]
    z = jnp.concatenate([t_lin[:, None], jnp.cos(ang), -jnp.sin(ang)], axis=-1)
    rev = lambda a: jnp.concatenate([a[:1], jnp.flip(a[1:], axis=0)], axis=0)
    zz = jnp.concatenate([z, rev(z)], axis=0)
    zz = jnp.pad(zz, ((0, 0), (0, LANES - HYENA_EMB)))
    tl = jnp.concatenate([t_lin, rev(t_lin)])[:, None]
    w1 = jnp.pad(p["hy_fw1"], ((0, LANES - HYENA_EMB), (0, 0)))
    w3 = p["hy_fw3"].reshape(hidden, HYENA_ORDER, 2, wmix).transpose(2, 0, 1, 3)
    w3 = w3.reshape(2, hidden, HYENA_ORDER * wmix)
    deltas = jnp.linspace(math.log(HYENA_TARGET) / HYENA_SLOW_DECAY,
                          math.log(HYENA_TARGET) / HYENA_FAST_DECAY, wmix, dtype=F32)
    ad = jnp.tile(jnp.abs(deltas), HYENA_ORDER)[None, :]
    ow = HYENA_ORDER * wmix
    half = seq // r
    full = lambda shape: pl.BlockSpec(shape, lambda i: tuple(0 for _ in shape))
    return pl.pallas_call(
        functools.partial(_filter_kernel, half_tiles=half),
        grid=(2 * half,),
        in_specs=[
            pl.BlockSpec((r, LANES), lambda i: (i, 0)),
            pl.BlockSpec((r, 1), lambda i: (i, 0)),
            full((LANES, hidden)), full((1, hidden)), full((1, hidden)),
            full((hidden, hidden)), full((1, hidden)),
            pl.BlockSpec((1, hidden, ow), lambda i: (jnp.where(i >= half, 1, 0), 0, 0)),
            pl.BlockSpec((1, hidden, ow), lambda i: (1, 0, 0)),
            full((1, ow)),
        ],
        out_specs=[pl.BlockSpec((r, ow), lambda i: (i, 0)),
                   pl.BlockSpec((1, ow), lambda i: (0, 0))],
        out_shape=[jax.ShapeDtypeStruct((2 * seq, ow), F32),
                   jax.ShapeDtypeStruct((1, ow), F32)],
        compiler_params=_params("arbitrary"),
        name="hyena_filter",
    )(zz, tl, w1, p["hy_fb1"][None, :], p["hy_freq"][None, :], p["hy_fw2"], p["hy_fb2"][None, :],
      w3, w3, ad)


def _kron_fwd_kernel(f_ref, x_ref, o_ref):
    f = f_ref[...]
    halves = []
    for h in range(2):
        xh = x_ref[0, :, h * SUBLANES:(h + 1) * SUBLANES, :]
        xh = xh.reshape(xh.shape[0] * SUBLANES, xh.shape[2]).astype(BF16)
        r = jnp.dot(f, xh, preferred_element_type=F32)
        halves.append(r.reshape(r.shape[0] // SUBLANES, SUBLANES, r.shape[1]))
    o_ref[...] = jnp.concatenate(halves, axis=1).astype(o_ref.dtype)


def _kron_fwd(fk, x4, sel, name):
    _, nt1, nt2, w = x4.shape
    rows = fk.shape[0] // SUBLANES
    tw = 512 if w % 512 == 0 else LANES
    rt = 2 * SUBLANES
    return pl.pallas_call(
        _kron_fwd_kernel,
        grid=(nt2 // rt, w // tw),
        in_specs=[pl.BlockSpec(fk.shape, lambda i, j: (0, 0)),
                  pl.BlockSpec((1, nt1, rt, tw), lambda i, j: (sel, 0, i, j))],
        out_specs=pl.BlockSpec((rows, rt, tw), lambda i, j: (0, i, j)),
        out_shape=jax.ShapeDtypeStruct((rows, nt2, w), BF16),
        compiler_params=_params("parallel", "parallel"),
        name=name,
    )(fk, x4)


def _kron_inv_gate_kernel(g_ref, b_ref, x_ref, v_ref, sk_ref, o_ref):
    g = g_ref[...]
    b = b_ref[...].astype(F32)
    sk = sk_ref[...]
    halves = []
    for h in range(2):
        lo, hi = h * SUBLANES, (h + 1) * SUBLANES
        bh = b[:, lo:hi, :]
        bh = bh.reshape(bh.shape[0] * SUBLANES, bh.shape[2]).astype(BF16)
        y = jnp.dot(g, bh, preferred_element_type=F32)
        y = y.reshape(y.shape[0] // SUBLANES, SUBLANES, y.shape[1])
        halves.append(x_ref[0, :, lo:hi, :] * (y + sk * v_ref[0, :, lo:hi, :]))
    o_ref[0] = jnp.concatenate(halves, axis=1).astype(o_ref.dtype)


def _kron_inv_gate(gk, b3, x4, x_sel, v4, v_sel, sk, out_dtype, name):
    _, nt2, w = b3.shape
    nt1 = gk.shape[0] // SUBLANES
    tw = 512 if w % 512 == 0 else LANES
    rt = 2 * SUBLANES
    return pl.pallas_call(
        _kron_inv_gate_kernel,
        grid=(nt2 // rt, w // tw),
        in_specs=[pl.BlockSpec(gk.shape, lambda i, j: (0, 0)),
                  pl.BlockSpec((b3.shape[0], rt, tw), lambda i, j: (0, i, j)),
                  pl.BlockSpec((1, nt1, rt, tw), lambda i, j: (x_sel, 0, i, j)),
                  pl.BlockSpec((1, nt1, rt, tw), lambda i, j: (v_sel, 0, i, j)),
                  pl.BlockSpec((1, 1, tw), lambda i, j: (0, 0, j))],
        out_specs=pl.BlockSpec((1, nt1, rt, tw), lambda i, j: (0, 0, i, j)),
        out_shape=jax.ShapeDtypeStruct((1, nt1, nt2, w), out_dtype),
        compiler_params=_params("parallel", "parallel"),
        name=name,
    )(gk, b3, x4, v4, sk.reshape(1, 1, w))


def _bmm_scale_kernel(m_ref, a_ref, s_ref, o_ref, *, bpb):
    rows_in = a_ref.shape[0] // bpb
    rows_out = o_ref.shape[0] // bpb
    for b in range(bpb):
        y = _dot(m_ref[b], a_ref[b * rows_in:(b + 1) * rows_in, :]) * s_ref[...]
        o_ref[b * rows_out:(b + 1) * rows_out, :] = y.astype(o_ref.dtype)


def _bmm_scale(m2, a, scale, rows_in, name):
    nb, rows_out, _ = m2.shape
    n = a.shape[1]
    tn = n // 2 if (n // 2) % LANES == 0 else n
    bpb = 4 if nb % 4 == 0 else 1
    return pl.pallas_call(
        functools.partial(_bmm_scale_kernel, bpb=bpb),
        grid=(nb // bpb, n // tn),
        in_specs=[pl.BlockSpec((bpb, rows_out, rows_in), lambda b, j: (b, 0, 0)),
                  pl.BlockSpec((bpb * rows_in, tn), lambda b, j: (b, j)),
                  pl.BlockSpec((1, tn), lambda b, j: (0, j))],
        out_specs=pl.BlockSpec((bpb * rows_out, tn), lambda b, j: (b, j)),
        out_shape=jax.ShapeDtypeStruct((nb * rows_out, n), BF16),
        compiler_params=_params("parallel", "arbitrary"),
        name=name,
    )(m2, a, scale)


def _spectral_one(m2, a, k, m3):
    x = _dot(m2, a)
    f = x.shape[0] // 2
    xr, xi = x[:f], x[f:]
    kr, ki = k[:f].astype(F32), k[f:].astype(F32)
    y = jnp.concatenate([xr * kr - xi * ki, xr * ki + xi * kr], axis=0)
    return _dot(m3, y)


def _spectral_kernel(m2_ref, a_ref, k_ref, m3_ref, o_ref, *, bpb):
    rows_in = a_ref.shape[0] // bpb
    f2 = k_ref.shape[0] // bpb
    rows_out = o_ref.shape[0] // bpb
    for b in range(bpb):
        y = _spectral_one(m2_ref[b], a_ref[b * rows_in:(b + 1) * rows_in, :],
                          k_ref[b * f2:(b + 1) * f2, :], m3_ref[b])
        o_ref[b * rows_out:(b + 1) * rows_out, :] = y.astype(o_ref.dtype)


def _spectral_gate_kernel(m2_ref, a_ref, k_ref, m3_ref, x_ref, v_ref, sk_ref, o_ref):
    y = _spectral_one(m2_ref[0], a_ref[...], k_ref[...], m3_ref[0])
    o_ref[...] = (x_ref[...] * (y + sk_ref[...] * v_ref[...])).astype(o_ref.dtype)


def _spectral(m2, a, kspec, order, m3, out_dtype, gate=None, name="hyena_spectral"):
    nb, f2, rows_in = m2.shape
    rows_out = m3.shape[1]
    wmix = a.shape[1]
    bpb = 4 if (nb % 4 == 0 and gate is None) else 1
    in_specs = [pl.BlockSpec((bpb, f2, rows_in), lambda b: (b, 0, 0)),
                pl.BlockSpec((bpb * rows_in, wmix), lambda b: (b, 0)),
                pl.BlockSpec((bpb * f2, wmix), lambda b: (b, order)),
                pl.BlockSpec((bpb, rows_out, f2), lambda b: (b, 0, 0))]
    args = [m2, a, kspec, m3]
    kern = functools.partial(_spectral_kernel, bpb=bpb)
    if gate is not None:
        x, v, sk = gate
        blk = pl.BlockSpec((rows_out, wmix), lambda b: (b, 0))
        in_specs += [blk, blk, pl.BlockSpec((1, wmix), lambda b: (0, 0))]
        args += [x, v, sk]
        kern = _spectral_gate_kernel
    return pl.pallas_call(
        kern,
        grid=(nb // bpb,),
        in_specs=in_specs,
        out_specs=pl.BlockSpec((bpb * rows_out, wmix), lambda b: (b, 0)),
        out_shape=jax.ShapeDtypeStruct((nb * rows_out, wmix), out_dtype),
        compiler_params=_params("parallel"),
        name=name,
    )(*args)


def _dft_tables(seq):
    n = 2 * seq
    n2 = DFT_INNER
    n1 = n // n2
    i1 = jnp.arange(n1, dtype=jnp.int32)
    i2 = jnp.arange(n2, dtype=jnp.int32)
    ang1 = (2.0 * math.pi / n1) * ((i1[:, None] * i1[None, :]) % n1).astype(F32)
    c1, s1 = jnp.cos(ang1), jnp.sin(ang1)
    f1 = jnp.stack([c1, -s1], axis=1).reshape(2 * n1, n1)
    q = i1[:, None, None] + n1 * i2[None, :, None]
    ang = (2.0 * math.pi / n) * ((q * i2[None, None, :]) % n).astype(F32)
    tr, ti = jnp.cos(ang), -jnp.sin(ang)
    m2 = jnp.concatenate([jnp.concatenate([tr, -ti], axis=2),
                          jnp.concatenate([ti, tr], axis=2)], axis=1)
    trt, tit = jnp.swapaxes(tr, 1, 2), jnp.swapaxes(ti, 1, 2)
    m3 = jnp.concatenate([jnp.concatenate([trt, tit], axis=2),
                          jnp.concatenate([-tit, trt], axis=2)], axis=1)
    g = jnp.stack([c1, -s1], axis=2).reshape(n1, 2 * n1)[: n1 // 2] / n
    eye = jnp.eye(SUBLANES, dtype=F32)
    kron = lambda m: jnp.kron(m, eye).astype(BF16)
    return kron(f1[:, : n1 // 2]), kron(f1), m2.astype(BF16), m3.astype(BF16), kron(g)


def _direct_dft_tables(seq):
    n = 2 * seq
    i = jnp.arange(n, dtype=jnp.int32)
    ang = (2.0 * math.pi / n) * ((i[:, None] * i[None, :]) % n).astype(F32)
    c, s = jnp.cos(ang), jnp.sin(ang)
    fwd = jnp.concatenate([c, -s], axis=0)
    inv = jnp.concatenate([c[:seq], -s[:seq]], axis=1) / n
    return fwd[:, :seq].astype(BF16)[None], fwd.astype(BF16)[None], inv.astype(BF16)[None]


def _hyena_long(u3, taps, ssq, skip, tables):
    fk_half, fk_full, m2, m3, gk = tables
    _, seq, wmix = u3.shape
    n2 = DFT_INNER
    n1 = 2 * seq // n2
    ow = taps.shape[1]
    scale = lax.rsqrt(ssq + EPS)
    ak = _kron_fwd(fk_full, taps.reshape(1, n1, n2, ow), 0, "hyena_filter_dft1")
    kspec = _bmm_scale(m2, ak.reshape(n1 * 2 * n2, ow), scale, 2 * n2, "hyena_filter_dft2")
    u4 = u3.reshape(3, n1 // 2, n2, wmix)
    z4, z_sel = u4, 0
    for o in range(HYENA_ORDER):
        a = _kron_fwd(fk_half, z4, z_sel, "hyena_dft1")
        b = _spectral(m2, a.reshape(n1 * 2 * n2, wmix), kspec, o, m3, BF16)
        last = o == HYENA_ORDER - 1
        z4 = _kron_inv_gate(gk, b.reshape(2 * n1, n2, wmix), u4, 1 + o, z4, z_sel, skip[o],
                            BF16 if last else F32, "hyena_dft4_gate")
        z_sel = 0
    return z4.reshape(seq, wmix)


def _hyena_short(u3, taps, ssq, skip, tables):
    fwd_half, fwd_full, inv = tables
    scale = lax.rsqrt(ssq + EPS)
    kspec = _bmm_scale(fwd_full, taps, scale, taps.shape[0], "hyena_ctx_filter_dft")
    z = u3[0]
    for o in range(HYENA_ORDER):
        last = o == HYENA_ORDER - 1
        z = _spectral(fwd_half, z, kspec, o, inv, BF16 if last else F32,
                      gate=(u3[1 + o], z, skip[o][None, :]), name="hyena_ctx_spectral")
    return z


def _attn_kernel(q_ref, k_ref, v_ref, o_ref, m_sc, acc_sc, s_sc, p_sc, a_sc, *, rows):
    j = pl.program_id(2)
    _, tq, tk = s_sc.shape
    nlb = tk // LANES

    @pl.when(j == 0)
    def _():
        m_sc[...] = jnp.full_like(m_sc, -jnp.inf)
        acc_sc[...] = jnp.zeros_like(acc_sc)

    k = k_ref[...]
    v = v_ref[...]
    v1 = jnp.concatenate([v, jnp.ones_like(v)], axis=1)
    for g in range(GQA_GROUP):
        q = q_ref[:, g * HEAD_DIM:(g + 1) * HEAD_DIM]
        s_sc[g] = lax.dot_general(q, k, (((1,), (1,)), ((), ())), preferred_element_type=F32)

    for g in range(GQA_GROUP):
        for c in range(tq // rows):
            rs = slice(c * rows, (c + 1) * rows)
            blocks = [s_sc[g, rs, b * LANES:(b + 1) * LANES] for b in range(nlb)]
            bmax = blocks[0]
            for blk in blocks[1:]:
                bmax = jnp.maximum(bmax, blk)
            m_prev = m_sc[g, rs, :]
            m_new = jnp.maximum(m_prev, jnp.max(bmax, axis=1, keepdims=True))
            for b, blk in enumerate(blocks):
                p_sc[g, rs, b * LANES:(b + 1) * LANES] = jnp.exp2(blk - m_new).astype(BF16)
            m_sc[g, rs, :] = m_new
            a_sc[g, rs, :] = jnp.exp2(m_prev - m_new)

    for g in range(GQA_GROUP):
        alpha = jnp.concatenate([a_sc[g], a_sc[g]], axis=1)
        acc_sc[g] = alpha * acc_sc[g] + jnp.dot(p_sc[g], v1, preferred_element_type=F32)

    @pl.when(j == pl.num_programs(2) - 1)
    def _():
        for g in range(GQA_GROUP):
            acc = acc_sc[g]
            o_ref[:, g * HEAD_DIM:(g + 1) * HEAD_DIM] = (
                acc[:, :HEAD_DIM] / acc[:, HEAD_DIM:]).astype(o_ref.dtype)


def _attention(q, k, v, q_row0, n_q, k_row0, n_k, tq, tk):
    n_kv = k.shape[1] // HEAD_DIM
    gw = GQA_GROUP * HEAD_DIM
    qb, kb = q_row0 // tq, k_row0 // tk
    return pl.pallas_call(
        functools.partial(_attn_kernel, rows=2 * SUBLANES),
        grid=(n_kv, n_q // tq, n_k // tk),
        in_specs=[pl.BlockSpec((tq, gw), lambda h, i, j: (qb + i, h)),
                  pl.BlockSpec((tk, HEAD_DIM), lambda h, i, j: (kb + j, h)),
                  pl.BlockSpec((tk, HEAD_DIM), lambda h, i, j: (kb + j, h))],
        out_specs=pl.BlockSpec((tq, gw), lambda h, i, j: (i, h)),
        out_shape=jax.ShapeDtypeStruct((n_q, q.shape[1]), BF16),
        scratch_shapes=[pltpu.VMEM((GQA_GROUP, tq, LANES), F32),
                        pltpu.VMEM((GQA_GROUP, tq, 2 * HEAD_DIM), F32),
                        pltpu.VMEM((GQA_GROUP, tq, tk), F32),
                        pltpu.VMEM((GQA_GROUP, tq, tk), BF16),
                        pltpu.VMEM((GQA_GROUP, tq, LANES), F32)],
        compiler_params=_params("parallel", "parallel", "arbitrary"),
        name="attention",
    )(q, k, v)


def _rope_tables(seq, n_ctx):
    rows = seq // GRID_W
    row = jnp.repeat(jnp.arange(rows, dtype=F32), GRID_W)
    col = jnp.tile(jnp.arange(GRID_W, dtype=F32), rows)
    n_pairs = HEAD_DIM // 4
    inv = ROPE_THETA ** (-jnp.arange(n_pairs, dtype=F32) / n_pairs)
    ang = jnp.concatenate([row[:, None] * inv, col[:, None] * inv], axis=-1)
    ang = jnp.concatenate([ang, jnp.zeros((n_ctx, HEAD_DIM // 2), F32)], axis=0)
    c = jnp.repeat(jnp.cos(ang), 2, axis=1)
    s = jnp.repeat(jnp.sin(ang), 2, axis=1)
    even = (jnp.arange(HEAD_DIM) % 2 == 0)[None, :]
    return c, jnp.where(even, -s, 0.0), jnp.where(even, 0.0, s)


def kernel(x, c, ctx, c_ctx, w_mod_down, w_mod_up, b_mod, norm_ffn1, norm_mix, norm_ffn2,
           ffn1_w_in, ffn1_w_out, ffn2_w_in, ffn2_w_out, w_in, lru_conv, lru_w_a, lru_b_a,
           lru_w_x, lru_b_x, lru_lambda, hy_conv, hy_fw1, hy_fb1, hy_freq, hy_fw2, hy_fb2,
           hy_fw3, hy_skip, q_norm, k_norm, w_branch_a, w_branch_b, w_branch_c, w_out, final_norm):
    bsz, seq, d = x.shape
    assert bsz == 1 and c.shape[0] == 1 and ctx.shape[0] == 1
    n_ctx = ctx.shape[1]
    depth = w_in.shape[0]
    wmix = lru_conv.shape[-1]
    kvw = wmix // GQA_GROUP
    t_all = seq + n_ctx
    assert t_all % ROW_TILE == 0 and seq % EW_ROWS == 0 and n_ctx % EW_ROWS == 0

    col_ax = 0
    col_ck = col_ax + wmix
    col_cv = col_ck + kvw
    col_ag = col_cv + kvw
    col_b = col_ag + wmix
    col_cq = col_b + 3 * wmix
    col_g = col_cq + wmix

    xs = jnp.concatenate([x[0], ctx[0]], axis=0)
    cc = jnp.zeros((SUBLANES, d), F32).at[0].set(c[0]).at[1].set(c_ctx)
    mods_all = _modulation(cc, w_mod_down, w_mod_up, b_mod)

    ffn1_in, ffn1_out, ffn2_in, ffn2_out = ffn1_w_in, ffn1_w_out, ffn2_w_in, ffn2_w_out
    w_in_b, wba, wbb, wbc, w_out_b = w_in, w_branch_a, w_branch_b, w_branch_c, w_out

    rope = _rope_tables(seq, n_ctx)
    dft_lat = _dft_tables(seq)
    dft_ctx = _direct_dft_tables(n_ctx)
    q_scale = HEAD_DIM ** -0.5 * math.log2(math.e)
    tq_lat = next(t for t in (512, EW_ROWS) if seq % t == 0)
    tk_lat = next(t for t in (2816, 1408, ROW_TILE) if t_all % t == 0)

    for i in range(depth):
        ctx_out = i < depth - 1
        mods = mods_all[i]
        mods3 = mods.reshape(2 * N_MOD, 1, d)

        u = _norm_mod(xs, norm_ffn1[i], mods3, 0, seq)
        h = _ffn_up(u, ffn1_in, i)
        xs = _down(h, ffn1_out, i, xs, mods[:, 2], 0.5, seq)

        u = _norm_mod(xs, norm_mix[i], mods3, 3, seq)
        p_ax = _proj(u, w_in_b, i, col_ax, wmix, False, F32, "mixer_in_lru")
        p_b = _proj(u, w_in_b, i, col_ag, 4 * wmix, False, F32, "mixer_in_gelu_hyena")
        kh = _head_prep(_proj(u, w_in_b, i, col_ck, kvw, False, F32, "mixer_in_k"),
                        k_norm[i], rope, 1.0, "k_prep")
        vh = _proj(u, w_in_b, i, col_cv, kvw, False, BF16, "mixer_in_v")
        qh = _head_prep(_proj(u, w_in_b, i, col_cq, wmix, False, F32, "mixer_in_q"),
                        q_norm[i], rope, q_scale, "q_prep")
        gates = _proj(u, w_in_b, i, col_g, 3 * d, True, BF16, "mixer_gates")

        lru_args = (lru_conv[i], lru_w_a[i], lru_b_a[i], lru_w_x[i], lru_b_x[i], lru_lambda[i])
        ya_c, h_c = _lru(p_ax, p_b, seq, n_ctx, 0, 0, *lru_args, jnp.zeros((2, wmix), F32))
        ya_l, _ = _lru(p_ax, p_b, 0, seq, 0, 0, *lru_args, h_c)

        hp = {"hy_fw1": hy_fw1[i], "hy_fb1": hy_fb1[i], "hy_freq": hy_freq[i], "hy_fw2": hy_fw2[i],
              "hy_fb2": hy_fb2[i], "hy_fw3": hy_fw3[i], "hy_skip": hy_skip[i]}
        taps_l, ssq_l = _hyena_filter(seq, hp)
        u3_l = _conv3(p_b, 0, seq, wmix, hy_conv[i], wmix)
        yb_l = _hyena_long(u3_l, taps_l, ssq_l, hy_skip[i], dft_lat)

        yc_l = _attention(qh, kh, vh, 0, seq, 0, t_all, tq_lat, tk_lat)

        if ctx_out:
            taps_c, ssq_c = _hyena_filter(n_ctx, hp)
            u3_c = _conv3(p_b, seq, n_ctx, wmix, hy_conv[i], wmix)
            yb_c = _hyena_short(u3_c, taps_c, ssq_c, hy_skip[i], dft_ctx)
            yc_c = _attention(qh, kh, vh, seq, n_ctx, seq, n_ctx, n_ctx, n_ctx)
        else:
            yb_c = jnp.zeros((n_ctx, wmix), BF16)
            yc_c = jnp.zeros((n_ctx, wmix), BF16)

        ya = jnp.concatenate([ya_l, ya_c], axis=0)
        yb = jnp.concatenate([yb_l, yb_c], axis=0)
        yc = jnp.concatenate([yc_l, yc_c], axis=0)
        m = _merge(ya, yb, yc, wba, wbb, wbc, i, gates)
        xs = _down(m, w_out_b, i, xs, mods[:, 5], 1.0, seq)

        u = _norm_mod(xs, norm_ffn2[i], mods3, 6, seq)
        h = _ffn_up(u, ffn2_in, i)
        xs = _down(h, ffn2_out, i, xs, mods[:, 8], 0.5, seq)

    return _final_norm(xs, final_norm, seq)[None]
```

```python
import functools
import math

import jax
import jax.numpy as jnp
from jax import lax
from jax.experimental import pallas as pl
from jax.experimental.pallas import tpu as pltpu

F32 = jnp.float32
BF16 = jnp.bfloat16

HEAD_DIM = 128
LANES = 128
SUBLANES = 8
GQA_GROUP = 3
GRID_W = 64
ROPE_THETA = 10000.0
LRU_C = 8.0
CONV_A = 4
CONV_B = 3
HYENA_ORDER = 2
HYENA_BANDS = 16
HYENA_EMB = 2 * HYENA_BANDS + 1
HYENA_FAST_DECAY = 0.3
HYENA_SLOW_DECAY = 1.5
HYENA_TARGET = 1e-2
N_MOD = 9
EPS = 1e-6
DFT_INNER = 128
VMEM_LIMIT = 56 * 1024 * 1024

ROW_TILE = 768
COL_TILE = 512
EW_ROWS = 256


def _params(*sem):
    return pltpu.CompilerParams(dimension_semantics=sem, vmem_limit_bytes=VMEM_LIMIT)


def _dot(a, b):
    return jnp.dot(a.astype(BF16), b.astype(BF16), preferred_element_type=F32)


def _mod_kernel(c_ref, wd_ref, wu_ref, b_ref, o_ref):
    c = c_ref[...]
    s = c * jax.nn.sigmoid(c)
    t = _dot(s, wd_ref[0])
    o_ref[0] = _dot(t, wu_ref[0]) + b_ref[0]


def _modulation(cc, w_down, w_up, b_mod):
    depth, d, rank = w_down.shape
    out = pl.pallas_call(
        _mod_kernel,
        grid=(depth, N_MOD),
        in_specs=[
            pl.BlockSpec((SUBLANES, d), lambda l, j: (0, 0)),
            pl.BlockSpec((1, d, rank), lambda l, j: (l, 0, 0)),
            pl.BlockSpec((1, rank, d), lambda l, j: (l, 0, j)),
            pl.BlockSpec((1, 1, d), lambda l, j: (l, 0, j)),
        ],
        out_specs=pl.BlockSpec((1, SUBLANES, d), lambda l, j: (l, 0, j)),
        out_shape=jax.ShapeDtypeStruct((depth, SUBLANES, N_MOD * d), F32),
        compiler_params=_params("arbitrary", "arbitrary"),
        name="modulation",
    )(cc, w_down, w_up, b_mod.reshape(depth, 1, N_MOD * d))
    return out.reshape(depth, SUBLANES, N_MOD, d)[:, :2]


def _norm_mod_kernel(x_ref, g_ref, sh_ref, sc_ref, o_ref):
    x = x_ref[...]
    y = x * lax.rsqrt(jnp.mean(x * x, axis=-1, keepdims=True) + EPS)
    y = y * g_ref[...]
    o_ref[...] = (y * (1.0 + sc_ref[0]) + sh_ref[0]).astype(o_ref.dtype)


def _norm_mod(x, g, mods, idx, n_lat):
    t, d = x.shape
    nl = n_lat // EW_ROWS

    def sel(i, k):
        return (jnp.where(i >= nl, N_MOD, 0) + k, 0, 0)

    return pl.pallas_call(
        _norm_mod_kernel,
        grid=(t // EW_ROWS,),
        in_specs=[
            pl.BlockSpec((EW_ROWS, d), lambda i: (i, 0)),
            pl.BlockSpec((1, d), lambda i: (0, 0)),
            pl.BlockSpec((1, 1, d), lambda i: sel(i, idx)),
            pl.BlockSpec((1, 1, d), lambda i: sel(i, idx + 1)),
        ],
        out_specs=pl.BlockSpec((EW_ROWS, d), lambda i: (i, 0)),
        out_shape=jax.ShapeDtypeStruct((t, d), BF16),
        compiler_params=_params("parallel"),
        name="norm_mod",
    )(x, g.reshape(1, d), mods, mods)


def _final_norm_kernel(x_ref, g_ref, o_ref):
    x = x_ref[...]
    y = x * lax.rsqrt(jnp.mean(x * x, axis=-1, keepdims=True) + EPS)
    o_ref[...] = y * g_ref[...]


def _final_norm(x, g, n_lat):
    t, d = x.shape
    return pl.pallas_call(
        _final_norm_kernel,
        grid=(n_lat // EW_ROWS,),
        in_specs=[pl.BlockSpec((EW_ROWS, d), lambda i: (i, 0)),
                  pl.BlockSpec((1, d), lambda i: (0, 0))],
        out_specs=pl.BlockSpec((EW_ROWS, d), lambda i: (i, 0)),
        out_shape=jax.ShapeDtypeStruct((n_lat, d), F32),
        compiler_params=_params("parallel"),
        name="final_norm",
    )(x, g.reshape(1, d))


def _tall_row_tile(t):
    return next(tm for tm in (1408, ROW_TILE) if t % tm == 0)


def _serpentine(j, i, n_i):
    return jnp.where(j % 2 == 0, i, n_i - 1 - i)


def _with_bf16_weights(w_refs, wb_refs, body):
    first = pl.program_id(1) == 0

    @pl.when(first)
    def _():
        ws = []
        for w_ref, wb in zip(w_refs, wb_refs):
            wv = w_ref[0].astype(BF16)
            wb[...] = wv
            ws.append(wv)
        body(ws)

    @pl.when(jnp.logical_not(first))
    def _():
        body([wb[...] for wb in wb_refs])


def _ffn_up_kernel(u_ref, wg_ref, wu_ref, o_ref, wgb, wub):
    def body(ws):
        u = u_ref[...]
        a = jnp.dot(u, ws[0], preferred_element_type=F32)
        b = jnp.dot(u, ws[1], preferred_element_type=F32)
        o_ref[...] = (a * jax.nn.sigmoid(a) * b).astype(o_ref.dtype)

    _with_bf16_weights((wg_ref, wu_ref), (wgb, wub), body)


def _ffn_up(u, w_gu, layer):
    t, d = u.shape
    f = w_gu.shape[-1] // 2
    tn = COL_TILE // 2
    nj = f // tn
    tm = _tall_row_tile(t)
    ni = t // tm
    return pl.pallas_call(
        _ffn_up_kernel,
        grid=(nj, ni),
        in_specs=[
            pl.BlockSpec((tm, d), lambda j, i: (_serpentine(j, i, ni), 0)),
            pl.BlockSpec((1, d, tn), lambda j, i: (layer, 0, j)),
            pl.BlockSpec((1, d, tn), lambda j, i: (layer, 0, j + nj)),
        ],
        out_specs=pl.BlockSpec((tm, tn), lambda j, i: (_serpentine(j, i, ni), j)),
        out_shape=jax.ShapeDtypeStruct((t, f), BF16),
        scratch_shapes=[pltpu.VMEM((d, tn), BF16), pltpu.VMEM((d, tn), BF16)],
        compiler_params=_params("parallel", "arbitrary"),
        name="ffn_up",
    )(u, w_gu, w_gu)


def _down_kernel(h_ref, w_ref, x_ref, g_ref, o_ref, wb, *, coef, n_lat, ni):
    def body(ws):
        acc = jnp.dot(h_ref[...], ws[0], preferred_element_type=F32)
        tm = acc.shape[0]
        tile = _serpentine(pl.program_id(0), pl.program_id(1), ni)
        row = tile * tm + lax.broadcasted_iota(jnp.int32, (tm, 1), 0)
        g = jnp.where(row >= n_lat, g_ref[1:2, :], g_ref[0:1, :])
        o_ref[...] = x_ref[...] + coef * g * acc

    _with_bf16_weights((w_ref,), (wb,), body)


def _down(h, w, layer, x, gates, coef, n_lat):
    t, k = h.shape
    d = w.shape[-1]
    ni = t // ROW_TILE
    return pl.pallas_call(
        functools.partial(_down_kernel, coef=coef, n_lat=n_lat, ni=ni),
        grid=(d // COL_TILE, ni),
        in_specs=[
            pl.BlockSpec((ROW_TILE, k), lambda j, i: (_serpentine(j, i, ni), 0)),
            pl.BlockSpec((1, k, COL_TILE), lambda j, i: (layer, 0, j)),
            pl.BlockSpec((ROW_TILE, COL_TILE), lambda j, i: (_serpentine(j, i, ni), j)),
            pl.BlockSpec((2, COL_TILE), lambda j, i: (0, j)),
        ],
        out_specs=pl.BlockSpec((ROW_TILE, COL_TILE), lambda j, i: (_serpentine(j, i, ni), j)),
        out_shape=jax.ShapeDtypeStruct((t, d), F32),
        input_output_aliases={2: 0},
        scratch_shapes=[pltpu.VMEM((k, COL_TILE), BF16)],
        compiler_params=_params("parallel", "arbitrary"),
        name="down_residual",
    )(h, w, x, gates)


def _proj_kernel(u_ref, w_ref, o_ref, wb, *, sigmoid):
    def body(ws):
        acc = jnp.dot(u_ref[...], ws[0], preferred_element_type=F32)
        if sigmoid:
            acc = jax.nn.sigmoid(acc)
        o_ref[...] = acc.astype(o_ref.dtype)

    _with_bf16_weights((w_ref,), (wb,), body)


def _col_tile(col0, ncols):
    return next(t for t in (COL_TILE, 256, LANES) if col0 % t == 0 and ncols % t == 0)


def _proj(u, w, layer, col0, ncols, sigmoid, out_dtype, name):
    t, d = u.shape
    tn = _col_tile(col0, ncols)
    j0 = col0 // tn
    tm = _tall_row_tile(t)
    ni = t // tm
    return pl.pallas_call(
        functools.partial(_proj_kernel, sigmoid=sigmoid),
        grid=(ncols // tn, ni),
        in_specs=[
            pl.BlockSpec((tm, d), lambda j, i: (_serpentine(j, i, ni), 0)),
            pl.BlockSpec((1, d, tn), lambda j, i: (layer, 0, j + j0)),
        ],
        out_specs=pl.BlockSpec((tm, tn), lambda j, i: (_serpentine(j, i, ni), j)),
        out_shape=jax.ShapeDtypeStruct((t, ncols), out_dtype),
        scratch_shapes=[pltpu.VMEM((d, tn), BF16)],
        compiler_params=_params("parallel", "arbitrary"),
        name=name,
    )(u, w)


def _head_prep_kernel(t_ref, g_ref, cc_ref, se_ref, so_ref, o_ref, *, scale):
    ones = jnp.ones((HEAD_DIM, LANES), BF16)
    for hh in range(t_ref.shape[1] // HEAD_DIM):
        cols = slice(hh * HEAD_DIM, (hh + 1) * HEAD_DIM)
        y = t_ref[:, cols]
        sq = y * y
        hi = sq.astype(BF16)
        lo = (sq - hi.astype(F32)).astype(BF16)
        ssq = (jnp.dot(hi, ones, preferred_element_type=F32)
               + jnp.dot(lo, ones, preferred_element_type=F32))
        y = y * lax.rsqrt(ssq * (1.0 / HEAD_DIM) + EPS) * g_ref[...]
        y = (y * cc_ref[...] + pltpu.roll(y, LANES - 1, 1) * se_ref[...]
             + pltpu.roll(y, 1, 1) * so_ref[...])
        o_ref[:, cols] = (y * scale).astype(o_ref.dtype)


def _head_prep(p, gain, rope, scale, name):
    t, ncols = p.shape
    cc, se, so = rope
    tab = pl.BlockSpec((EW_ROWS, LANES), lambda i: (i, 0))
    blk = pl.BlockSpec((EW_ROWS, ncols), lambda i: (i, 0))
    return pl.pallas_call(
        functools.partial(_head_prep_kernel, scale=scale),
        grid=(t // EW_ROWS,),
        in_specs=[blk, pl.BlockSpec((1, LANES), lambda i: (0, 0)), tab, tab, tab],
        out_specs=blk,
        out_shape=jax.ShapeDtypeStruct((t, ncols), BF16),
        compiler_params=_params("parallel"),
        name=name,
    )(p, gain.reshape(1, LANES), cc, se, so)


def _merge_kernel(ya_ref, yb_ref, yc_ref, wa_ref, wb_ref, wc_ref, ga_ref, gb_ref, gc_ref, o_ref,
                  wab, wbb, wcb):
    def body(ws):
        m = ga_ref[...].astype(F32) * jnp.dot(ya_ref[...], ws[0], preferred_element_type=F32)
        m += gb_ref[...].astype(F32) * jnp.dot(yb_ref[...], ws[1], preferred_element_type=F32)
        m += gc_ref[...].astype(F32) * jnp.dot(yc_ref[...], ws[2], preferred_element_type=F32)
        o_ref[...] = m.astype(o_ref.dtype)

    _with_bf16_weights((wa_ref, wb_ref, wc_ref), (wab, wbb, wcb), body)


def _merge(ya, yb, yc, wa, wb, wc, layer, gates):
    t, w = ya.shape
    d = wa.shape[-1]
    nj = d // COL_TILE
    ni = t // ROW_TILE
    y_spec = pl.BlockSpec((ROW_TILE, w), lambda j, i: (_serpentine(j, i, ni), 0))
    w_spec = pl.BlockSpec((1, w, COL_TILE), lambda j, i: (layer, 0, j))

    def g_spec(k):
        return pl.BlockSpec((ROW_TILE, COL_TILE), lambda j, i: (_serpentine(j, i, ni), j + k * nj))

    return pl.pallas_call(
        _merge_kernel,
        grid=(nj, ni),
        in_specs=[y_spec, y_spec, y_spec, w_spec, w_spec, w_spec, g_spec(0), g_spec(1), g_spec(2)],
        out_specs=pl.BlockSpec((ROW_TILE, COL_TILE), lambda j, i: (_serpentine(j, i, ni), j)),
        out_shape=jax.ShapeDtypeStruct((t, d), BF16),
        scratch_shapes=[pltpu.VMEM((w, COL_TILE), BF16)] * 3,
        compiler_params=_params("parallel", "arbitrary"),
        name="merge",
    )(ya, yb, yc, wa, wb, wc, gates, gates, gates)


def _lru_kernel(pa_ref, pg_ref, cw_ref, wa_ref, ba_ref, wx_ref, bx_ref, lam_ref, h0_ref,
                ya_ref, hT_ref, work, a_sc, b_sc, *, ts, chunk):
    xs = work.at[0]
    pad = SUBLANES
    win = chunk + 2 * pad
    n_chunks = ts // chunk
    zeros = jnp.zeros((pad, LANES), F32)
    xs[pl.ds(0, pad), :] = zeros
    xs[pl.ds(pad + ts, pad), :] = zeros

    def copy_in(c, carry):
        t0 = pl.multiple_of(c * chunk, chunk)
        xs[pl.ds(pad + t0, chunk), :] = pa_ref[pl.ds(t0, chunk), :]
        return carry

    lax.fori_loop(0, n_chunks, copy_in, 0)

    sp = [jax.nn.softplus(-lam_ref[d:d + 1, :]) for d in range(2)]

    def gates(c, carry):
        t0 = pl.multiple_of(c * chunk, chunk)
        xw = xs[pl.ds(t0, win), :]
        xa = None
        for k in range(CONV_A):
            sh = pltpu.roll(xw, (win + 1 - k) % win, 0) if k != 1 else xw
            term = sh[pad:pad + chunk, :] * cw_ref[k:k + 1, :]
            xa = term if xa is None else xa + term
        xb = xa.astype(BF16)
        for d in range(2):
            r = jax.nn.sigmoid(_dot(xb, wa_ref[d, 0]) + ba_ref[d:d + 1, :])
            i = jax.nn.sigmoid(_dot(xb, wx_ref[d, 0]) + bx_ref[d:d + 1, :])
            log_a = -LRU_C * r * sp[d]
            a = jnp.exp(log_a)
            b = jnp.sqrt(1.0 - a * a) * (i * xa)
            a_sc[d, pl.ds(t0, chunk), :] = a
            b_sc[d, pl.ds(t0, chunk), :] = b
        return carry

    lax.fori_loop(0, n_chunks, gates, 0)

    row = lax.broadcasted_iota(jnp.int32, (SUBLANES, LANES), 0)
    steps = (1, 2, 4)

    def scan(j, carry):
        cf, cb = carry
        tf = pl.multiple_of(j * SUBLANES, SUBLANES)
        tb = pl.multiple_of(ts - (j + 1) * SUBLANES, SUBLANES)
        af = a_sc[0, pl.ds(tf, SUBLANES), :]
        bf = b_sc[0, pl.ds(tf, SUBLANES), :]
        ab = a_sc[1, pl.ds(tb, SUBLANES), :]
        bb = b_sc[1, pl.ds(tb, SUBLANES), :]
        for s in steps:
            mf = row >= s
            bf = bf + af * jnp.where(mf, pltpu.roll(bf, s, 0), 0.0)
            af = af * jnp.where(mf, pltpu.roll(af, s, 0), 1.0)
            mb = row < SUBLANES - s
            bb = bb + ab * jnp.where(mb, pltpu.roll(bb, SUBLANES - s, 0), 0.0)
            ab = ab * jnp.where(mb, pltpu.roll(ab, SUBLANES - s, 0), 1.0)
        hf = bf + af * cf
        hb = bb + ab * cb
        work[0, pl.ds(tf, SUBLANES), :] = hf
        work[1, pl.ds(tb, SUBLANES), :] = hb
        cf = jnp.broadcast_to(hf[SUBLANES - 1:SUBLANES, :], (SUBLANES, LANES))
        cb = jnp.broadcast_to(hb[0:1, :], (SUBLANES, LANES))
        return cf, cb

    c0 = (jnp.broadcast_to(h0_ref[0:1, :], (SUBLANES, LANES)),
          jnp.broadcast_to(h0_ref[1:2, :], (SUBLANES, LANES)))
    cf, cb = lax.fori_loop(0, ts // SUBLANES, scan, c0, unroll=4)
    hT_ref[0:1, :] = cf[0:1, :]
    hT_ref[1:2, :] = cb[0:1, :]

    def finish(c, carry):
        t0 = pl.multiple_of(c * chunk, chunk)
        h = work[0, pl.ds(t0, chunk), :] + work[1, pl.ds(t0, chunk), :]
        g = jax.nn.gelu(pg_ref[pl.ds(t0, chunk), :], approximate=True)
        ya_ref[pl.ds(t0, chunk), :] = (h * g).astype(ya_ref.dtype)
        return carry

    lax.fori_loop(0, n_chunks, finish, 0)


def _lru(p_ax, p_ag, row0, ts, col_ax, col_ag, cw, wa, ba, wx, bx, lam, h0):
    w = cw.shape[-1]
    nblk = w // LANES
    rb = row0 // ts
    cax = col_ax // LANES
    cag = col_ag // LANES
    chunk = min(EW_ROWS, ts)
    kern = functools.partial(_lru_kernel, ts=ts, chunk=chunk)
    vec = pl.BlockSpec((2, LANES), lambda j: (0, j))
    mat = pl.BlockSpec((2, 1, LANES, LANES), lambda j: (0, j, 0, 0))
    return pl.pallas_call(
        kern,
        grid=(nblk,),
        in_specs=[
            pl.BlockSpec((ts, LANES), lambda j: (rb, cax + j)),
            pl.BlockSpec((ts, LANES), lambda j: (rb, cag + j)),
            pl.BlockSpec((CONV_A, LANES), lambda j: (0, j)),
            mat, vec, mat, vec, vec, vec,
        ],
        out_specs=[pl.BlockSpec((ts, LANES), lambda j: (0, j)),
                   pl.BlockSpec((2, LANES), lambda j: (0, j))],
        out_shape=[jax.ShapeDtypeStruct((ts, w), BF16),
                   jax.ShapeDtypeStruct((2, w), F32)],
        scratch_shapes=[pltpu.VMEM((2, ts + 2 * SUBLANES, LANES), F32),
                        pltpu.VMEM((2, ts, LANES), F32),
                        pltpu.VMEM((2, ts, LANES), F32)],
        compiler_params=_params("parallel"),
        name="rglru",
    )(p_ax, p_ag, cw, wa, ba, wx, bx, lam, h0)


def _conv3_kernel(x_ref, xp_ref, xn_ref, w_ref, o_ref):
    i = pl.program_id(0)
    first = i == 0
    last = i == pl.num_programs(0) - 1
    x = x_ref[...]
    r = x.shape[0]
    row = lax.broadcasted_iota(jnp.int32, (r, 1), 0)
    prev_row = jnp.where(first, 0.0, xp_ref[SUBLANES - 1:SUBLANES, :])
    next_row = jnp.where(last, 0.0, xn_ref[0:1, :])
    xm1 = jnp.where(row == 0, prev_row, pltpu.roll(x, 1, 0))
    xp1 = jnp.where(row == r - 1, next_row, pltpu.roll(x, r - 1, 0))
    o_ref[0] = w_ref[0:1, :] * xm1 + w_ref[1:2, :] * x + w_ref[2:3, :] * xp1


def _conv3(p_b, row0, ts, col_b, w3, wmix):
    r = min(2 * EW_ROWS, ts)
    rb = row0 // r
    hb = r // SUBLANES
    cb = col_b // wmix
    n_r = ts // r
    return pl.pallas_call(
        _conv3_kernel,
        grid=(n_r, 3),
        in_specs=[
            pl.BlockSpec((r, wmix), lambda i, j: (rb + i, cb + j)),
            pl.BlockSpec((SUBLANES, wmix), lambda i, j: (jnp.maximum((rb + i) * hb - 1, 0), cb + j)),
            pl.BlockSpec((SUBLANES, wmix),
                         lambda i, j: (jnp.minimum((rb + i + 1) * hb, (rb + n_r) * hb - 1), cb + j)),
            pl.BlockSpec((CONV_B, wmix), lambda i, j: (0, j)),
        ],
        out_specs=pl.BlockSpec((1, r, wmix), lambda i, j: (j, i, 0)),
        out_shape=jax.ShapeDtypeStruct((3, ts, wmix), F32),
        compiler_params=_params("arbitrary", "arbitrary"),
        name="hyena_conv3",
    )(p_b, p_b, p_b, w3)


def _filter_kernel(z_ref, tl_ref, w1_ref, b1_ref, fr_ref, w2_ref, b2_ref, w3_ref, w3b_ref, ad_ref,
                   k_ref, ssq_ref, *, half_tiles):
    i = pl.program_id(0)
    hi = lax.Precision.HIGHEST
    fr = fr_ref[...]
    h = jnp.sin(fr * (jnp.dot(z_ref[...], w1_ref[...], precision=hi, preferred_element_type=F32)
                      + b1_ref[...]))
    h = jnp.sin(fr * (jnp.dot(h, w2_ref[...], precision=hi, preferred_element_type=F32) + b2_ref[...]))
    decay = jnp.exp(-tl_ref[...] * ad_ref[...])
    taps = _dot(h, w3_ref[0]) * decay
    r = taps.shape[0]
    row = lax.broadcasted_iota(jnp.int32, (r, 1), 0)
    k_ref[...] = taps

    @pl.when(i == 0)
    def _():
        back = _dot(h, w3b_ref[0]) * decay
        k_ref[...] = taps + jnp.where(row == 0, back, 0.0)
        ssq_ref[...] = jnp.zeros_like(ssq_ref)

    @pl.when(i == half_tiles)
    def _():
        k_ref[...] = jnp.where(row == 0, 0.0, taps)

    kk = k_ref[...]
    ssq_ref[...] += jnp.sum(kk * kk, axis=0, keepdims=True)


def _hyena_filter(seq, p):
    wmix = p["hy_skip"].shape[-1]
    hidden = p["hy_fw1"].shape[-1]
    r = min(2 * EW_ROWS, seq)
    t_idx = jnp.arange(seq, dtype=F32)
    t_lin = t_idx / max(seq - 1, 1)
    bands = jnp.linspace(1e-4, HYENA_BANDS - 1, HYENA_BANDS, dtype=F32)
    ang = (2.0 * math.pi / seq) * t_idx[:, None] * bands[None, :]
    z = jnp.concatenate([t_lin[:, None], jnp.cos(ang), -jnp.sin(ang)], axis=-1)
    rev = lambda a: jnp.concatenate([a[:1], jnp.flip(a[1:], axis=0)], axis=0)
    zz = jnp.concatenate([z, rev(z)], axis=0)
    zz = jnp.pad(zz, ((0, 0), (0, LANES - HYENA_EMB)))
    tl = jnp.concatenate([t_lin, rev(t_lin)])[:, None]
    w1 = jnp.pad(p["hy_fw1"], ((0, LANES - HYENA_EMB), (0, 0)))
    w3 = p["hy_fw3"].reshape(hidden, HYENA_ORDER, 2, wmix).transpose(2, 0, 1, 3)
    w3 = w3.reshape(2, hidden, HYENA_ORDER * wmix)
    deltas = jnp.linspace(math.log(HYENA_TARGET) / HYENA_SLOW_DECAY,
                          math.log(HYENA_TARGET) / HYENA_FAST_DECAY, wmix, dtype=F32)
    ad = jnp.tile(jnp.abs(deltas), HYENA_ORDER)[None, :]
    ow = HYENA_ORDER * wmix
    half = seq // r
    full = lambda shape: pl.BlockSpec(shape, lambda i: tuple(0 for _ in shape))
    return pl.pallas_call(
        functools.partial(_filter_kernel, half_tiles=half),
        grid=(2 * half,),
        in_specs=[
            pl.BlockSpec((r, LANES), lambda i: (i, 0)),
            pl.BlockSpec((r, 1), lambda i: (i, 0)),
            full((LANES, hidden)), full((1, hidden)), full((1, hidden)),
            full((hidden, hidden)), full((1, hidden)),
            pl.BlockSpec((1, hidden, ow), lambda i: (jnp.where(i >= half, 1, 0), 0, 0)),
            pl.BlockSpec((1, hidden, ow), lambda i: (1, 0, 0)),
            full((1, ow)),
        ],
        out_specs=[pl.BlockSpec((r, ow), lambda i: (i, 0)),
                   pl.BlockSpec((1, ow), lambda i: (0, 0))],
        out_shape=[jax.ShapeDtypeStruct((2 * seq, ow), F32),
                   jax.ShapeDtypeStruct((1, ow), F32)],
        compiler_params=_params("arbitrary"),
        name="hyena_filter",
    )(zz, tl, w1, p["hy_fb1"][None, :], p["hy_freq"][None, :], p["hy_fw2"], p["hy_fb2"][None, :],
      w3, w3, ad)


def _kron_fwd_kernel(f_ref, x_ref, o_ref):
    f = f_ref[...]
    halves = []
    for h in range(2):
        xh = x_ref[0, :, h * SUBLANES:(h + 1) * SUBLANES, :]
        xh = xh.reshape(xh.shape[0] * SUBLANES, xh.shape[2]).astype(BF16)
        r = jnp.dot(f, xh, preferred_element_type=F32)
        halves.append(r.reshape(r.shape[0] // SUBLANES, SUBLANES, r.shape[1]))
    o_ref[...] = jnp.concatenate(halves, axis=1).astype(o_ref.dtype)


def _kron_fwd(fk, x4, sel, name):
    _, nt1, nt2, w = x4.shape
    rows = fk.shape[0] // SUBLANES
    tw = next(c for c in (768, 512, LANES) if w % c == 0)
    rt = 2 * SUBLANES
    return pl.pallas_call(
        _kron_fwd_kernel,
        grid=(nt2 // rt, w // tw),
        in_specs=[pl.BlockSpec(fk.shape, lambda i, j: (0, 0)),
                  pl.BlockSpec((1, nt1, rt, tw), lambda i, j: (sel, 0, i, j))],
        out_specs=pl.BlockSpec((rows, rt, tw), lambda i, j: (0, i, j)),
        out_shape=jax.ShapeDtypeStruct((rows, nt2, w), BF16),
        compiler_params=_params("parallel", "parallel"),
        name=name,
    )(fk, x4)


def _kron_inv_gate_kernel(g_ref, b_ref, x_ref, v_ref, sk_ref, o_ref):
    g = g_ref[...]
    b = b_ref[...].astype(F32)
    sk = sk_ref[...]
    halves = []
    for h in range(2):
        lo, hi = h * SUBLANES, (h + 1) * SUBLANES
        bh = b[:, lo:hi, :]
        bh = bh.reshape(bh.shape[0] * SUBLANES, bh.shape[2]).astype(BF16)
        y = jnp.dot(g, bh, preferred_element_type=F32)
        y = y.reshape(y.shape[0] // SUBLANES, SUBLANES, y.shape[1])
        halves.append(x_ref[0, :, lo:hi, :] * (y + sk * v_ref[0, :, lo:hi, :]))
    o_ref[0] = jnp.concatenate(halves, axis=1).astype(o_ref.dtype)


def _kron_inv_gate(gk, b3, x4, x_sel, v4, v_sel, sk, out_dtype, name):
    _, nt2, w = b3.shape
    nt1 = gk.shape[0] // SUBLANES
    tw = next(c for c in (768, 512, LANES) if w % c == 0)
    rt = 2 * SUBLANES
    return pl.pallas_call(
        _kron_inv_gate_kernel,
        grid=(nt2 // rt, w // tw),
        in_specs=[pl.BlockSpec(gk.shape, lambda i, j: (0, 0)),
                  pl.BlockSpec((b3.shape[0], rt, tw), lambda i, j: (0, i, j)),
                  pl.BlockSpec((1, nt1, rt, tw), lambda i, j: (x_sel, 0, i, j)),
                  pl.BlockSpec((1, nt1, rt, tw), lambda i, j: (v_sel, 0, i, j)),
                  pl.BlockSpec((1, 1, tw), lambda i, j: (0, 0, j))],
        out_specs=pl.BlockSpec((1, nt1, rt, tw), lambda i, j: (0, 0, i, j)),
        out_shape=jax.ShapeDtypeStruct((1, nt1, nt2, w), out_dtype),
        compiler_params=_params("parallel", "parallel"),
        name=name,
    )(gk, b3, x4, v4, sk.reshape(1, 1, w))


def _bmm_scale_kernel(m_ref, a_ref, s_ref, o_ref, *, bpb):
    rows_in = a_ref.shape[0] // bpb
    rows_out = o_ref.shape[0] // bpb
    for b in range(bpb):
        y = _dot(m_ref[b], a_ref[b * rows_in:(b + 1) * rows_in, :]) * s_ref[...]
        o_ref[b * rows_out:(b + 1) * rows_out, :] = y.astype(o_ref.dtype)


def _bmm_scale(m2, a, scale, rows_in, name):
    nb, rows_out, _ = m2.shape
    n = a.shape[1]
    tn = n // 2 if (n // 2) % LANES == 0 else n
    bpb = 8 if nb % 8 == 0 else 1
    return pl.pallas_call(
        functools.partial(_bmm_scale_kernel, bpb=bpb),
        grid=(nb // bpb, n // tn),
        in_specs=[pl.BlockSpec((bpb, rows_out, rows_in), lambda b, j: (b, 0, 0)),
                  pl.BlockSpec((bpb * rows_in, tn), lambda b, j: (b, j)),
                  pl.BlockSpec((1, tn), lambda b, j: (0, j))],
        out_specs=pl.BlockSpec((bpb * rows_out, tn), lambda b, j: (b, j)),
        out_shape=jax.ShapeDtypeStruct((nb * rows_out, n), BF16),
        compiler_params=_params("parallel", "arbitrary"),
        name=name,
    )(m2, a, scale)


def _spectral_one(m2, a, k, m3):
    x = _dot(m2, a)
    f = x.shape[0] // 2
    xr, xi = x[:f], x[f:]
    kr, ki = k[:f].astype(F32), k[f:].astype(F32)
    y = jnp.concatenate([xr * kr - xi * ki, xr * ki + xi * kr], axis=0)
    return _dot(m3, y)


def _spectral_kernel(m2_ref, a_ref, k_ref, m3_ref, o_ref, *, bpb):
    rows_in = a_ref.shape[0] // bpb
    f2 = k_ref.shape[0] // bpb
    rows_out = o_ref.shape[0] // bpb
    for b in range(bpb):
        y = _spectral_one(m2_ref[b], a_ref[b * rows_in:(b + 1) * rows_in, :],
                          k_ref[b * f2:(b + 1) * f2, :], m3_ref[b])
        o_ref[b * rows_out:(b + 1) * rows_out, :] = y.astype(o_ref.dtype)


def _spectral_gate_kernel(m2_ref, a_ref, k_ref, m3_ref, x_ref, v_ref, sk_ref, o_ref):
    y = _spectral_one(m2_ref[0], a_ref[...], k_ref[...], m3_ref[0])
    o_ref[...] = (x_ref[...] * (y + sk_ref[...] * v_ref[...])).astype(o_ref.dtype)


def _spectral(m2, a, kspec, order, m3, out_dtype, gate=None, name="hyena_spectral"):
    nb, f2, rows_in = m2.shape
    rows_out = m3.shape[1]
    wmix = a.shape[1]
    bpb = 8 if (nb % 8 == 0 and gate is None) else 1
    in_specs = [pl.BlockSpec((bpb, f2, rows_in), lambda b: (b, 0, 0)),
                pl.BlockSpec((bpb * rows_in, wmix), lambda b: (b, 0)),
                pl.BlockSpec((bpb * f2, wmix), lambda b: (b, order)),
                pl.BlockSpec((bpb, rows_out, f2), lambda b: (b, 0, 0))]
    args = [m2, a, kspec, m3]
    kern = functools.partial(_spectral_kernel, bpb=bpb)
    if gate is not None:
        x, v, sk = gate
        blk = pl.BlockSpec((rows_out, wmix), lambda b: (b, 0))
        in_specs += [blk, blk, pl.BlockSpec((1, wmix), lambda b: (0, 0))]
        args += [x, v, sk]
        kern = _spectral_gate_kernel
    return pl.pallas_call(
        kern,
        grid=(nb // bpb,),
        in_specs=in_specs,
        out_specs=pl.BlockSpec((bpb * rows_out, wmix), lambda b: (b, 0)),
        out_shape=jax.ShapeDtypeStruct((nb * rows_out, wmix), out_dtype),
        compiler_params=_params("parallel"),
        name=name,
    )(*args)


def _dft_tables(seq):
    n = 2 * seq
    n2 = DFT_INNER
    n1 = n // n2
    i1 = jnp.arange(n1, dtype=jnp.int32)
    i2 = jnp.arange(n2, dtype=jnp.int32)
    ang1 = (2.0 * math.pi / n1) * ((i1[:, None] * i1[None, :]) % n1).astype(F32)
    c1, s1 = jnp.cos(ang1), jnp.sin(ang1)
    f1 = jnp.stack([c1, -s1], axis=1).reshape(2 * n1, n1)
    q = i1[:, None, None] + n1 * i2[None, :, None]
    ang = (2.0 * math.pi / n) * ((q * i2[None, None, :]) % n).astype(F32)
    tr, ti = jnp.cos(ang), -jnp.sin(ang)
    m2 = jnp.concatenate([jnp.concatenate([tr, -ti], axis=2),
                          jnp.concatenate([ti, tr], axis=2)], axis=1)
    trt, tit = jnp.swapaxes(tr, 1, 2), jnp.swapaxes(ti, 1, 2)
    m3 = jnp.concatenate([jnp.concatenate([trt, tit], axis=2),
                          jnp.concatenate([-tit, trt], axis=2)], axis=1)
    g = jnp.stack([c1, -s1], axis=2).reshape(n1, 2 * n1)[: n1 // 2] / n
    eye = jnp.eye(SUBLANES, dtype=F32)
    kron = lambda m: jnp.kron(m, eye).astype(BF16)
    return kron(f1[:, : n1 // 2]), kron(f1), m2.astype(BF16), m3.astype(BF16), kron(g)


def _direct_dft_tables(seq):
    n = 2 * seq
    i = jnp.arange(n, dtype=jnp.int32)
    ang = (2.0 * math.pi / n) * ((i[:, None] * i[None, :]) % n).astype(F32)
    c, s = jnp.cos(ang), jnp.sin(ang)
    fwd = jnp.concatenate([c, -s], axis=0)
    inv = jnp.concatenate([c[:seq], -s[:seq]], axis=1) / n
    return fwd[:, :seq].astype(BF16)[None], fwd.astype(BF16)[None], inv.astype(BF16)[None]


def _hyena_long(u3, taps, ssq, skip, tables):
    fk_half, fk_full, m2, m3, gk = tables
    _, seq, wmix = u3.shape
    n2 = DFT_INNER
    n1 = 2 * seq // n2
    ow = taps.shape[1]
    scale = lax.rsqrt(ssq + EPS)
    ak = _kron_fwd(fk_full, taps.reshape(1, n1, n2, ow), 0, "hyena_filter_dft1")
    kspec = _bmm_scale(m2, ak.reshape(n1 * 2 * n2, ow), scale, 2 * n2, "hyena_filter_dft2")
    u4 = u3.reshape(3, n1 // 2, n2, wmix)
    z4, z_sel = u4, 0
    for o in range(HYENA_ORDER):
        a = _kron_fwd(fk_half, z4, z_sel, "hyena_dft1")
        b = _spectral(m2, a.reshape(n1 * 2 * n2, wmix), kspec, o, m3, BF16)
        last = o == HYENA_ORDER - 1
        z4 = _kron_inv_gate(gk, b.reshape(2 * n1, n2, wmix), u4, 1 + o, z4, z_sel, skip[o],
                            BF16 if last else F32, "hyena_dft4_gate")
        z_sel = 0
    return z4.reshape(seq, wmix)


def _hyena_short(u3, taps, ssq, skip, tables):
    fwd_half, fwd_full, inv = tables
    scale = lax.rsqrt(ssq + EPS)
    kspec = _bmm_scale(fwd_full, taps, scale, taps.shape[0], "hyena_ctx_filter_dft")
    z = u3[0]
    for o in range(HYENA_ORDER):
        last = o == HYENA_ORDER - 1
        z = _spectral(fwd_half, z, kspec, o, inv, BF16 if last else F32,
                      gate=(u3[1 + o], z, skip[o][None, :]), name="hyena_ctx_spectral")
    return z


def _attn_kernel(q_ref, k_ref, v_ref, o_ref, m_sc, acc_sc, s_sc, p_sc, a_sc, *, rows):
    j = pl.program_id(2)
    _, tq, tk = s_sc.shape
    nlb = tk // LANES

    @pl.when(j == 0)
    def _():
        m_sc[...] = jnp.full_like(m_sc, -jnp.inf)
        acc_sc[...] = jnp.zeros_like(acc_sc)

    k = k_ref[...]
    v = v_ref[...]
    v1 = jnp.concatenate([v, jnp.ones_like(v)], axis=1)
    for g in range(GQA_GROUP):
        q = q_ref[:, g * HEAD_DIM:(g + 1) * HEAD_DIM]
        s_sc[g] = lax.dot_general(q, k, (((1,), (1,)), ((), ())), preferred_element_type=F32)

    for g in range(GQA_GROUP):
        for c in range(tq // rows):
            rs = slice(c * rows, (c + 1) * rows)
            blocks = [s_sc[g, rs, b * LANES:(b + 1) * LANES] for b in range(nlb)]
            bmax = blocks[0]
            for blk in blocks[1:]:
                bmax = jnp.maximum(bmax, blk)
            m_prev = m_sc[g, rs, :]
            m_new = jnp.maximum(m_prev, jnp.max(bmax, axis=1, keepdims=True))
            for b, blk in enumerate(blocks):
                p_sc[g, rs, b * LANES:(b + 1) * LANES] = jnp.exp2(blk - m_new).astype(BF16)
            m_sc[g, rs, :] = m_new
            a_sc[g, rs, :] = jnp.exp2(m_prev - m_new)

    for g in range(GQA_GROUP):
        alpha = jnp.concatenate([a_sc[g], a_sc[g]], axis=1)
        acc_sc[g] = alpha * acc_sc[g] + jnp.dot(p_sc[g], v1, preferred_element_type=F32)

    @pl.when(j == pl.num_programs(2) - 1)
    def _():
        for g in range(GQA_GROUP):
            acc = acc_sc[g]
            o_ref[:, g * HEAD_DIM:(g + 1) * HEAD_DIM] = (
                acc[:, :HEAD_DIM] / acc[:, HEAD_DIM:]).astype(o_ref.dtype)


def _attention(q, k, v, q_row0, n_q, k_row0, n_k, tq, tk):
    n_kv = k.shape[1] // HEAD_DIM
    gw = GQA_GROUP * HEAD_DIM
    qb, kb = q_row0 // tq, k_row0 // tk
    return pl.pallas_call(
        functools.partial(_attn_kernel, rows=2 * SUBLANES),
        grid=(n_kv, n_q // tq, n_k // tk),
        in_specs=[pl.BlockSpec((tq, gw), lambda h, i, j: (qb + i, h)),
                  pl.BlockSpec((tk, HEAD_DIM), lambda h, i, j: (kb + j, h)),
                  pl.BlockSpec((tk, HEAD_DIM), lambda h, i, j: (kb + j, h))],
        out_specs=pl.BlockSpec((tq, gw), lambda h, i, j: (i, h)),
        out_shape=jax.ShapeDtypeStruct((n_q, q.shape[1]), BF16),
        scratch_shapes=[pltpu.VMEM((GQA_GROUP, tq, LANES), F32),
                        pltpu.VMEM((GQA_GROUP, tq, 2 * HEAD_DIM), F32),
                        pltpu.VMEM((GQA_GROUP, tq, tk), F32),
                        pltpu.VMEM((GQA_GROUP, tq, tk), BF16),
                        pltpu.VMEM((GQA_GROUP, tq, LANES), F32)],
        compiler_params=_params("parallel", "parallel", "arbitrary"),
        name="attention",
    )(q, k, v)


def _rope_tables(seq, n_ctx):
    rows = seq // GRID_W
    row = jnp.repeat(jnp.arange(rows, dtype=F32), GRID_W)
    col = jnp.tile(jnp.arange(GRID_W, dtype=F32), rows)
    n_pairs = HEAD_DIM // 4
    inv = ROPE_THETA ** (-jnp.arange(n_pairs, dtype=F32) / n_pairs)
    ang = jnp.concatenate([row[:, None] * inv, col[:, None] * inv], axis=-1)
    ang = jnp.concatenate([ang, jnp.zeros((n_ctx, HEAD_DIM // 2), F32)], axis=0)
    c = jnp.repeat(jnp.cos(ang), 2, axis=1)
    s = jnp.repeat(jnp.sin(ang), 2, axis=1)
    even = (jnp.arange(HEAD_DIM) % 2 == 0)[None, :]
    return c, jnp.where(even, -s, 0.0), jnp.where(even, 0.0, s)


def kernel(x, c, ctx, c_ctx, w_mod_down, w_mod_up, b_mod, norm_ffn1, norm_mix, norm_ffn2,
           ffn1_w_in, ffn1_w_out, ffn2_w_in, ffn2_w_out, w_in, lru_conv, lru_w_a, lru_b_a,
           lru_w_x, lru_b_x, lru_lambda, hy_conv, hy_fw1, hy_fb1, hy_freq, hy_fw2, hy_fb2,
           hy_fw3, hy_skip, q_norm, k_norm, w_branch_a, w_branch_b, w_branch_c, w_out, final_norm):
    bsz, seq, d = x.shape
    assert bsz == 1 and c.shape[0] == 1 and ctx.shape[0] == 1
    n_ctx = ctx.shape[1]
    depth = w_in.shape[0]
    wmix = lru_conv.shape[-1]
    kvw = wmix // GQA_GROUP
    t_all = seq + n_ctx
    assert t_all % ROW_TILE == 0 and seq % EW_ROWS == 0 and n_ctx % EW_ROWS == 0

    col_ax = 0
    col_ck = col_ax + wmix
    col_cv = col_ck + kvw
    col_ag = col_cv + kvw
    col_b = col_ag + wmix
    col_cq = col_b + 3 * wmix
    col_g = col_cq + wmix

    xs = jnp.concatenate([x[0], ctx[0]], axis=0)
    cc = jnp.zeros((SUBLANES, d), F32).at[0].set(c[0]).at[1].set(c_ctx)
    mods_all = _modulation(cc, w_mod_down, w_mod_up, b_mod)

    ffn1_in, ffn1_out, ffn2_in, ffn2_out = ffn1_w_in, ffn1_w_out, ffn2_w_in, ffn2_w_out
    w_in_b, wba, wbb, wbc, w_out_b = w_in, w_branch_a, w_branch_b, w_branch_c, w_out

    rope = _rope_tables(seq, n_ctx)
    dft_lat = _dft_tables(seq)
    dft_ctx = _direct_dft_tables(n_ctx)
    q_scale = HEAD_DIM ** -0.5 * math.log2(math.e)
    tq_lat = next(t for t in (512, EW_ROWS) if seq % t == 0)
    tk_lat = next(t for t in (2816, 1408, ROW_TILE) if t_all % t == 0)

    for i in range(depth):
        ctx_out = i < depth - 1
        mods = mods_all[i]
        mods3 = mods.reshape(2 * N_MOD, 1, d)

        u = _norm_mod(xs, norm_ffn1[i], mods3, 0, seq)
        h = _ffn_up(u, ffn1_in, i)
        xs = _down(h, ffn1_out, i, xs, mods[:, 2], 0.5, seq)

        u = _norm_mod(xs, norm_mix[i], mods3, 3, seq)
        p_ax = _proj(u, w_in_b, i, col_ax, wmix, False, F32, "mixer_in_lru")
        p_b = _proj(u, w_in_b, i, col_ag, 4 * wmix, False, F32, "mixer_in_gelu_hyena")
        kh = _head_prep(_proj(u, w_in_b, i, col_ck, kvw, False, F32, "mixer_in_k"),
                        k_norm[i], rope, 1.0, "k_prep")
        vh = _proj(u, w_in_b, i, col_cv, kvw, False, BF16, "mixer_in_v")
        qh = _head_prep(_proj(u, w_in_b, i, col_cq, wmix, False, F32, "mixer_in_q"),
                        q_norm[i], rope, q_scale, "q_prep")
        gates = _proj(u, w_in_b, i, col_g, 3 * d, True, BF16, "mixer_gates")

        lru_args = (lru_conv[i], lru_w_a[i], lru_b_a[i], lru_w_x[i], lru_b_x[i], lru_lambda[i])
        ya_c, h_c = _lru(p_ax, p_b, seq, n_ctx, 0, 0, *lru_args, jnp.zeros((2, wmix), F32))
        ya_l, _ = _lru(p_ax, p_b, 0, seq, 0, 0, *lru_args, h_c)

        hp = {"hy_fw1": hy_fw1[i], "hy_fb1": hy_fb1[i], "hy_freq": hy_freq[i], "hy_fw2": hy_fw2[i],
              "hy_fb2": hy_fb2[i], "hy_fw3": hy_fw3[i], "hy_skip": hy_skip[i]}
        taps_l, ssq_l = _hyena_filter(seq, hp)
        u3_l = _conv3(p_b, 0, seq, wmix, hy_conv[i], wmix)
        yb_l = _hyena_long(u3_l, taps_l, ssq_l, hy_skip[i], dft_lat)

        yc_l = _attention(qh, kh, vh, 0, seq, 0, t_all, tq_lat, tk_lat)

        if ctx_out:
            taps_c, ssq_c = _hyena_filter(n_ctx, hp)
            u3_c = _conv3(p_b, seq, n_ctx, wmix, hy_conv[i], wmix)
            yb_c = _hyena_short(u3_c, taps_c, ssq_c, hy_skip[i], dft_ctx)
            yc_c = _attention(qh, kh, vh, seq, n_ctx, seq, n_ctx, n_ctx, n_ctx)
        else:
            yb_c = jnp.zeros((n_ctx, wmix), BF16)
            yc_c = jnp.zeros((n_ctx, wmix), BF16)

        ya = jnp.concatenate([ya_l, ya_c], axis=0)
        yb = jnp.concatenate([yb_l, yb_c], axis=0)
        yc = jnp.concatenate([yc_l, yc_c], axis=0)
        m = _merge(ya, yb, yc, wba, wbb, wbc, i, gates)
        xs = _down(m, w_out_b, i, xs, mods[:, 5], 1.0, seq)

        u = _norm_mod(xs, norm_ffn2[i], mods3, 6, seq)
        h = _ffn_up(u, ffn2_in, i)
        xs = _down(h, ffn2_out, i, xs, mods[:, 8], 0.5, seq)

    return _final_norm(xs, final_norm, seq)[None]
```
